```python
import jax, jax.numpy as jnp
from jax import lax
import numpy as np


D_MODEL = 1024
BATCH = 8
SEQ = 2048
DEPTH = 1
DEC_BATCH = 32
DEC_SEQ = 4
PAST_LEN = 16384
PAGE_SIZE = 128

HEAD_DIM = 64
H_FOX = (D_MODEL // 2) // HEAD_DIM
H_DSA = (D_MODEL // 2) // HEAD_DIM
W_FOX = H_FOX * HEAD_DIM
W_DSA = H_DSA * HEAD_DIM
MIX_WIDTH = W_FOX + W_DSA
H_IDX = 16
D_IDX = 64
TOPK_MAX = 256
Q_BLOCK = 128
ROPE_THETA = 10000.0
ATTN_SCALE = HEAD_DIM ** -0.5
IDX_SCALE = D_IDX ** -0.5
IDX_W_SCALE = H_IDX ** -0.5
FORGET_BIAS_INIT = 4.0
N_EXPERTS = 32
TOP_K_EXPERTS = 4
D_EXPERT = D_MODEL
SWIGLU_LIMIT = 7.0
SWIGLU_ALPHA = 1.702
EXPERT_BLOCK = 128
LN_EPS = 1e-5
DEEPNORM_ALPHA = (2.0 * DEPTH) ** 0.25
DEEPNORM_BETA = (8.0 * DEPTH) ** -0.25
SPLITS = (W_FOX, W_FOX, W_FOX, H_FOX, W_DSA, W_DSA, W_DSA, H_IDX * D_IDX, D_IDX, H_IDX)
N_IN = sum(SPLITS)

kernel_name = 'hybrid_fox_dsa_moe_step'


def _split_points():
    pts, acc = [], 0
    for w in SPLITS[:-1]:
        acc += w
        pts.append(acc)
    return pts


def rope(x, pos):
    half = x.shape[-1] // 2
    inv = ROPE_THETA ** (-jnp.arange(half, dtype=jnp.float32) / half)
    ang = pos.astype(jnp.float32)[:, None] * inv[None, :]
    cos = jnp.cos(ang)[:, None, :]
    sin = jnp.sin(ang)[:, None, :]
    xf = x.astype(jnp.float32)
    x1, x2 = xf[..., :half], xf[..., half:]
    return jnp.concatenate([x1 * cos - x2 * sin, x2 * cos + x1 * sin], axis=-1).astype(x.dtype)


def layer_norm(z, g, b):
    zf = z.astype(jnp.float32)
    mu = jnp.mean(zf, axis=-1, keepdims=True)
    var = jnp.mean(jnp.square(zf - mu), axis=-1, keepdims=True)
    return ((zf - mu) * lax.rsqrt(var + LN_EPS) * g.astype(jnp.float32) + b.astype(jnp.float32)).astype(z.dtype)


def in_proj(x, w_in, b_f, pos):
    B, S, _ = x.shape
    parts = jnp.split(jnp.einsum('bsd,dn->bsn', x, w_in), _split_points(), axis=-1)
    fq, fk, fv = [p.reshape(B, S, H_FOX, HEAD_DIM) for p in parts[0:3]]
    logf = jax.nn.log_sigmoid(parts[3].astype(jnp.float32) + b_f.astype(jnp.float32))
    dq, dk, dv = [p.reshape(B, S, H_DSA, HEAD_DIM) for p in parts[4:7]]
    dq = rope(dq, pos)
    dk = rope(dk, pos)
    iq = rope(parts[7].reshape(B, S, H_IDX, D_IDX), pos)
    ik = rope(parts[8][:, :, None, :], pos)[:, :, 0, :]
    iw = parts[9] * IDX_W_SCALE
    return fq, fk, fv, logf, dq, dk, dv, iq, ik, iw


def fox_prompt(q, k, v, logf):
    B, S, H, Dh = q.shape
    c = jnp.cumsum(logf, axis=1)
    cT = jnp.transpose(c, (0, 2, 1))
    kpos = jnp.arange(S)

    def block(start):
        qb = lax.dynamic_slice_in_dim(q, start, Q_BLOCK, axis=1)
        cb = lax.dynamic_slice_in_dim(cT, start, Q_BLOCK, axis=2)
        qpos = start + jnp.arange(Q_BLOCK)
        s = jnp.einsum('bqhd,bkhd->bhqk', qb, k).astype(jnp.float32) * ATTN_SCALE
        s = s + cb[:, :, :, None] - cT[:, :, None, :]
        s = jnp.where(kpos[None, :] <= qpos[:, None], s, -jnp.inf)
        p = jax.nn.softmax(s, axis=-1).astype(v.dtype)
        return jnp.einsum('bhqk,bkhd->bqhd', p, v)

    out = lax.map(block, jnp.arange(S // Q_BLOCK) * Q_BLOCK)
    return jnp.moveaxis(out, 0, 1).reshape(B, S, H * Dh)


def dsa_prompt(q, k, v, iq, ik, iw):
    B, S, H, Dh = q.shape
    topk = min(TOPK_MAX, S // 4)
    kpos = jnp.arange(S)
    starts = jnp.arange(S // Q_BLOCK) * Q_BLOCK

    def seq(args):
        q1, k1, v1, iq1, ik1, iw1 = args

        def block(start):
            qb = lax.dynamic_slice_in_dim(q1, start, Q_BLOCK, axis=0)
            iqb = lax.dynamic_slice_in_dim(iq1, start, Q_BLOCK, axis=0)
            iwb = lax.dynamic_slice_in_dim(iw1, start, Q_BLOCK, axis=0)
            qpos = start + jnp.arange(Q_BLOCK)
            rel = jax.nn.relu(jnp.einsum('qhd,kd->qhk', iqb, ik1).astype(jnp.float32) * IDX_SCALE)
            score = jnp.einsum('qh,qhk->qk', iwb.astype(jnp.float32), rel)
            score = jnp.where(kpos[None, :] <= qpos[:, None], score, -jnp.inf)
            _, idx = lax.top_k(score, topk)
            valid = idx <= qpos[:, None]
            ks = k1[idx]
            vs = v1[idx]
            s = jnp.einsum('qhd,qkhd->qhk', qb, ks).astype(jnp.float32) * ATTN_SCALE
            s = jnp.where(valid[:, None, :], s, -jnp.inf)
            p = jax.nn.softmax(s, axis=-1).astype(vs.dtype)
            return jnp.einsum('qhk,qkhd->qhd', p, vs)

        return lax.map(block, starts).reshape(S, H * Dh)

    return lax.map(seq, (q, k, v, iq, ik, iw))


def sample_mixers(l, fq, fk, fv, logf, dq, dk, dv, iq, ik, iw, page_table,
                  cache_fox_k, cache_fox_v, cache_fox_logf, cache_dsa_k, cache_dsa_v, cache_dsa_kidx):
    T = fq.shape[1]
    past = page_table.shape[1] * PAGE_SIZE
    L = past + T
    topk = min(TOPK_MAX, L // 4)
    qpos = past + jnp.arange(T)
    causal = jnp.arange(L)[None, :] <= qpos[:, None]

    def gather_past(pool, pages):
        g = pool[l, pages]
        return g.reshape((past,) + g.shape[2:])

    def one(args):
        pages, fq1, fk1, fv1, lf1, dq1, dk1, dv1, iq1, ik1, iw1 = args
        kf = jnp.concatenate([gather_past(cache_fox_k, pages).astype(fk1.dtype), fk1], axis=0)
        vf = jnp.concatenate([gather_past(cache_fox_v, pages).astype(fv1.dtype), fv1], axis=0)
        c = jnp.cumsum(jnp.concatenate([gather_past(cache_fox_logf, pages).astype(jnp.float32),
                                        lf1.astype(jnp.float32)], axis=0), axis=0)
        s = jnp.einsum('qhd,khd->hqk', fq1, kf).astype(jnp.float32) * ATTN_SCALE
        s = s + c[past:].T[:, :, None] - c.T[:, None, :]
        s = jnp.where(causal[None], s, -jnp.inf)
        p = jax.nn.softmax(s, axis=-1).astype(vf.dtype)
        o_fox = jnp.einsum('hqk,khd->qhd', p, vf).reshape(T, W_FOX)
        kidx = jnp.concatenate([gather_past(cache_dsa_kidx, pages).astype(ik1.dtype), ik1], axis=0)
        rel = jax.nn.relu(jnp.einsum('qhd,kd->qhk', iq1, kidx).astype(jnp.float32) * IDX_SCALE)
        score = jnp.einsum('qh,qhk->qk', iw1.astype(jnp.float32), rel)
        score = jnp.where(causal, score, -jnp.inf)
        _, idx = lax.top_k(score, topk)
        valid = idx <= qpos[:, None]
        in_past = (idx < past)[..., None, None]
        pidx = jnp.minimum(idx, past - 1)
        phys = pages[pidx // PAGE_SIZE]
        off = pidx % PAGE_SIZE
        nidx = jnp.clip(idx - past, 0, T - 1)
        ks = jnp.where(in_past, cache_dsa_k[l, phys, off].astype(dk1.dtype), dk1[nidx])
        vs = jnp.where(in_past, cache_dsa_v[l, phys, off].astype(dv1.dtype), dv1[nidx])
        s2 = jnp.einsum('qhd,qkhd->qhk', dq1, ks).astype(jnp.float32) * ATTN_SCALE
        s2 = jnp.where(valid[:, None, :], s2, -jnp.inf)
        p2 = jax.nn.softmax(s2, axis=-1).astype(vs.dtype)
        o_dsa = jnp.einsum('qhk,qkhd->qhd', p2, vs).reshape(T, W_DSA)
        return jnp.concatenate([o_fox, o_dsa], axis=-1)

    return lax.map(one, (page_table, fq, fk, fv, logf, dq, dk, dv, iq, ik, iw))


def moe(x, w_router, b_router, w_gate, b_gate, w_up, b_up, w_down, b_down):
    shp = x.shape
    D = shp[-1]
    xt = x.reshape(-1, D)
    T = xt.shape[0]
    logits = jnp.einsum('td,de->te', xt, w_router).astype(jnp.float32) + b_router.astype(jnp.float32)
    top_val, top_idx = lax.top_k(logits, TOP_K_EXPERTS)
    gates = jax.nn.softmax(top_val, axis=-1)
    A = T * TOP_K_EXPERTS
    flat_e = top_idx.reshape(-1).astype(jnp.int32)
    flat_tok = jnp.arange(A, dtype=jnp.int32) // TOP_K_EXPERTS
    flat_g = gates.reshape(-1)
    order = jnp.argsort(flat_e, stable=True)
    e_sorted = flat_e[order]
    counts = jnp.zeros((N_EXPERTS,), jnp.int32).at[flat_e].add(1)
    padded = (counts + EXPERT_BLOCK - 1) // EXPERT_BLOCK * EXPERT_BLOCK
    pad_end = jnp.cumsum(padded)
    pad_start = pad_end - padded
    grp_start = jnp.cumsum(counts) - counts
    dest = pad_start[e_sorted] + jnp.arange(A, dtype=jnp.int32) - grp_start[e_sorted]
    n_blocks = -(-(A + N_EXPERTS * (EXPERT_BLOCK - 1)) // EXPERT_BLOCK)
    P = n_blocks * EXPERT_BLOCK
    slot_tok = jnp.full((P,), T, jnp.int32).at[dest].set(flat_tok[order])
    slot_gate = jnp.zeros((P,), jnp.float32).at[dest].set(flat_g[order])
    block_e = jnp.minimum(jnp.searchsorted(pad_end, jnp.arange(n_blocks, dtype=jnp.int32) * EXPERT_BLOCK,
                                           side='right'), N_EXPERTS - 1)
    x_pad = jnp.concatenate([xt, jnp.zeros((1, D), xt.dtype)], axis=0)
    xs = x_pad[slot_tok].reshape(n_blocks, EXPERT_BLOCK, D)

    def expert_block(args):
        xb, e = args
        g = xb @ w_gate[e] + b_gate[e]
        u = xb @ w_up[e] + b_up[e]
        g = jnp.minimum(g, SWIGLU_LIMIT)
        u = jnp.clip(u, -SWIGLU_LIMIT, SWIGLU_LIMIT)
        h = g * jax.nn.sigmoid(SWIGLU_ALPHA * g) * (u + 1.0)
        return h @ w_down[e] + b_down[e]

    ys = lax.map(expert_block, (xs, block_e))
    contrib = ys.reshape(P, D) * slot_gate[:, None].astype(ys.dtype)
    out = jnp.zeros((T + 1, D), ys.dtype).at[slot_tok].add(contrib)[:T]
    return out.reshape(shp).astype(x.dtype)


def layer_tail(x, o, w_o, ln1_g, ln1_b, w_router, b_router, w_gate, b_gate, w_up, b_up, w_down, b_down,
               ln2_g, ln2_b):
    x = layer_norm(DEEPNORM_ALPHA * x + jnp.einsum('bsm,md->bsd', o, w_o), ln1_g, ln1_b)
    x = layer_norm(DEEPNORM_ALPHA * x + moe(x, w_router, b_router, w_gate, b_gate, w_up, b_up, w_down, b_down),
                   ln2_g, ln2_b)
    return x


def setup_inputs(seed: int = 0) -> dict:
    key = jax.random.key(seed)
    ks = jax.random.split(key, 32)
    f32 = jnp.float32
    n_pages = PAST_LEN // PAGE_SIZE
    n_pool = (DEC_BATCH * n_pages * 5) // 4

    def nrm(k, shape, scale):
        return jax.random.normal(k, shape, f32) * scale

    return {
        'x_prompt': nrm(ks[0], (BATCH, SEQ, D_MODEL), 1.0),
        'x_sample': nrm(ks[1], (DEC_BATCH, DEC_SEQ, D_MODEL), 1.0),
        'cache_fox_k': nrm(ks[2], (DEPTH, n_pool, PAGE_SIZE, H_FOX, HEAD_DIM), 1.0),
        'cache_fox_v': nrm(ks[3], (DEPTH, n_pool, PAGE_SIZE, H_FOX, HEAD_DIM), 1.0),
        'cache_fox_logf': jax.nn.log_sigmoid(FORGET_BIAS_INIT + nrm(ks[4], (DEPTH, n_pool, PAGE_SIZE, H_FOX), 1.0)),
        'cache_dsa_k': nrm(ks[5], (DEPTH, n_pool, PAGE_SIZE, H_DSA, HEAD_DIM), 1.0),
        'cache_dsa_v': nrm(ks[6], (DEPTH, n_pool, PAGE_SIZE, H_DSA, HEAD_DIM), 1.0),
        'cache_dsa_kidx': nrm(ks[7], (DEPTH, n_pool, PAGE_SIZE, D_IDX), 1.0),
        'page_table': jax.random.permutation(ks[8], n_pool)[:DEC_BATCH * n_pages]
                      .reshape(DEC_BATCH, n_pages).astype(jnp.int32),
        'w_in': nrm(ks[9], (DEPTH, D_MODEL, N_IN), D_MODEL ** -0.5),
        'b_f': FORGET_BIAS_INIT + nrm(ks[10], (DEPTH, H_FOX), 0.1),
        'w_o': nrm(ks[11], (DEPTH, MIX_WIDTH, D_MODEL), MIX_WIDTH ** -0.5 * DEEPNORM_BETA),
        'ln1_g': 1.0 + nrm(ks[12], (DEPTH, D_MODEL), 0.05),
        'ln1_b': nrm(ks[13], (DEPTH, D_MODEL), 0.05),
        'w_router': nrm(ks[14], (DEPTH, D_MODEL, N_EXPERTS), D_MODEL ** -0.5),
        'b_router': nrm(ks[15], (DEPTH, N_EXPERTS), 0.01),
        'w_gate': nrm(ks[16], (DEPTH, N_EXPERTS, D_MODEL, D_EXPERT), D_MODEL ** -0.5),
        'b_gate': nrm(ks[17], (DEPTH, N_EXPERTS, D_EXPERT), 0.01),
        'w_up': nrm(ks[18], (DEPTH, N_EXPERTS, D_MODEL, D_EXPERT), D_MODEL ** -0.5),
        'b_up': nrm(ks[19], (DEPTH, N_EXPERTS, D_EXPERT), 0.01),
        'w_down': nrm(ks[20], (DEPTH, N_EXPERTS, D_EXPERT, D_MODEL), D_EXPERT ** -0.5 * DEEPNORM_BETA),
        'b_down': nrm(ks[21], (DEPTH, N_EXPERTS, D_MODEL), 0.01),
        'ln2_g': 1.0 + nrm(ks[22], (DEPTH, D_MODEL), 0.05),
        'ln2_b': nrm(ks[23], (DEPTH, D_MODEL), 0.05),
    }


def reference(x_prompt, x_sample, cache_fox_k, cache_fox_v, cache_fox_logf, cache_dsa_k, cache_dsa_v,
              cache_dsa_kidx, page_table, w_in, b_f, w_o, ln1_g, ln1_b, w_router, b_router, w_gate, b_gate,
              w_up, b_up, w_down, b_down, ln2_g, ln2_b):
    S = x_prompt.shape[1]
    pos_p = jnp.arange(S)
    past = page_table.shape[1] * PAGE_SIZE
    pos_s = past + jnp.arange(x_sample.shape[1])
    xp, xs = x_prompt, x_sample
    rows_p = [[] for _ in range(6)]
    rows_s = [[] for _ in range(6)]
    for l in range(DEPTH):
        tail = (w_o[l], ln1_g[l], ln1_b[l], w_router[l], b_router[l], w_gate[l], b_gate[l], w_up[l], b_up[l],
                w_down[l], b_down[l], ln2_g[l], ln2_b[l])
        fq, fk, fv, lf, dq, dk, dv, iq, ik, iw = in_proj(xp, w_in[l], b_f[l], pos_p)
        o_p = jnp.concatenate([fox_prompt(fq, fk, fv, lf), dsa_prompt(dq, dk, dv, iq, ik, iw)], axis=-1)
        xp = layer_tail(xp, o_p, *tail)
        for r, a in zip(rows_p, (fk, fv, lf, dk, dv, ik)):
            r.append(a)
        sq, sk, sv, slf, sdq, sdk, sdv, siq, sik, siw = in_proj(xs, w_in[l], b_f[l], pos_s)
        o_s = sample_mixers(l, sq, sk, sv, slf, sdq, sdk, sdv, siq, sik, siw, page_table,
                            cache_fox_k, cache_fox_v, cache_fox_logf, cache_dsa_k, cache_dsa_v, cache_dsa_kidx)
        xs = layer_tail(xs, o_s, *tail)
        for r, a in zip(rows_s, (sk, sv, slf, sdk, sdv, sik)):
            r.append(a)
    p_fox_k, p_fox_v, p_fox_logf, p_dsa_k, p_dsa_v, p_dsa_kidx = [jnp.stack(r, axis=0) for r in rows_p]
    s_fox_k, s_fox_v, s_fox_logf, s_dsa_k, s_dsa_v, s_dsa_kidx = [jnp.stack(r, axis=0) for r in rows_s]
    return (xp, xs, p_fox_k, p_fox_v, p_fox_logf, p_dsa_k, p_dsa_v, p_dsa_kidx,
            s_fox_k, s_fox_v, s_fox_logf, s_dsa_k, s_dsa_v, s_dsa_kidx)
```

```python
import functools

import jax
import jax.numpy as jnp
from jax import lax
from jax.experimental import pallas as pl
from jax.experimental.pallas import tpu as pltpu

F32, BF16, I32 = jnp.float32, jnp.bfloat16, jnp.int32
NEG_INF = float("-inf")

HEAD_DIM = 64
H_FOX = 8
H_DSA = 8
W_FOX = H_FOX * HEAD_DIM
W_DSA = H_DSA * HEAD_DIM
H_IDX = 16
D_IDX = 64
TOPK_MAX = 256
PAGE_SIZE = 128
ROPE_THETA = 10000.0
ATTN_SCALE = HEAD_DIM ** -0.5
IDX_SCALE = D_IDX ** -0.5
IDX_W_SCALE = H_IDX ** -0.5
N_EXPERTS = 32
TOP_K_EXPERTS = 4
SWIGLU_LIMIT = 7.0
SWIGLU_ALPHA = 1.702
LN_EPS = 1e-5

LANES = 128
KEY_NEG_INF = -2139095041
INT_MIN = -2 ** 31
VMEM_LIMIT = 56 * 1024 * 1024

NT_DIMS = (((1,), (1,)), ((), ()))


def _cparams(sem):
    return pltpu.CompilerParams(dimension_semantics=sem, vmem_limit_bytes=VMEM_LIMIT)


def _split3(x):
    hi = x.astype(BF16)
    r1 = x - hi.astype(F32)
    mid = r1.astype(BF16)
    lo = (r1 - mid.astype(F32)).astype(BF16)
    return hi, mid, lo


def _lane_cumsum(x, upper):
    hi, mid, lo = _split3(x)
    return (jnp.dot(hi, upper, preferred_element_type=F32)
            + jnp.dot(mid, upper, preferred_element_type=F32)
            + jnp.dot(lo, upper, preferred_element_type=F32))


def _upper_tri(n):
    r = lax.broadcasted_iota(I32, (n, n), 0)
    c = lax.broadcasted_iota(I32, (n, n), 1)
    return (r <= c).astype(BF16)


def _sort_key(x):
    bits = pltpu.bitcast(x, I32)
    return bits ^ ((bits >> 31) & 0x7FFFFFFF)


C_FQ, C_FK, C_FV, C_DQ, C_DK, C_DV, C_IQ, C_SM = 0, 512, 1024, 1536, 2048, 2560, 3072, 4096
N_PROJ = 4224
SM_IK, SM_LF, SM_IW = 0, 64, 72


def _in_proj_kernel(x_ref, w_ref, cos_ref, sin_ref, bf_ref,
                    fq_ref, fk_ref, fkb_ref, fv_ref, fvb_ref, lf_ref, lft_ref,
                    dq_ref, dk_ref, dkb_ref, dv_ref, dvb_ref, iq_ref, ik_ref, ikb_ref, iw_ref):
    xb = x_ref[...].astype(BF16)
    tm = xb.shape[0]
    cos = cos_ref[...]
    sin = sin_ref[...]
    lane = lax.broadcasted_iota(I32, (tm, LANES), 1)
    first_half = (lane & (HEAD_DIM - 1)) < HEAD_DIM // 2

    def mm(c0, n):
        return jnp.dot(xb, w_ref[:, c0:c0 + n], preferred_element_type=F32)

    def rope(y):
        outs = []
        for c in range(0, y.shape[1], LANES):
            yc = y[:, c:c + LANES]
            partner = jnp.where(first_half, pltpu.roll(yc, LANES - HEAD_DIM // 2, 1),
                                pltpu.roll(yc, HEAD_DIM // 2, 1))
            outs.append(yc * cos + partner * sin)
        return outs[0] if len(outs) == 1 else jnp.concatenate(outs, axis=1)

    y = mm(C_FQ, W_FOX)
    fq_ref[...] = (y * ATTN_SCALE).astype(BF16)
    y = mm(C_FK, W_FOX)
    fk_ref[...] = y
    fkb_ref[...] = y.astype(BF16)
    y = mm(C_FV, W_FOX)
    fv_ref[...] = y
    fvb_ref[...] = y.astype(BF16)
    y = rope(mm(C_DQ, W_DSA))
    dq_ref[...] = (y * ATTN_SCALE).astype(BF16)
    y = rope(mm(C_DK, W_DSA))
    dk_ref[...] = y
    dkb_ref[...] = y.astype(BF16)
    y = mm(C_DV, W_DSA)
    dv_ref[...] = y
    dvb_ref[...] = y.astype(BF16)
    y = rope(mm(C_IQ, H_IDX * D_IDX))
    iq_ref[...] = (y * IDX_SCALE).astype(BF16)
    z = mm(C_SM, LANES)
    r = rope(z)
    ik_ref[...] = r[:, SM_IK:SM_IK + D_IDX]
    ikb_ref[...] = r[:, SM_IK:SM_IK + D_IDX].astype(BF16)
    zb = z + bf_ref[...]
    lsg = -(jnp.maximum(-zb, 0.0) + jnp.log1p(jnp.exp(-jnp.abs(zb))))
    lf_ref[...] = lsg[:, SM_LF:SM_LF + H_FOX]
    lft_ref[...] = lsg.T[SM_LF:SM_LF + H_FOX, :]
    iw_ref[...] = z[:, SM_IW:SM_IW + H_IDX] * IDX_W_SCALE


def _in_proj(x2d, wp, bf_row, cos_t, sin_t, tm):
    rows, d = x2d.shape
    n_tab = cos_t.shape[0] // tm
    outs = [("fq", W_FOX, BF16), ("fk", W_FOX, F32), ("fkb", W_FOX, BF16), ("fv", W_FOX, F32),
            ("fvb", W_FOX, BF16), ("lf", H_FOX, F32), ("lft", None, F32),
            ("dq", W_DSA, BF16), ("dk", W_DSA, F32), ("dkb", W_DSA, BF16), ("dv", W_DSA, F32),
            ("dvb", W_DSA, BF16), ("iq", H_IDX * D_IDX, BF16), ("ik", D_IDX, F32),
            ("ikb", D_IDX, BF16), ("iw", H_IDX, F32)]
    out_shape, out_specs = [], []
    for name, w, dt in outs:
        if name == "lft":
            out_shape.append(jax.ShapeDtypeStruct((H_FOX, rows), dt))
            out_specs.append(pl.BlockSpec((H_FOX, tm), lambda i: (0, i)))
        else:
            out_shape.append(jax.ShapeDtypeStruct((rows, w), dt))
            out_specs.append(pl.BlockSpec((tm, w), lambda i: (i, 0)))
    res = pl.pallas_call(
        _in_proj_kernel,
        grid=(rows // tm,),
        in_specs=[pl.BlockSpec((tm, d), lambda i: (i, 0)),
                  pl.BlockSpec((d, N_PROJ), lambda i: (0, 0)),
                  pl.BlockSpec((tm, LANES), lambda i: (i % n_tab, 0)),
                  pl.BlockSpec((tm, LANES), lambda i: (i % n_tab, 0)),
                  pl.BlockSpec((1, LANES), lambda i: (0, 0))],
        out_specs=out_specs,
        out_shape=out_shape,
        compiler_params=_cparams(("arbitrary",)),
        name="in_proj",
    )(x2d, wp, cos_t, sin_t, bf_row)
    return {name: r for (name, _, _), r in zip(outs, res)}


def _fox_kernel(q_ref, k_ref, v_ref, lft_ref, o_ref, c_ref, *, tq, nk):
    i = pl.program_id(1)

    @pl.when(i == 0)
    def _():
        upper = _upper_tri(tq)
        carry = jnp.zeros((H_FOX, 1), F32)
        for j in range(nk):
            cs = _lane_cumsum(lft_ref[:, j * tq:(j + 1) * tq], upper) + carry
            c_ref[j] = cs
            carry = cs[:, tq - 1:tq]

    row = lax.broadcasted_iota(I32, (tq, tq), 0)
    col = lax.broadcasted_iota(I32, (tq, tq), 1)
    causal = col <= row

    for h in range(H_FOX):
        lo, hi = h * HEAD_DIM, (h + 1) * HEAD_DIM
        qh = q_ref[:, lo:hi]

        def kv_step(j, carry, masked, lo=lo, hi=hi, qh=qh, h=h):
            m, l, acc = carry
            start = pl.multiple_of(j * tq, tq)
            kh = k_ref[pl.ds(start, tq), lo:hi]
            vh = v_ref[pl.ds(start, tq), lo:hi]
            s = lax.dot_general(qh, kh, NT_DIMS, preferred_element_type=F32)
            s = s - c_ref[j, pl.ds(h, 1), :]
            if masked:
                s = jnp.where(causal, s, NEG_INF)
            m_new = jnp.maximum(m, jnp.max(s, axis=1, keepdims=True))
            alpha = jnp.exp(m - m_new)
            p = jnp.exp(s - m_new)
            l = alpha * l + jnp.sum(p, axis=1, keepdims=True)
            acc = alpha * acc + jnp.dot(p.astype(BF16), vh, preferred_element_type=F32)
            return m_new, l, acc

        init = (jnp.full((tq, 1), -1e30, F32), jnp.zeros((tq, 1), F32), jnp.zeros((tq, HEAD_DIM), F32))
        carry = lax.fori_loop(0, i, lambda j, c: kv_step(j, c, False), init)
        m, l, acc = kv_step(i, carry, True)
        o_ref[:, lo:hi] = (acc / l).astype(BF16)


def _fox_prompt(qb, kb, vb, lft, batch, seq, tq):
    nq = seq // tq
    return pl.pallas_call(
        functools.partial(_fox_kernel, tq=tq, nk=nq),
        grid=(batch, nq),
        in_specs=[pl.BlockSpec((tq, W_FOX), lambda b, i: (b * nq + i, 0)),
                  pl.BlockSpec((seq, W_FOX), lambda b, i: (b, 0)),
                  pl.BlockSpec((seq, W_FOX), lambda b, i: (b, 0)),
                  pl.BlockSpec((H_FOX, seq), lambda b, i: (0, b))],
        out_specs=pl.BlockSpec((tq, W_FOX), lambda b, i: (b * nq + i, 0)),
        out_shape=jax.ShapeDtypeStruct((batch * seq, W_FOX), BF16),
        scratch_shapes=[pltpu.VMEM((nq, H_FOX, tq), F32)],
        compiler_params=_cparams(("arbitrary", "arbitrary")),
        name="fox_prompt",
    )(qb, kb, vb, lft)


def _kth_largest_key(key_ref, n_chunks, rows, topk):
    def bit_step(bi, thr):
        cand = thr + lax.shift_left(jnp.int32(1), 31 - bi)

        def count(c, cnt):
            return cnt + jnp.sum((key_ref[c] >= cand).astype(F32), axis=1, keepdims=True)

        cnt = lax.fori_loop(0, n_chunks, count, jnp.zeros((rows, 1), F32))
        return jnp.where(cnt >= topk, cand, thr)

    return lax.fori_loop(0, 32, bit_step, jnp.full((rows, 1), INT_MIN, I32))


def _write_select_bias(key_ref, bias_ref, n_chunks, rows, width, topk, thr):
    def counts(c, carry):
        gt, eq = carry
        k = key_ref[c]
        return (gt + jnp.sum((k > thr).astype(F32), axis=1, keepdims=True),
                eq + jnp.sum((k == thr).astype(F32), axis=1, keepdims=True))

    zero = jnp.zeros((rows, 1), F32)
    gt, eq = lax.fori_loop(0, n_chunks, counts, (zero, zero))
    need = topk - gt
    excess = jnp.max(jnp.where((eq > need) & (thr > KEY_NEG_INF), 1.0, 0.0)) > 0.0

    @pl.when(jnp.logical_not(excess))
    def _():
        def fill(c, _):
            k = key_ref[c]
            bias_ref[c] = jnp.where((k >= thr) & (k > KEY_NEG_INF), 0.0, NEG_INF)
            return 0
        lax.fori_loop(0, n_chunks, fill, 0)

    @pl.when(excess)
    def _():
        upper = _upper_tri(width)

        def fill(c, before):
            k = key_ref[c]
            tie = k == thr
            incl = jnp.dot(tie.astype(BF16), upper, preferred_element_type=F32) + before
            excl = incl - tie.astype(F32)
            sel = (k > thr) | (tie & (excl < need))
            bias_ref[c] = jnp.where(sel & (k > KEY_NEG_INF), 0.0, NEG_INF)
            return incl[:, width - 1:width]
        lax.fori_loop(0, n_chunks, fill, zero)


def _dsa_kernel(iq_ref, iw_ref, ik_ref, q_ref, k_ref, v_ref, o_ref, key_ref, bias_ref, *, tq, ck, topk):
    i = pl.program_id(1)
    q0 = i * tq
    n_chunks = (q0 + tq + ck - 1) // ck
    row = q0 + lax.broadcasted_iota(I32, (tq, ck), 0)
    col0 = lax.broadcasted_iota(I32, (tq, ck), 1)
    iw = iw_ref[...]

    def score_chunk(c, _):
        start = pl.multiple_of(c * ck, ck)
        ikc = ik_ref[pl.ds(start, ck), :]
        sc = jnp.zeros((tq, ck), F32)
        for h in range(H_IDX):
            rel = lax.dot_general(iq_ref[:, h * D_IDX:(h + 1) * D_IDX], ikc, NT_DIMS,
                                  preferred_element_type=F32)
            sc = sc + iw[:, h:h + 1] * jnp.maximum(rel, 0.0)
        sc = jnp.where(col0 + start <= row, sc, NEG_INF)
        key_ref[c] = _sort_key(sc)
        return 0

    lax.fori_loop(0, n_chunks, score_chunk, 0)
    thr = _kth_largest_key(key_ref, n_chunks, tq, topk)
    _write_select_bias(key_ref, bias_ref, n_chunks, tq, ck, topk, thr)

    for h in range(H_DSA):
        lo, hi = h * HEAD_DIM, (h + 1) * HEAD_DIM
        qh = q_ref[:, lo:hi]

        def kv_step(c, carry, lo=lo, hi=hi, qh=qh):
            m, l, acc = carry
            start = pl.multiple_of(c * ck, ck)
            kh = k_ref[pl.ds(start, ck), lo:hi]
            vh = v_ref[pl.ds(start, ck), lo:hi]
            s = lax.dot_general(qh, kh, NT_DIMS, preferred_element_type=F32) + bias_ref[c]
            m_new = jnp.maximum(m, jnp.max(s, axis=1, keepdims=True))
            alpha = jnp.exp(m - m_new)
            p = jnp.exp(s - m_new)
            l = alpha * l + jnp.sum(p, axis=1, keepdims=True)
            acc = alpha * acc + jnp.dot(p.astype(BF16), vh, preferred_element_type=F32)
            return m_new, l, acc

        init = (jnp.full((tq, 1), -1e30, F32), jnp.zeros((tq, 1), F32), jnp.zeros((tq, HEAD_DIM), F32))
        m, l, acc = lax.fori_loop(0, n_chunks, kv_step, init)
        o_ref[:, lo:hi] = (acc / l).astype(BF16)


def _dsa_prompt(iq, iw, ikb, dq, dkb, dvb, batch, seq, tq, ck):
    nq = seq // tq
    topk = min(TOPK_MAX, seq // 4)
    return pl.pallas_call(
        functools.partial(_dsa_kernel, tq=tq, ck=ck, topk=topk),
        grid=(batch, nq),
        in_specs=[pl.BlockSpec((tq, H_IDX * D_IDX), lambda b, i: (b * nq + i, 0)),
                  pl.BlockSpec((tq, H_IDX), lambda b, i: (b * nq + i, 0)),
                  pl.BlockSpec((seq, D_IDX), lambda b, i: (b, 0)),
                  pl.BlockSpec((tq, W_DSA), lambda b, i: (b * nq + i, 0)),
                  pl.BlockSpec((seq, W_DSA), lambda b, i: (b, 0)),
                  pl.BlockSpec((seq, W_DSA), lambda b, i: (b, 0))],
        out_specs=pl.BlockSpec((tq, W_DSA), lambda b, i: (b * nq + i, 0)),
        out_shape=jax.ShapeDtypeStruct((batch * seq, W_DSA), BF16),
        scratch_shapes=[pltpu.VMEM((seq // ck, tq, ck), I32), pltpu.VMEM((seq // ck, tq, ck), F32)],
        compiler_params=_cparams(("arbitrary", "arbitrary")),
        name="dsa_prompt",
    )(iq, iw, ikb, dq, dkb, dvb)


PAGES_PER_STEP = 8


def _head_mask(rows, width):
    r = lax.broadcasted_iota(I32, (rows, width), 0)
    c = lax.broadcasted_iota(I32, (rows, width), 1)
    return (c // HEAD_DIM) == (r % H_FOX)


def _block_diag_q(q, n_q):
    hm = _head_mask(H_FOX, q.shape[1])
    qf = q.astype(F32)
    parts = [jnp.where(hm, jnp.broadcast_to(qf[t:t + 1, :], hm.shape), 0.0) for t in range(n_q)]
    return jnp.concatenate(parts, axis=0).astype(BF16)


def _rows_to_groups(x, n_q):
    return jnp.concatenate([jnp.broadcast_to(x[t:t + 1, :], (H_FOX, x.shape[1])) for t in range(n_q)], axis=0)


def _softmax_page(s, v_bf16, m_ref, l_ref, acc_ref):
    m = m_ref[...]
    m_new = jnp.maximum(m, jnp.max(s, axis=1, keepdims=True))
    alpha = jnp.exp(m - m_new)
    p = jnp.exp(s - m_new)
    l_ref[...] = alpha * l_ref[...] + jnp.sum(p, axis=1, keepdims=True)
    acc_ref[...] = alpha * acc_ref[...] + jnp.dot(p.astype(BF16), v_bf16, preferred_element_type=F32)
    m_ref[...] = m_new


def _finish_decode(o_ref, l_ref, acc_ref, n_q):
    o = acc_ref[...] / l_ref[...]
    o = jnp.where(_head_mask(o.shape[0], o.shape[1]), o, 0.0)
    o_ref[...] = jnp.sum(o.reshape(n_q, H_FOX, o.shape[1]), axis=1).astype(BF16)


def _sample_fox_kernel(pt_ref, q_ref, kn_ref, vn_ref, lfn_ref, *rest, n_q, pps):
    k_refs, v_refs, lf_refs = rest[:pps], rest[pps:2 * pps], rest[2 * pps:3 * pps]
    o_ref, m_ref, l_ref, acc_ref, cc_ref, qbd_ref = rest[3 * pps:]
    step = pl.program_id(1)
    rows = n_q * H_FOX

    @pl.when(step == 0)
    def _():
        m_ref[...] = jnp.full(m_ref.shape, -1e30, F32)
        l_ref[...] = jnp.zeros(l_ref.shape, F32)
        acc_ref[...] = jnp.zeros(acc_ref.shape, F32)
        cc_ref[...] = jnp.zeros(cc_ref.shape, F32)
        qbd_ref[...] = _block_diag_q(q_ref[...], n_q)

    upper = _upper_tri(PAGE_SIZE)

    def page(k_ref, v_ref, lf_ref, mask):
        c = _lane_cumsum(lf_ref[...], upper) + cc_ref[...]
        cc_ref[...] = c[:, PAGE_SIZE - 1:PAGE_SIZE]
        s = lax.dot_general(qbd_ref[...], k_ref[...].astype(BF16), NT_DIMS, preferred_element_type=F32)
        s = s - jnp.concatenate([c] * n_q, axis=0)
        if mask is not None:
            s = jnp.where(mask, s, NEG_INF)
        _softmax_page(s, v_ref[...].astype(BF16), m_ref, l_ref, acc_ref)

    for p in range(pps):
        page(k_refs[p], v_refs[p], lf_refs[p], None)

    @pl.when(step == pl.num_programs(1) - 1)
    def _():
        r = lax.broadcasted_iota(I32, (rows, PAGE_SIZE), 0)
        c = lax.broadcasted_iota(I32, (rows, PAGE_SIZE), 1)
        page(kn_ref, vn_ref, lfn_ref, c <= r // H_FOX)
        _finish_decode(o_ref, l_ref, acc_ref, n_q)


def _page_spec(block, p, pps):
    nd = len(block)
    return pl.BlockSpec((None,) + block, lambda b, s, pt: (pt[b, s * pps + p],) + (0,) * nd)


def _sample_fox(page_table, q, k_new, v_new, lft_new, pool_k, pool_v, pool_lft):
    dec_b, n_q, w = q.shape
    n_pages = page_table.shape[1]
    pps = PAGES_PER_STEP
    rows = n_q * H_FOX
    seq_spec = lambda blk: pl.BlockSpec((None,) + blk, lambda b, s, pt: (b,) + (0,) * len(blk))
    in_specs = ([seq_spec((n_q, w)), seq_spec((PAGE_SIZE, w)), seq_spec((PAGE_SIZE, w)),
                 seq_spec((H_FOX, PAGE_SIZE))]
                + [_page_spec((PAGE_SIZE, w), p, pps) for p in range(pps)]
                + [_page_spec((PAGE_SIZE, w), p, pps) for p in range(pps)]
                + [_page_spec((H_FOX, PAGE_SIZE), p, pps) for p in range(pps)])
    return pl.pallas_call(
        functools.partial(_sample_fox_kernel, n_q=n_q, pps=pps),
        grid_spec=pltpu.PrefetchScalarGridSpec(
            num_scalar_prefetch=1,
            grid=(dec_b, n_pages // pps),
            in_specs=in_specs,
            out_specs=pl.BlockSpec((None, n_q, w), lambda b, s, pt: (b, 0, 0)),
            scratch_shapes=[pltpu.VMEM((rows, 1), F32), pltpu.VMEM((rows, 1), F32),
                            pltpu.VMEM((rows, w), F32), pltpu.VMEM((H_FOX, 1), F32),
                            pltpu.VMEM((rows, w), BF16)]),
        out_shape=jax.ShapeDtypeStruct((dec_b, n_q, w), BF16),
        compiler_params=_cparams(("arbitrary", "arbitrary")),
        name="sample_fox",
    )(page_table, q, k_new, v_new, lft_new, *([pool_k] * pps), *([pool_v] * pps), *([pool_lft] * pps))


def _sample_index_kernel(pt_ref, iq_ref, iw_ref, kn_ref, *rest, pps, n_q, n_chunks, topk):
    k_refs = rest[:pps]
    bias_ref, key_ref = rest[pps:]
    step = pl.program_id(1)
    iq = iq_ref[...]
    iw = iw_ref[...]

    def scores(k_ref):
        rel = lax.dot_general(iq, k_ref[...].astype(BF16), NT_DIMS, preferred_element_type=F32)
        w = iw * jnp.maximum(rel, 0.0)
        return jnp.sum(w.reshape(H_IDX, H_FOX, PAGE_SIZE), axis=0)

    for p in range(pps):
        key_ref[step * pps + p] = _sort_key(scores(k_refs[p]))

    @pl.when(step == pl.num_programs(1) - 1)
    def _():
        r = lax.broadcasted_iota(I32, (H_FOX, PAGE_SIZE), 0)
        c = lax.broadcasted_iota(I32, (H_FOX, PAGE_SIZE), 1)
        sc = jnp.where((c <= r) & (r < n_q), scores(kn_ref), NEG_INF)
        key_ref[n_chunks - 1] = _sort_key(sc)
        thr = _kth_largest_key(key_ref, n_chunks, H_FOX, topk)
        _write_select_bias(key_ref, bias_ref, n_chunks, H_FOX, PAGE_SIZE, topk, thr)


def _sample_index(page_table, iqx, iwx, kidx_new, pool_kidx, n_q):
    dec_b = iqx.shape[0]
    n_pages = page_table.shape[1]
    pps = PAGES_PER_STEP
    n_chunks = n_pages + 1
    topk = min(TOPK_MAX, (n_pages * PAGE_SIZE + n_q) // 4)
    seq_spec = lambda blk: pl.BlockSpec((None,) + blk, lambda b, s, pt: (b,) + (0,) * len(blk))
    in_specs = ([seq_spec((H_IDX * H_FOX, D_IDX)), seq_spec((H_IDX * H_FOX, 1)), seq_spec((PAGE_SIZE, D_IDX))]
                + [_page_spec((PAGE_SIZE, D_IDX), p, pps) for p in range(pps)])
    return pl.pallas_call(
        functools.partial(_sample_index_kernel, pps=pps, n_q=n_q, n_chunks=n_chunks, topk=topk),
        grid_spec=pltpu.PrefetchScalarGridSpec(
            num_scalar_prefetch=1,
            grid=(dec_b, n_pages // pps),
            in_specs=in_specs,
            out_specs=pl.BlockSpec((None, n_chunks, H_FOX, PAGE_SIZE), lambda b, s, pt: (b, 0, 0, 0)),
            scratch_shapes=[pltpu.VMEM((n_chunks, H_FOX, PAGE_SIZE), I32)]),
        out_shape=jax.ShapeDtypeStruct((dec_b, n_chunks, H_FOX, PAGE_SIZE), F32),
        compiler_params=_cparams(("arbitrary", "arbitrary")),
        name="sample_index",
    )(page_table, iqx, iwx, kidx_new, *([pool_kidx] * pps))


def _sample_dsa_kernel(pt_ref, q_ref, kn_ref, vn_ref, bias_ref, biasn_ref, *rest, n_q, pps):
    k_refs, v_refs = rest[:pps], rest[pps:2 * pps]
    o_ref, m_ref, l_ref, acc_ref, qbd_ref = rest[2 * pps:]
    step = pl.program_id(1)

    @pl.when(step == 0)
    def _():
        m_ref[...] = jnp.full(m_ref.shape, -1e30, F32)
        l_ref[...] = jnp.zeros(l_ref.shape, F32)
        acc_ref[...] = jnp.zeros(acc_ref.shape, F32)
        qbd_ref[...] = _block_diag_q(q_ref[...], n_q)

    def page(k_ref, v_ref, bias):
        s = lax.dot_general(qbd_ref[...], k_ref[...].astype(BF16), NT_DIMS, preferred_element_type=F32)
        s = s + _rows_to_groups(bias, n_q)
        _softmax_page(s, v_ref[...].astype(BF16), m_ref, l_ref, acc_ref)

    for p in range(pps):
        page(k_refs[p], v_refs[p], bias_ref[p])

    @pl.when(step == pl.num_programs(1) - 1)
    def _():
        page(kn_ref, vn_ref, biasn_ref[0])
        _finish_decode(o_ref, l_ref, acc_ref, n_q)


def _sample_dsa(page_table, q, k_new, v_new, bias, pool_k, pool_v):
    dec_b, n_q, w = q.shape
    n_pages = page_table.shape[1]
    pps = PAGES_PER_STEP
    rows = n_q * H_DSA
    seq_spec = lambda blk: pl.BlockSpec((None,) + blk, lambda b, s, pt: (b,) + (0,) * len(blk))
    in_specs = ([seq_spec((n_q, w)), seq_spec((PAGE_SIZE, w)), seq_spec((PAGE_SIZE, w)),
                 pl.BlockSpec((None, pps, H_FOX, PAGE_SIZE), lambda b, s, pt: (b, s, 0, 0)),
                 pl.BlockSpec((None, 1, H_FOX, PAGE_SIZE), lambda b, s, pt: (b, n_pages, 0, 0))]
                + [_page_spec((PAGE_SIZE, w), p, pps) for p in range(pps)]
                + [_page_spec((PAGE_SIZE, w), p, pps) for p in range(pps)])
    return pl.pallas_call(
        functools.partial(_sample_dsa_kernel, n_q=n_q, pps=pps),
        grid_spec=pltpu.PrefetchScalarGridSpec(
            num_scalar_prefetch=1,
            grid=(dec_b, n_pages // pps),
            in_specs=in_specs,
            out_specs=pl.BlockSpec((None, n_q, w), lambda b, s, pt: (b, 0, 0)),
            scratch_shapes=[pltpu.VMEM((rows, 1), F32), pltpu.VMEM((rows, 1), F32),
                            pltpu.VMEM((rows, w), F32), pltpu.VMEM((rows, w), BF16)]),
        out_shape=jax.ShapeDtypeStruct((dec_b, n_q, w), BF16),
        compiler_params=_cparams(("arbitrary", "arbitrary")),
        name="sample_dsa",
    )(page_table, q, k_new, v_new, bias, bias, *([pool_k] * pps), *([pool_v] * pps))


def _layer_norm(r, g, b):
    mu = jnp.mean(r, axis=1, keepdims=True)
    d = r - mu
    var = jnp.mean(d * d, axis=1, keepdims=True)
    return d * lax.rsqrt(var + LN_EPS) * g + b


def _tail_kernel(of_ref, od_ref, x_ref, wo_ref, g_ref, b_ref, wr_ref, br_ref, cin_ref,
                 x1_ref, e_ref, gate_ref, rank_ref, cnt_ref, carry_ref, *, alpha):
    i = pl.program_id(0)
    tm = x_ref.shape[0]

    @pl.when(i == 0)
    def _():
        carry_ref[...] = cin_ref[...]

    r = (alpha * x_ref[...]
         + jnp.dot(of_ref[...], wo_ref[0:W_FOX, :], preferred_element_type=F32)
         + jnp.dot(od_ref[...], wo_ref[W_FOX:W_FOX + W_DSA, :], preferred_element_type=F32))
    x1 = _layer_norm(r, g_ref[...], b_ref[...])
    x1_ref[...] = x1

    logits = jnp.dot(x1.astype(BF16), wr_ref[...], preferred_element_type=F32) + br_ref[...]
    lane = lax.broadcasted_iota(I32, (tm, LANES), 1)
    lane_f = lane.astype(F32)
    vals, idxs, hots = [], [], []
    lg = logits
    for _ in range(TOP_K_EXPERTS):
        m = jnp.max(lg, axis=1, keepdims=True)
        idx = jnp.min(jnp.where(lg == m, lane_f, float(LANES)), axis=1, keepdims=True)
        hot = lane_f == idx
        vals.append(m)
        idxs.append(idx)
        hots.append(hot)
        lg = jnp.where(hot, NEG_INF, lg)
    exps = [jnp.exp(v - vals[0]) for v in vals]
    den = exps[0]
    for e in exps[1:]:
        den = den + e
    sel = jnp.zeros((tm, LANES), F32)
    for hot in hots:
        sel = sel + hot.astype(F32)
    rr = lax.broadcasted_iota(I32, (tm, tm), 0)
    cc = lax.broadcasted_iota(I32, (tm, tm), 1)
    strict_lower = (cc < rr).astype(BF16)
    before = jnp.dot(strict_lower, sel.astype(BF16), preferred_element_type=F32) + carry_ref[...]
    e_out = jnp.zeros((tm, LANES), F32)
    g_out = jnp.zeros((tm, LANES), F32)
    r_out = jnp.zeros((tm, LANES), F32)
    for j in range(TOP_K_EXPERTS):
        rank = jnp.sum(jnp.where(hots[j], before, 0.0), axis=1, keepdims=True)
        e_out = jnp.where(lane == j, idxs[j], e_out)
        g_out = jnp.where(lane == j, exps[j] / den, g_out)
        r_out = jnp.where(lane == j, rank, r_out)
    e_ref[...] = e_out.astype(I32)
    gate_ref[...] = g_out
    rank_ref[...] = r_out.astype(I32)
    carry_ref[...] = carry_ref[...] + jnp.sum(sel, axis=0, keepdims=True)
    cnt_ref[...] = carry_ref[...]


def _tail(of, od, x2d, wo, g1, b1, wr, br, cin, alpha, tm):
    rows, d = x2d.shape
    row_spec = lambda w: pl.BlockSpec((tm, w), lambda i: (i, 0))
    full = lambda a: pl.BlockSpec(a.shape, lambda i: (0,) * a.ndim)
    return pl.pallas_call(
        functools.partial(_tail_kernel, alpha=alpha),
        grid=(rows // tm,),
        in_specs=[row_spec(W_FOX), row_spec(W_DSA), row_spec(d), full(wo), full(g1), full(b1),
                  full(wr), full(br), full(cin)],
        out_specs=[row_spec(d), row_spec(LANES), row_spec(LANES), row_spec(LANES),
                   pl.BlockSpec((1, LANES), lambda i: (0, 0))],
        out_shape=[jax.ShapeDtypeStruct((rows, d), F32), jax.ShapeDtypeStruct((rows, LANES), I32),
                   jax.ShapeDtypeStruct((rows, LANES), F32), jax.ShapeDtypeStruct((rows, LANES), I32),
                   jax.ShapeDtypeStruct((1, LANES), F32)],
        scratch_shapes=[pltpu.VMEM((1, LANES), F32)],
        compiler_params=_cparams(("arbitrary",)),
        name="tail_router",
    )(of, od, x2d, wo, g1, b1, wr, br, cin)


EXPERT_BLOCK = 256
DISPATCH_TOKENS = 128


def _dispatch_kernel(dest_ref, x_hbm, xs_in, xs_out, sem, *, td):
    del xs_in
    i = pl.program_id(0)

    def row_copy(src_row, dst_row):
        return pltpu.make_async_copy(x_hbm.at[pl.ds(src_row, 1), :], xs_out.at[pl.ds(dst_row, 1), :], sem)

    def issue(t, c):
        for j in range(TOP_K_EXPERTS):
            row_copy(i * td + t, dest_ref[t * TOP_K_EXPERTS + j]).start()
        return c

    lax.fori_loop(0, td, issue, 0)

    def drain(t, c):
        for j in range(TOP_K_EXPERTS):
            row_copy(0, 0).wait()
        return c

    lax.fori_loop(0, td, drain, 0)


def _dispatch(dest_flat, x1, xs):
    rows, d = x1.shape
    td = DISPATCH_TOKENS
    return pl.pallas_call(
        functools.partial(_dispatch_kernel, td=td),
        grid=(rows // td,),
        in_specs=[pl.BlockSpec((td * TOP_K_EXPERTS,), lambda i: (i,), memory_space=pltpu.SMEM),
                  pl.BlockSpec(memory_space=pl.ANY), pl.BlockSpec(memory_space=pl.ANY)],
        out_specs=pl.BlockSpec(memory_space=pl.ANY),
        out_shape=jax.ShapeDtypeStruct(xs.shape, xs.dtype),
        scratch_shapes=[pltpu.SemaphoreType.DMA(())],
        input_output_aliases={2: 0},
        compiler_params=_cparams(("arbitrary",)),
        name="moe_dispatch",
    )(dest_flat, x1, xs)


def _ffn_kernel(be_ref, first_ref, nused_ref, xs_ref, wg_ref, wu_ref, wd_ref, bg_ref, bu_ref, bd_ref,
                ys_ref, wgb_ref, wub_ref, wdb_ref):
    b = pl.program_id(0)

    @pl.when(first_ref[b] == 1)
    def _():
        wgb_ref[...] = wg_ref[...].astype(BF16)
        wub_ref[...] = wu_ref[...].astype(BF16)
        wdb_ref[...] = wd_ref[...].astype(BF16)

    @pl.when(b < nused_ref[0])
    def _():
        xb = xs_ref[...].astype(BF16)
        g = jnp.dot(xb, wgb_ref[...], preferred_element_type=F32) + bg_ref[...]
        u = jnp.dot(xb, wub_ref[...], preferred_element_type=F32) + bu_ref[...]
        g = jnp.minimum(g, SWIGLU_LIMIT)
        u = jnp.clip(u, -SWIGLU_LIMIT, SWIGLU_LIMIT)
        h = g * (1.0 / (1.0 + jnp.exp(-SWIGLU_ALPHA * g))) * (u + 1.0)
        ys_ref[...] = jnp.dot(h.astype(BF16), wdb_ref[...], preferred_element_type=F32) + bd_ref[...]

    @pl.when(b >= nused_ref[0])
    def _():
        ys_ref[...] = jnp.zeros(ys_ref.shape, F32)


def _expert_ffn(block_e, is_first, n_used, xs, wg, wu, wd, bg, bu, bd):
    p, d = xs.shape
    bs = EXPERT_BLOCK
    de = wg.shape[2]
    w_spec = lambda shape: pl.BlockSpec((None,) + shape, lambda b, be, fi, nu: (be[b], 0, 0))
    return pl.pallas_call(
        _ffn_kernel,
        grid_spec=pltpu.PrefetchScalarGridSpec(
            num_scalar_prefetch=3,
            grid=(p // bs,),
            in_specs=[pl.BlockSpec((bs, d), lambda b, be, fi, nu: (b, 0)),
                      w_spec((d, de)), w_spec((d, de)), w_spec((de, d)),
                      w_spec((1, de)), w_spec((1, de)), w_spec((1, d))],
            out_specs=pl.BlockSpec((bs, d), lambda b, be, fi, nu: (b, 0)),
            scratch_shapes=[pltpu.VMEM((d, de), BF16), pltpu.VMEM((d, de), BF16), pltpu.VMEM((de, d), BF16)]),
        out_shape=jax.ShapeDtypeStruct((p, d), F32),
        compiler_params=_cparams(("arbitrary",)),
        name="moe_ffn",
    )(block_e, is_first, n_used, xs, wg, wu, wd, bg, bu, bd)


def _combine_kernel(dest_ref, ys_hbm, gate_ref, x1_ref, g_ref, b_ref, o_ref, buf_ref, sem, *, tc, alpha):
    def row_copy(src_row, j, t):
        return pltpu.make_async_copy(ys_hbm.at[pl.ds(src_row, 1), :], buf_ref.at[j, pl.ds(t, 1), :], sem)

    def issue(t, c):
        for j in range(TOP_K_EXPERTS):
            row_copy(dest_ref[t * TOP_K_EXPERTS + j], j, t).start()
        return c

    lax.fori_loop(0, tc, issue, 0)

    def drain(t, c):
        for j in range(TOP_K_EXPERTS):
            row_copy(0, j, t).wait()
        return c

    lax.fori_loop(0, tc, drain, 0)

    gates = gate_ref[...]
    moe = gates[:, 0:1] * buf_ref[0]
    for j in range(1, TOP_K_EXPERTS):
        moe = moe + gates[:, j:j + 1] * buf_ref[j]
    o_ref[...] = _layer_norm(alpha * x1_ref[...] + moe, g_ref[...], b_ref[...])


def _combine(dest_flat, ys, gates, x1, g2, b2, alpha):
    rows, d = x1.shape
    tc = DISPATCH_TOKENS
    return pl.pallas_call(
        functools.partial(_combine_kernel, tc=tc, alpha=alpha),
        grid=(rows // tc,),
        in_specs=[pl.BlockSpec((tc * TOP_K_EXPERTS,), lambda i: (i,), memory_space=pltpu.SMEM),
                  pl.BlockSpec(memory_space=pl.ANY),
                  pl.BlockSpec((tc, LANES), lambda i: (i, 0)),
                  pl.BlockSpec((tc, d), lambda i: (i, 0)),
                  pl.BlockSpec((1, d), lambda i: (0, 0)),
                  pl.BlockSpec((1, d), lambda i: (0, 0))],
        out_specs=pl.BlockSpec((tc, d), lambda i: (i, 0)),
        out_shape=jax.ShapeDtypeStruct((rows, d), F32),
        scratch_shapes=[pltpu.VMEM((TOP_K_EXPERTS, tc, d), F32), pltpu.SemaphoreType.DMA(())],
        compiler_params=_cparams(("arbitrary",)),
        name="moe_combine",
    )(dest_flat, ys, gates, x1, g2, b2)


def _prep_w_in(w, b_f):
    d = w.shape[0]
    sizes = (W_FOX, W_FOX, W_FOX, H_FOX, W_DSA, W_DSA, W_DSA, H_IDX * D_IDX, D_IDX, H_IDX)
    parts, acc = [], 0
    for s in sizes:
        parts.append(w[:, acc:acc + s])
        acc += s
    fq, fk, fv, fl, dq, dk, dv, iq, ik, iw = parts
    small = jnp.concatenate([ik, fl, iw, jnp.zeros((d, LANES - D_IDX - H_FOX - H_IDX), w.dtype)], axis=1)
    wp = jnp.concatenate([fq, fk, fv, dq, dk, dv, iq, small], axis=1).astype(BF16)
    bf_row = jnp.zeros((1, LANES), F32).at[0, SM_LF:SM_LF + H_FOX].set(b_f.astype(F32))
    return wp, bf_row


def _rope_tables(pos):
    half = HEAD_DIM // 2
    inv = ROPE_THETA ** (-jnp.arange(half, dtype=F32) / half)
    ang = pos.astype(F32)[:, None] * inv[None, :]
    cos, sin = jnp.cos(ang), jnp.sin(ang)
    cos_t = jnp.concatenate([cos, cos, cos, cos], axis=1)
    sin_t = jnp.concatenate([-sin, sin, -sin, sin], axis=1)
    return cos_t, sin_t


def _pad_rows(a, n):
    return jnp.pad(a, ((0, 0), (0, n - a.shape[1]), (0, 0)))


def kernel(x_prompt, x_sample, cache_fox_k, cache_fox_v, cache_fox_logf, cache_dsa_k, cache_dsa_v,
           cache_dsa_kidx, page_table, w_in, b_f, w_o, ln1_g, ln1_b, w_router, b_router, w_gate, b_gate,
           w_up, b_up, w_down, b_down, ln2_g, ln2_b):
    batch, seq, d = x_prompt.shape
    dec_b, n_q, _ = x_sample.shape
    depth = w_in.shape[0]
    n_pool = cache_fox_k.shape[1]
    n_pages = page_table.shape[1]
    past = n_pages * PAGE_SIZE
    alpha = (2.0 * depth) ** 0.25
    rows_p, rows_s = batch * seq, dec_b * n_q

    cos_p, sin_p = _rope_tables(jnp.arange(seq))
    cos_s, sin_s = _rope_tables(past + (jnp.arange(rows_s) % n_q))

    xp = x_prompt.reshape(rows_p, d)
    xs_ = x_sample.reshape(rows_s, d)
    outs_p = [[] for _ in range(6)]
    outs_s = [[] for _ in range(6)]
    for l in range(depth):
        wp, bf_row = _prep_w_in(w_in[l], b_f[l])
        wo = w_o[l].astype(BF16)
        wr = jnp.pad(w_router[l], ((0, 0), (0, LANES - N_EXPERTS))).astype(BF16)
        br = jnp.full((1, LANES), NEG_INF, F32).at[0, :N_EXPERTS].set(b_router[l].astype(F32))
        g1, b1 = ln1_g[l].reshape(1, d), ln1_b[l].reshape(1, d)
        g2, b2 = ln2_g[l].reshape(1, d), ln2_b[l].reshape(1, d)

        pp = _in_proj(xp, wp, bf_row, cos_p, sin_p, tm=256)
        o_fox = _fox_prompt(pp["fq"], pp["fkb"], pp["fvb"], pp["lft"], batch, seq, tq=256)
        o_dsa = _dsa_prompt(pp["iq"], pp["iw"], pp["ikb"], pp["dq"], pp["dkb"], pp["dvb"], batch, seq,
                            tq=128, ck=512)

        ps = _in_proj(xs_, wp, bf_row, cos_s, sin_s, tm=rows_s)
        pool_fk = cache_fox_k[l].reshape(n_pool, PAGE_SIZE, W_FOX)
        pool_fv = cache_fox_v[l].reshape(n_pool, PAGE_SIZE, W_FOX)
        pool_lft = jnp.swapaxes(cache_fox_logf[l], 1, 2)
        pool_dk = cache_dsa_k[l].reshape(n_pool, PAGE_SIZE, W_DSA)
        pool_dv = cache_dsa_v[l].reshape(n_pool, PAGE_SIZE, W_DSA)
        pool_ki = cache_dsa_kidx[l]
        seq3 = lambda a: a.reshape(dec_b, n_q, a.shape[-1])
        lft_new = jnp.pad(jnp.swapaxes(seq3(ps["lf"]), 1, 2), ((0, 0), (0, 0), (0, PAGE_SIZE - n_q)))
        so_fox = _sample_fox(page_table, seq3(ps["fq"]), _pad_rows(seq3(ps["fk"]), PAGE_SIZE),
                             _pad_rows(seq3(ps["fv"]), PAGE_SIZE), lft_new, pool_fk, pool_fv, pool_lft)
        iqx = seq3(ps["iq"]).reshape(dec_b, n_q, H_IDX, D_IDX).transpose(0, 2, 1, 3)
        iqx = jnp.pad(iqx, ((0, 0), (0, 0), (0, H_FOX - n_q), (0, 0))).reshape(dec_b, H_IDX * H_FOX, D_IDX)
        iwx = jnp.pad(seq3(ps["iw"]).transpose(0, 2, 1), ((0, 0), (0, 0), (0, H_FOX - n_q)))
        iwx = iwx.reshape(dec_b, H_IDX * H_FOX, 1)
        sel_bias = _sample_index(page_table, iqx, iwx, _pad_rows(seq3(ps["ik"]), PAGE_SIZE), pool_ki, n_q)
        so_dsa = _sample_dsa(page_table, seq3(ps["dq"]), _pad_rows(seq3(ps["dk"]), PAGE_SIZE),
                             _pad_rows(seq3(ps["dv"]), PAGE_SIZE), sel_bias, pool_dk, pool_dv)

        cin = jnp.zeros((1, LANES), F32)
        x1_p, e_p, gate_p, rank_p, cnt_p = _tail(o_fox, o_dsa, xp, wo, g1, b1, wr, br, cin, alpha, tm=256)
        x1_s, e_s, gate_s, rank_s, cnt = _tail(so_fox.reshape(rows_s, W_FOX), so_dsa.reshape(rows_s, W_DSA),
                                               xs_, wo, g1, b1, wr, br, cnt_p, alpha, tm=rows_s)

        bs = EXPERT_BLOCK
        counts = cnt[0, :N_EXPERTS].astype(I32)
        padded = (counts + bs - 1) // bs * bs
        pad_end = jnp.cumsum(padded)
        pad_start = pad_end - padded
        n_assign = (rows_p + rows_s) * TOP_K_EXPERTS
        n_blocks = -(-(n_assign + N_EXPERTS * (bs - 1)) // bs)
        blk = jnp.arange(n_blocks, dtype=I32)
        n_used = (pad_end[-1] // bs).astype(I32)
        block_e = jnp.minimum(jnp.searchsorted(pad_end, blk * bs, side="right"), N_EXPERTS - 1).astype(I32)
        block_e = jnp.where(blk < n_used, block_e, block_e[jnp.maximum(n_used - 1, 0)])
        is_first = jnp.concatenate([jnp.ones((1,), I32), (block_e[1:] != block_e[:-1]).astype(I32)])
        dest_p = (pad_start[e_p[:, :TOP_K_EXPERTS]] + rank_p[:, :TOP_K_EXPERTS]).reshape(-1)
        dest_s = (pad_start[e_s[:, :TOP_K_EXPERTS]] + rank_s[:, :TOP_K_EXPERTS]).reshape(-1)

        xs_sorted = jnp.zeros((n_blocks * bs, d), F32)
        xs_sorted = _dispatch(dest_p, x1_p, xs_sorted)
        xs_sorted = _dispatch(dest_s, x1_s, xs_sorted)
        ys = _expert_ffn(block_e, is_first, n_used.reshape(1), xs_sorted, w_gate[l], w_up[l], w_down[l],
                         b_gate[l].reshape(N_EXPERTS, 1, -1), b_up[l].reshape(N_EXPERTS, 1, -1),
                         b_down[l].reshape(N_EXPERTS, 1, -1))
        xp_new = _combine(dest_p, ys, gate_p, x1_p, g2, b2, alpha)
        xs_new = _combine(dest_s, ys, gate_s, x1_s, g2, b2, alpha)

        for acc, a in zip(outs_p, (pp["fk"].reshape(batch, seq, H_FOX, HEAD_DIM),
                                   pp["fv"].reshape(batch, seq, H_FOX, HEAD_DIM),
                                   pp["lf"].reshape(batch, seq, H_FOX),
                                   pp["dk"].reshape(batch, seq, H_DSA, HEAD_DIM),
                                   pp["dv"].reshape(batch, seq, H_DSA, HEAD_DIM),
                                   pp["ik"].reshape(batch, seq, D_IDX))):
            acc.append(a)
        for acc, a in zip(outs_s, (ps["fk"].reshape(dec_b, n_q, H_FOX, HEAD_DIM),
                                   ps["fv"].reshape(dec_b, n_q, H_FOX, HEAD_DIM),
                                   ps["lf"].reshape(dec_b, n_q, H_FOX),
                                   ps["dk"].reshape(dec_b, n_q, H_DSA, HEAD_DIM),
                                   ps["dv"].reshape(dec_b, n_q, H_DSA, HEAD_DIM),
                                   ps["ik"].reshape(dec_b, n_q, D_IDX))):
            acc.append(a)
        xp, xs_ = xp_new, xs_new

    stack = lambda lst: [jnp.stack(r, axis=0) for r in lst]
    return (xp.reshape(batch, seq, d), xs_.reshape(dec_b, n_q, d), *stack(outs_p), *stack(outs_s))
```

```python
import functools

import jax
import jax.numpy as jnp
from jax import lax
from jax.experimental import pallas as pl
from jax.experimental.pallas import tpu as pltpu

F32, BF16, I32 = jnp.float32, jnp.bfloat16, jnp.int32
NEG_INF = float("-inf")

HEAD_DIM = 64
H_FOX = 8
H_DSA = 8
W_FOX = H_FOX * HEAD_DIM
W_DSA = H_DSA * HEAD_DIM
H_IDX = 16
D_IDX = 64
TOPK_MAX = 256
PAGE_SIZE = 128
ROPE_THETA = 10000.0
ATTN_SCALE = HEAD_DIM ** -0.5
IDX_SCALE = D_IDX ** -0.5
IDX_W_SCALE = H_IDX ** -0.5
N_EXPERTS = 32
TOP_K_EXPERTS = 4
SWIGLU_LIMIT = 7.0
SWIGLU_ALPHA = 1.702
LN_EPS = 1e-5

LANES = 128
KEY_NEG_INF = -2139095041
INT_MIN = -2 ** 31
VMEM_LIMIT = 56 * 1024 * 1024

NT_DIMS = (((1,), (1,)), ((), ()))


def _cparams(sem):
    return pltpu.CompilerParams(dimension_semantics=sem, vmem_limit_bytes=VMEM_LIMIT)


def _split3(x):
    hi = x.astype(BF16)
    r1 = x - hi.astype(F32)
    mid = r1.astype(BF16)
    lo = (r1 - mid.astype(F32)).astype(BF16)
    return hi, mid, lo


def _lane_cumsum(x, upper):
    hi, mid, lo = _split3(x)
    return (jnp.dot(hi, upper, preferred_element_type=F32)
            + jnp.dot(mid, upper, preferred_element_type=F32)
            + jnp.dot(lo, upper, preferred_element_type=F32))


def _upper_tri(n):
    r = lax.broadcasted_iota(I32, (n, n), 0)
    c = lax.broadcasted_iota(I32, (n, n), 1)
    return (r <= c).astype(BF16)


def _sort_key(x):
    bits = pltpu.bitcast(x, I32)
    return bits ^ ((bits >> 31) & 0x7FFFFFFF)


C_FQ, C_FK, C_FV, C_DQ, C_DK, C_DV, C_IQ, C_SM = 0, 512, 1024, 1536, 2048, 2560, 3072, 4096
N_PROJ = 4224
SM_IK, SM_LF, SM_IW = 0, 64, 72


def _in_proj_kernel(x_ref, w_ref, cos_ref, sin_ref, bf_ref,
                    fq_ref, fk_ref, fkb_ref, fv_ref, fvb_ref, lf_ref, lft_ref,
                    dq_ref, dk_ref, dkb_ref, dv_ref, dvb_ref, iq_ref, ik_ref, ikb_ref, iw_ref):
    xb = x_ref[...].astype(BF16)
    tm = xb.shape[0]
    cos = cos_ref[...]
    sin = sin_ref[...]
    lane = lax.broadcasted_iota(I32, (tm, LANES), 1)
    first_half = (lane & (HEAD_DIM - 1)) < HEAD_DIM // 2

    def mm(c0, n):
        return jnp.dot(xb, w_ref[:, c0:c0 + n], preferred_element_type=F32)

    def rope(y):
        outs = []
        for c in range(0, y.shape[1], LANES):
            yc = y[:, c:c + LANES]
            partner = jnp.where(first_half, pltpu.roll(yc, LANES - HEAD_DIM // 2, 1),
                                pltpu.roll(yc, HEAD_DIM // 2, 1))
            outs.append(yc * cos + partner * sin)
        return outs[0] if len(outs) == 1 else jnp.concatenate(outs, axis=1)

    y = mm(C_FQ, W_FOX)
    fq_ref[...] = (y * ATTN_SCALE).astype(BF16)
    y = mm(C_FK, W_FOX)
    fk_ref[...] = y
    fkb_ref[...] = y.astype(BF16)
    y = mm(C_FV, W_FOX)
    fv_ref[...] = y
    fvb_ref[...] = y.astype(BF16)
    y = rope(mm(C_DQ, W_DSA))
    dq_ref[...] = (y * ATTN_SCALE).astype(BF16)
    y = rope(mm(C_DK, W_DSA))
    dk_ref[...] = y
    dkb_ref[...] = y.astype(BF16)
    y = mm(C_DV, W_DSA)
    dv_ref[...] = y
    dvb_ref[...] = y.astype(BF16)
    y = rope(mm(C_IQ, H_IDX * D_IDX))
    iq_ref[...] = (y * IDX_SCALE).astype(BF16)
    z = mm(C_SM, LANES)
    r = rope(z)
    ik_ref[...] = r[:, SM_IK:SM_IK + D_IDX]
    ikb_ref[...] = r[:, SM_IK:SM_IK + D_IDX].astype(BF16)
    zb = z + bf_ref[...]
    lsg = -(jnp.maximum(-zb, 0.0) + jnp.log1p(jnp.exp(-jnp.abs(zb))))
    lf_ref[...] = lsg[:, SM_LF:SM_LF + H_FOX]
    lft_ref[...] = lsg.T[SM_LF:SM_LF + H_FOX, :]
    iw_ref[...] = z[:, SM_IW:SM_IW + H_IDX] * IDX_W_SCALE


def _in_proj(x2d, wp, bf_row, cos_t, sin_t, tm):
    rows, d = x2d.shape
    n_tab = cos_t.shape[0] // tm
    outs = [("fq", W_FOX, BF16), ("fk", W_FOX, F32), ("fkb", W_FOX, BF16), ("fv", W_FOX, F32),
            ("fvb", W_FOX, BF16), ("lf", H_FOX, F32), ("lft", None, F32),
            ("dq", W_DSA, BF16), ("dk", W_DSA, F32), ("dkb", W_DSA, BF16), ("dv", W_DSA, F32),
            ("dvb", W_DSA, BF16), ("iq", H_IDX * D_IDX, BF16), ("ik", D_IDX, F32),
            ("ikb", D_IDX, BF16), ("iw", H_IDX, F32)]
    out_shape, out_specs = [], []
    for name, w, dt in outs:
        if name == "lft":
            out_shape.append(jax.ShapeDtypeStruct((H_FOX, rows), dt))
            out_specs.append(pl.BlockSpec((H_FOX, tm), lambda i: (0, i)))
        else:
            out_shape.append(jax.ShapeDtypeStruct((rows, w), dt))
            out_specs.append(pl.BlockSpec((tm, w), lambda i: (i, 0)))
    res = pl.pallas_call(
        _in_proj_kernel,
        grid=(rows // tm,),
        in_specs=[pl.BlockSpec((tm, d), lambda i: (i, 0)),
                  pl.BlockSpec((d, N_PROJ), lambda i: (0, 0)),
                  pl.BlockSpec((tm, LANES), lambda i: (i % n_tab, 0)),
                  pl.BlockSpec((tm, LANES), lambda i: (i % n_tab, 0)),
                  pl.BlockSpec((1, LANES), lambda i: (0, 0))],
        out_specs=out_specs,
        out_shape=out_shape,
        compiler_params=_cparams(("arbitrary",)),
        name="in_proj",
    )(x2d, wp, cos_t, sin_t, bf_row)
    return {name: r for (name, _, _), r in zip(outs, res)}


def _flash_update(h, s, v, m_ref, l_ref, acc_ref):
    m = m_ref[h]
    m_new = jnp.maximum(m, jnp.max(s, axis=1, keepdims=True))
    alpha = jnp.exp(m - m_new)
    p = jnp.exp(s - m_new)
    l_ref[h] = alpha * l_ref[h] + jnp.sum(p, axis=1, keepdims=True)
    acc_ref[h] = alpha * acc_ref[h] + jnp.dot(p.astype(BF16), v, preferred_element_type=F32)
    m_ref[h] = m_new


def _flash_init(m_ref, l_ref, acc_ref):
    m_ref[...] = jnp.full(m_ref.shape, -1e30, F32)
    l_ref[...] = jnp.zeros(l_ref.shape, F32)
    acc_ref[...] = jnp.zeros(acc_ref.shape, F32)


def _flash_finish(o_ref, l_ref, acc_ref, n_heads):
    for h in range(n_heads):
        o_ref[:, h * HEAD_DIM:(h + 1) * HEAD_DIM] = (acc_ref[h] / l_ref[h]).astype(BF16)


def _flash_scratch(n_heads, tq):
    return [pltpu.VMEM((n_heads, tq, 1), F32), pltpu.VMEM((n_heads, tq, 1), F32),
            pltpu.VMEM((n_heads, tq, HEAD_DIM), F32)]


def _fox_kernel(q_ref, k_ref, v_ref, lft_ref, o_ref, c_ref, m_ref, l_ref, acc_ref, *, tq, nk):
    i = pl.program_id(1)

    @pl.when(i == 0)
    def _():
        upper = _upper_tri(tq)
        carry = jnp.zeros((H_FOX, 1), F32)
        for j in range(nk):
            cs = _lane_cumsum(lft_ref[:, j * tq:(j + 1) * tq], upper) + carry
            c_ref[j] = cs
            carry = cs[:, tq - 1:tq]

    row = lax.broadcasted_iota(I32, (tq, tq), 0)
    col = lax.broadcasted_iota(I32, (tq, tq), 1)
    causal = col <= row
    _flash_init(m_ref, l_ref, acc_ref)

    def kv_step(j, masked):
        start = pl.multiple_of(j * tq, tq)
        for h in range(H_FOX):
            lo, hi = h * HEAD_DIM, (h + 1) * HEAD_DIM
            s = lax.dot_general(q_ref[:, lo:hi], k_ref[pl.ds(start, tq), lo:hi], NT_DIMS,
                                preferred_element_type=F32)
            s = s - c_ref[j, pl.ds(h, 1), :]
            if masked:
                s = jnp.where(causal, s, NEG_INF)
            _flash_update(h, s, v_ref[pl.ds(start, tq), lo:hi], m_ref, l_ref, acc_ref)

    def body(j, c):
        kv_step(j, False)
        return c

    lax.fori_loop(0, i, body, 0)
    kv_step(i, True)
    _flash_finish(o_ref, l_ref, acc_ref, H_FOX)


def _fox_prompt(qb, kb, vb, lft, batch, seq, tq):
    nq = seq // tq
    return pl.pallas_call(
        functools.partial(_fox_kernel, tq=tq, nk=nq),
        grid=(batch, nq),
        in_specs=[pl.BlockSpec((tq, W_FOX), lambda b, i: (b * nq + i, 0)),
                  pl.BlockSpec((seq, W_FOX), lambda b, i: (b, 0)),
                  pl.BlockSpec((seq, W_FOX), lambda b, i: (b, 0)),
                  pl.BlockSpec((H_FOX, seq), lambda b, i: (0, b))],
        out_specs=pl.BlockSpec((tq, W_FOX), lambda b, i: (b * nq + i, 0)),
        out_shape=jax.ShapeDtypeStruct((batch * seq, W_FOX), BF16),
        scratch_shapes=[pltpu.VMEM((nq, H_FOX, tq), F32)] + _flash_scratch(H_FOX, tq),
        compiler_params=_cparams(("arbitrary", "arbitrary")),
        name="fox_prompt",
    )(qb, kb, vb, lft)


def _kth_largest_key(key_ref, n_chunks, rows, topk):
    def bit_step(bi, thr):
        cand = thr + lax.shift_left(jnp.int32(1), 31 - bi)

        def count(c, cnt):
            return cnt + jnp.sum((key_ref[c] >= cand).astype(F32), axis=1, keepdims=True)

        cnt = lax.fori_loop(0, n_chunks, count, jnp.zeros((rows, 1), F32))
        return jnp.where(cnt >= topk, cand, thr)

    return lax.fori_loop(0, 32, bit_step, jnp.full((rows, 1), INT_MIN, I32))


def _write_select_bias(key_ref, bias_ref, n_chunks, rows, width, topk, thr):
    def counts(c, carry):
        gt, eq = carry
        k = key_ref[c]
        return (gt + jnp.sum((k > thr).astype(F32), axis=1, keepdims=True),
                eq + jnp.sum((k == thr).astype(F32), axis=1, keepdims=True))

    zero = jnp.zeros((rows, 1), F32)
    gt, eq = lax.fori_loop(0, n_chunks, counts, (zero, zero))
    need = topk - gt
    excess = jnp.max(jnp.where((eq > need) & (thr > KEY_NEG_INF), 1.0, 0.0)) > 0.0

    @pl.when(jnp.logical_not(excess))
    def _():
        def fill(c, _):
            k = key_ref[c]
            bias_ref[c] = jnp.where((k >= thr) & (k > KEY_NEG_INF), 0.0, NEG_INF)
            return 0
        lax.fori_loop(0, n_chunks, fill, 0)

    @pl.when(excess)
    def _():
        upper = _upper_tri(width)

        def fill(c, before):
            k = key_ref[c]
            tie = k == thr
            incl = jnp.dot(tie.astype(BF16), upper, preferred_element_type=F32) + before
            excl = incl - tie.astype(F32)
            sel = (k > thr) | (tie & (excl < need))
            bias_ref[c] = jnp.where(sel & (k > KEY_NEG_INF), 0.0, NEG_INF)
            return incl[:, width - 1:width]
        lax.fori_loop(0, n_chunks, fill, zero)


def _dsa_kernel(iq_ref, iw_ref, ik_ref, q_ref, k_ref, v_ref, o_ref, key_ref, bias_ref,
                m_ref, l_ref, acc_ref, *, tq, ck, topk):
    i = pl.program_id(1)
    q0 = i * tq
    n_chunks = (q0 + tq + ck - 1) // ck
    row = q0 + lax.broadcasted_iota(I32, (tq, ck), 0)
    col0 = lax.broadcasted_iota(I32, (tq, ck), 1)
    iw = iw_ref[...]

    def score_chunk(c, _):
        start = pl.multiple_of(c * ck, ck)
        ikc = ik_ref[pl.ds(start, ck), :]
        sc = jnp.zeros((tq, ck), F32)
        for h in range(H_IDX):
            rel = lax.dot_general(iq_ref[:, h * D_IDX:(h + 1) * D_IDX], ikc, NT_DIMS,
                                  preferred_element_type=F32)
            sc = sc + iw[:, h:h + 1] * jnp.maximum(rel, 0.0)
        sc = jnp.where(col0 + start <= row, sc, NEG_INF)
        key_ref[c] = _sort_key(sc)
        return 0

    lax.fori_loop(0, n_chunks, score_chunk, 0)
    thr = _kth_largest_key(key_ref, n_chunks, tq, topk)
    _write_select_bias(key_ref, bias_ref, n_chunks, tq, ck, topk, thr)

    _flash_init(m_ref, l_ref, acc_ref)

    def kv_step(c, carry):
        start = pl.multiple_of(c * ck, ck)
        bias = bias_ref[c]
        for h in range(H_DSA):
            lo, hi = h * HEAD_DIM, (h + 1) * HEAD_DIM
            s = lax.dot_general(q_ref[:, lo:hi], k_ref[pl.ds(start, ck), lo:hi], NT_DIMS,
                                preferred_element_type=F32) + bias
            _flash_update(h, s, v_ref[pl.ds(start, ck), lo:hi], m_ref, l_ref, acc_ref)
        return carry

    lax.fori_loop(0, n_chunks, kv_step, 0)
    _flash_finish(o_ref, l_ref, acc_ref, H_DSA)


def _dsa_prompt(iq, iw, ikb, dq, dkb, dvb, batch, seq, tq, ck):
    nq = seq // tq
    topk = min(TOPK_MAX, seq // 4)
    return pl.pallas_call(
        functools.partial(_dsa_kernel, tq=tq, ck=ck, topk=topk),
        grid=(batch, nq),
        in_specs=[pl.BlockSpec((tq, H_IDX * D_IDX), lambda b, i: (b * nq + i, 0)),
                  pl.BlockSpec((tq, H_IDX), lambda b, i: (b * nq + i, 0)),
                  pl.BlockSpec((seq, D_IDX), lambda b, i: (b, 0)),
                  pl.BlockSpec((tq, W_DSA), lambda b, i: (b * nq + i, 0)),
                  pl.BlockSpec((seq, W_DSA), lambda b, i: (b, 0)),
                  pl.BlockSpec((seq, W_DSA), lambda b, i: (b, 0))],
        out_specs=pl.BlockSpec((tq, W_DSA), lambda b, i: (b * nq + i, 0)),
        out_shape=jax.ShapeDtypeStruct((batch * seq, W_DSA), BF16),
        scratch_shapes=[pltpu.VMEM((seq // ck, tq, ck), I32), pltpu.VMEM((seq // ck, tq, ck), F32)]
        + _flash_scratch(H_DSA, tq),
        compiler_params=_cparams(("arbitrary", "arbitrary")),
        name="dsa_prompt",
    )(iq, iw, ikb, dq, dkb, dvb)


PAGES_PER_STEP = 8


def _head_mask(rows, width):
    r = lax.broadcasted_iota(I32, (rows, width), 0)
    c = lax.broadcasted_iota(I32, (rows, width), 1)
    return (c // HEAD_DIM) == (r % H_FOX)


def _block_diag_q(q, n_q):
    hm = _head_mask(H_FOX, q.shape[1])
    qf = q.astype(F32)
    parts = [jnp.where(hm, jnp.broadcast_to(qf[t:t + 1, :], hm.shape), 0.0) for t in range(n_q)]
    return jnp.concatenate(parts, axis=0).astype(BF16)


def _rows_to_groups(x, n_q):
    return jnp.concatenate([jnp.broadcast_to(x[t:t + 1, :], (H_FOX, x.shape[1])) for t in range(n_q)], axis=0)


def _softmax_page(s, vt_bf16, m_ref, l_ref, acc_ref):
    m = m_ref[...]
    m_new = jnp.maximum(m, jnp.max(s, axis=1, keepdims=True))
    alpha = jnp.exp(m - m_new)
    p = jnp.exp(s - m_new)
    l_ref[...] = alpha * l_ref[...] + jnp.sum(p, axis=1, keepdims=True)
    acc_ref[...] = alpha * acc_ref[...] + lax.dot_general(p.astype(BF16), vt_bf16, NT_DIMS,
                                                          preferred_element_type=F32)
    m_ref[...] = m_new


def _stack_pages(refs):
    return jnp.concatenate([r[...].reshape(-1, PAGE_SIZE) for r in refs], axis=1).astype(BF16)


def _finish_decode(o_ref, l_ref, acc_ref, n_q):
    o = acc_ref[...] / l_ref[...]
    o = jnp.where(_head_mask(o.shape[0], o.shape[1]), o, 0.0)
    o_ref[...] = jnp.sum(o.reshape(n_q, H_FOX, o.shape[1]), axis=1).astype(BF16)


def _sample_fox_kernel(pt_ref, q_ref, kn_ref, vn_ref, lfn_ref, *rest, n_q, pps):
    k_refs, v_refs, lf_refs = rest[:pps], rest[pps:2 * pps], rest[2 * pps:3 * pps]
    o_ref, m_ref, l_ref, acc_ref, cc_ref, qbd_ref = rest[3 * pps:]
    step = pl.program_id(1)
    rows = n_q * H_FOX

    @pl.when(step == 0)
    def _():
        m_ref[...] = jnp.full(m_ref.shape, -1e30, F32)
        l_ref[...] = jnp.zeros(l_ref.shape, F32)
        acc_ref[...] = jnp.zeros(acc_ref.shape, F32)
        cc_ref[...] = jnp.zeros(cc_ref.shape, F32)
        qbd_ref[...] = _block_diag_q(q_ref[...], n_q)

    upper = _upper_tri(PAGE_SIZE)

    within = _lane_cumsum(jnp.concatenate([r[...] for r in lf_refs], axis=0), upper)
    off = cc_ref[...]
    parts = []
    for p in range(pps):
        cp = within[p * H_FOX:(p + 1) * H_FOX] + off
        parts.append(cp)
        off = cp[:, PAGE_SIZE - 1:PAGE_SIZE]
    cc_ref[...] = off
    cbias = jnp.concatenate(parts, axis=1)
    s = jnp.dot(qbd_ref[...], _stack_pages(k_refs), preferred_element_type=F32)
    s = s - jnp.concatenate([cbias] * n_q, axis=0)
    _softmax_page(s, _stack_pages(v_refs), m_ref, l_ref, acc_ref)

    @pl.when(step == pl.num_programs(1) - 1)
    def _():
        r = lax.broadcasted_iota(I32, (rows, PAGE_SIZE), 0)
        c = lax.broadcasted_iota(I32, (rows, PAGE_SIZE), 1)
        cn = _lane_cumsum(lfn_ref[...], upper) + cc_ref[...]
        sn = jnp.dot(qbd_ref[...], kn_ref[...].astype(BF16), preferred_element_type=F32)
        sn = sn - jnp.concatenate([cn] * n_q, axis=0)
        sn = jnp.where(c <= r // H_FOX, sn, NEG_INF)
        _softmax_page(sn, vn_ref[...].astype(BF16), m_ref, l_ref, acc_ref)
        _finish_decode(o_ref, l_ref, acc_ref, n_q)


def _page_spec(block, p, pps):
    nd = len(block)
    return pl.BlockSpec((None,) + block, lambda b, s, pt: (pt[b, s * pps + p],) + (0,) * nd)


def _sample_fox(page_table, q, k_new, v_new, lft_new, pool_k, pool_v, pool_lft):
    dec_b, n_q, w = q.shape
    n_pages = page_table.shape[1]
    pps = PAGES_PER_STEP
    rows = n_q * H_FOX
    seq_spec = lambda blk: pl.BlockSpec((None,) + blk, lambda b, s, pt: (b,) + (0,) * len(blk))
    kv_page = (H_FOX, HEAD_DIM, PAGE_SIZE)
    in_specs = ([seq_spec((n_q, w)), seq_spec((w, PAGE_SIZE)), seq_spec((w, PAGE_SIZE)),
                 seq_spec((H_FOX, PAGE_SIZE))]
                + [_page_spec(kv_page, p, pps) for p in range(pps)]
                + [_page_spec(kv_page, p, pps) for p in range(pps)]
                + [_page_spec((H_FOX, PAGE_SIZE), p, pps) for p in range(pps)])
    return pl.pallas_call(
        functools.partial(_sample_fox_kernel, n_q=n_q, pps=pps),
        grid_spec=pltpu.PrefetchScalarGridSpec(
            num_scalar_prefetch=1,
            grid=(dec_b, n_pages // pps),
            in_specs=in_specs,
            out_specs=pl.BlockSpec((None, n_q, w), lambda b, s, pt: (b, 0, 0)),
            scratch_shapes=[pltpu.VMEM((rows, 1), F32), pltpu.VMEM((rows, 1), F32),
                            pltpu.VMEM((rows, w), F32), pltpu.VMEM((H_FOX, 1), F32),
                            pltpu.VMEM((rows, w), BF16)]),
        out_shape=jax.ShapeDtypeStruct((dec_b, n_q, w), BF16),
        compiler_params=_cparams(("arbitrary", "arbitrary")),
        name="sample_fox",
    )(page_table, q, k_new, v_new, lft_new, *([pool_k] * pps), *([pool_v] * pps), *([pool_lft] * pps))


def _sample_index_kernel(pt_ref, iq_ref, iw_ref, kn_ref, *rest, pps, n_q, topk):
    k_refs = rest[:pps]
    bias_ref, biasn_ref, key_ref, keyn_ref = rest[pps:]
    step = pl.program_id(1)
    n_steps = key_ref.shape[0]
    iq = iq_ref[...]
    iw = iw_ref[...]

    def scores(kt):
        rel = jnp.dot(iq, kt, preferred_element_type=F32)
        w = iw * jnp.maximum(rel, 0.0)
        return jnp.sum(w.reshape(H_IDX, H_FOX, kt.shape[1]), axis=0)

    key_ref[step] = _sort_key(scores(_stack_pages(k_refs)))

    @pl.when(step == n_steps - 1)
    def _():
        r = lax.broadcasted_iota(I32, (H_FOX, PAGE_SIZE), 0)
        c = lax.broadcasted_iota(I32, (H_FOX, PAGE_SIZE), 1)
        keyn_ref[...] = _sort_key(jnp.where((c <= r) & (r < n_q), scores(kn_ref[...].astype(BF16)), NEG_INF))

        def count(pred):
            past = jnp.sum(jnp.sum(pred(key_ref[...]).astype(F32), axis=0), axis=1, keepdims=True)
            return past + jnp.sum(pred(keyn_ref[...]).astype(F32), axis=1, keepdims=True)

        def bit_step(bi, thr):
            cand = thr + lax.shift_left(jnp.int32(1), 31 - bi)
            return jnp.where(count(lambda k: k >= cand) >= topk, cand, thr)

        thr = lax.fori_loop(0, 32, bit_step, jnp.full((H_FOX, 1), INT_MIN, I32))
        need = topk - count(lambda k: k > thr)
        ties = count(lambda k: k == thr)
        excess = jnp.max(jnp.where((ties > need) & (thr > KEY_NEG_INF), 1.0, 0.0)) > 0.0

        @pl.when(jnp.logical_not(excess))
        def _():
            k = key_ref[...]
            bias_ref[...] = jnp.where((k >= thr) & (k > KEY_NEG_INF), 0.0, NEG_INF)
            k = keyn_ref[...]
            biasn_ref[...] = jnp.where((k >= thr) & (k > KEY_NEG_INF), 0.0, NEG_INF)

        @pl.when(excess)
        def _():
            upper = _upper_tri(PAGE_SIZE)

            def tie_block(k, before):
                tie = k == thr
                incl = jnp.dot(tie.astype(BF16), upper, preferred_element_type=F32) + before
                sel = (k > thr) | (tie & (incl - tie.astype(F32) < need))
                return jnp.where(sel & (k > KEY_NEG_INF), 0.0, NEG_INF), incl[:, PAGE_SIZE - 1:PAGE_SIZE]

            def fill(c, before):
                for p in range(pps):
                    lanes = slice(p * PAGE_SIZE, (p + 1) * PAGE_SIZE)
                    b, before = tie_block(key_ref[c, :, lanes], before)
                    bias_ref[c, :, lanes] = b
                return before

            before = lax.fori_loop(0, n_steps, fill, jnp.zeros((H_FOX, 1), F32))
            biasn_ref[...] = tie_block(keyn_ref[...], before)[0]


def _sample_index(page_table, iqx, iwx, kidx_new, pool_kidx, n_q):
    dec_b = iqx.shape[0]
    n_pages = page_table.shape[1]
    pps = PAGES_PER_STEP
    n_steps = n_pages // pps
    step_w = pps * PAGE_SIZE
    topk = min(TOPK_MAX, (n_pages * PAGE_SIZE + n_q) // 4)
    seq_spec = lambda blk: pl.BlockSpec((None,) + blk, lambda b, s, pt: (b,) + (0,) * len(blk))
    in_specs = ([seq_spec((H_IDX * H_FOX, D_IDX)), seq_spec((H_IDX * H_FOX, 1)), seq_spec((D_IDX, PAGE_SIZE))]
                + [_page_spec((D_IDX, PAGE_SIZE), p, pps) for p in range(pps)])
    return pl.pallas_call(
        functools.partial(_sample_index_kernel, pps=pps, n_q=n_q, topk=topk),
        grid_spec=pltpu.PrefetchScalarGridSpec(
            num_scalar_prefetch=1,
            grid=(dec_b, n_steps),
            in_specs=in_specs,
            out_specs=[pl.BlockSpec((None, n_steps, H_FOX, step_w), lambda b, s, pt: (b, 0, 0, 0)),
                       pl.BlockSpec((None, H_FOX, PAGE_SIZE), lambda b, s, pt: (b, 0, 0))],
            scratch_shapes=[pltpu.VMEM((n_steps, H_FOX, step_w), I32), pltpu.VMEM((H_FOX, PAGE_SIZE), I32)]),
        out_shape=[jax.ShapeDtypeStruct((dec_b, n_steps, H_FOX, step_w), F32),
                   jax.ShapeDtypeStruct((dec_b, H_FOX, PAGE_SIZE), F32)],
        compiler_params=_cparams(("arbitrary", "arbitrary")),
        name="sample_index",
    )(page_table, iqx, iwx, kidx_new, *([pool_kidx] * pps))


def _sample_dsa_kernel(pt_ref, q_ref, kn_ref, vn_ref, bias_ref, biasn_ref, *rest, n_q, pps):
    k_refs, v_refs = rest[:pps], rest[pps:2 * pps]
    o_ref, m_ref, l_ref, acc_ref, qbd_ref = rest[2 * pps:]
    step = pl.program_id(1)

    @pl.when(step == 0)
    def _():
        m_ref[...] = jnp.full(m_ref.shape, -1e30, F32)
        l_ref[...] = jnp.zeros(l_ref.shape, F32)
        acc_ref[...] = jnp.zeros(acc_ref.shape, F32)
        qbd_ref[...] = _block_diag_q(q_ref[...], n_q)

    def attend(kt, vt, bias):
        s = jnp.dot(qbd_ref[...], kt, preferred_element_type=F32) + _rows_to_groups(bias, n_q)
        _softmax_page(s, vt, m_ref, l_ref, acc_ref)

    attend(_stack_pages(k_refs), _stack_pages(v_refs), bias_ref[0])

    @pl.when(step == pl.num_programs(1) - 1)
    def _():
        attend(kn_ref[...].astype(BF16), vn_ref[...].astype(BF16), biasn_ref[...])
        _finish_decode(o_ref, l_ref, acc_ref, n_q)


def _sample_dsa(page_table, q, k_new, v_new, bias, bias_new, pool_k, pool_v):
    dec_b, n_q, w = q.shape
    n_pages = page_table.shape[1]
    pps = PAGES_PER_STEP
    rows = n_q * H_DSA
    kv_page = (H_DSA, HEAD_DIM, PAGE_SIZE)
    seq_spec = lambda blk: pl.BlockSpec((None,) + blk, lambda b, s, pt: (b,) + (0,) * len(blk))
    in_specs = ([seq_spec((n_q, w)), seq_spec((w, PAGE_SIZE)), seq_spec((w, PAGE_SIZE)),
                 pl.BlockSpec((None, 1, H_FOX, pps * PAGE_SIZE), lambda b, s, pt: (b, s, 0, 0)),
                 seq_spec((H_FOX, PAGE_SIZE))]
                + [_page_spec(kv_page, p, pps) for p in range(pps)]
                + [_page_spec(kv_page, p, pps) for p in range(pps)])
    return pl.pallas_call(
        functools.partial(_sample_dsa_kernel, n_q=n_q, pps=pps),
        grid_spec=pltpu.PrefetchScalarGridSpec(
            num_scalar_prefetch=1,
            grid=(dec_b, n_pages // pps),
            in_specs=in_specs,
            out_specs=pl.BlockSpec((None, n_q, w), lambda b, s, pt: (b, 0, 0)),
            scratch_shapes=[pltpu.VMEM((rows, 1), F32), pltpu.VMEM((rows, 1), F32),
                            pltpu.VMEM((rows, w), F32), pltpu.VMEM((rows, w), BF16)]),
        out_shape=jax.ShapeDtypeStruct((dec_b, n_q, w), BF16),
        compiler_params=_cparams(("arbitrary", "arbitrary")),
        name="sample_dsa",
    )(page_table, q, k_new, v_new, bias, bias_new, *([pool_k] * pps), *([pool_v] * pps))


def _layer_norm(r, g, b):
    mu = jnp.mean(r, axis=1, keepdims=True)
    d = r - mu
    var = jnp.mean(d * d, axis=1, keepdims=True)
    return d * lax.rsqrt(var + LN_EPS) * g + b


def _tail_kernel(of_ref, od_ref, x_ref, wo_ref, g_ref, b_ref, wr_ref, br_ref, cin_ref,
                 x1_ref, e_ref, gate_ref, rank_ref, cnt_ref, carry_ref, *, alpha):
    i = pl.program_id(0)
    tm = x_ref.shape[0]

    @pl.when(i == 0)
    def _():
        carry_ref[...] = cin_ref[...]

    r = (alpha * x_ref[...]
         + jnp.dot(of_ref[...], wo_ref[0:W_FOX, :], preferred_element_type=F32)
         + jnp.dot(od_ref[...], wo_ref[W_FOX:W_FOX + W_DSA, :], preferred_element_type=F32))
    x1 = _layer_norm(r, g_ref[...], b_ref[...])
    x1_ref[...] = x1

    logits = jnp.dot(x1.astype(BF16), wr_ref[...], preferred_element_type=F32) + br_ref[...]
    lane = lax.broadcasted_iota(I32, (tm, LANES), 1)
    lane_f = lane.astype(F32)
    vals, idxs, hots = [], [], []
    lg = logits
    for _ in range(TOP_K_EXPERTS):
        m = jnp.max(lg, axis=1, keepdims=True)
        idx = jnp.min(jnp.where(lg == m, lane_f, float(LANES)), axis=1, keepdims=True)
        hot = lane_f == idx
        vals.append(m)
        idxs.append(idx)
        hots.append(hot)
        lg = jnp.where(hot, NEG_INF, lg)
    exps = [jnp.exp(v - vals[0]) for v in vals]
    den = exps[0]
    for e in exps[1:]:
        den = den + e
    sel = jnp.zeros((tm, LANES), F32)
    for hot in hots:
        sel = sel + hot.astype(F32)
    rr = lax.broadcasted_iota(I32, (tm, tm), 0)
    cc = lax.broadcasted_iota(I32, (tm, tm), 1)
    strict_lower = (cc < rr).astype(BF16)
    before = jnp.dot(strict_lower, sel.astype(BF16), preferred_element_type=F32) + carry_ref[...]
    e_out = jnp.zeros((tm, LANES), F32)
    g_out = jnp.zeros((tm, LANES), F32)
    r_out = jnp.zeros((tm, LANES), F32)
    for j in range(TOP_K_EXPERTS):
        rank = jnp.sum(jnp.where(hots[j], before, 0.0), axis=1, keepdims=True)
        e_out = jnp.where(lane == j, idxs[j], e_out)
        g_out = jnp.where(lane == j, exps[j] / den, g_out)
        r_out = jnp.where(lane == j, rank, r_out)
    e_ref[...] = e_out.astype(I32)
    gate_ref[...] = g_out
    rank_ref[...] = r_out.astype(I32)
    carry_ref[...] = carry_ref[...] + jnp.sum(sel, axis=0, keepdims=True)
    cnt_ref[...] = carry_ref[...]


def _tail(of, od, x2d, wo, g1, b1, wr, br, cin, alpha, tm):
    rows, d = x2d.shape
    row_spec = lambda w: pl.BlockSpec((tm, w), lambda i: (i, 0))
    full = lambda a: pl.BlockSpec(a.shape, lambda i: (0,) * a.ndim)
    return pl.pallas_call(
        functools.partial(_tail_kernel, alpha=alpha),
        grid=(rows // tm,),
        in_specs=[row_spec(W_FOX), row_spec(W_DSA), row_spec(d), full(wo), full(g1), full(b1),
                  full(wr), full(br), full(cin)],
        out_specs=[row_spec(d), row_spec(LANES), row_spec(LANES), row_spec(LANES),
                   pl.BlockSpec((1, LANES), lambda i: (0, 0))],
        out_shape=[jax.ShapeDtypeStruct((rows, d), F32), jax.ShapeDtypeStruct((rows, LANES), I32),
                   jax.ShapeDtypeStruct((rows, LANES), F32), jax.ShapeDtypeStruct((rows, LANES), I32),
                   jax.ShapeDtypeStruct((1, LANES), F32)],
        scratch_shapes=[pltpu.VMEM((1, LANES), F32)],
        compiler_params=_cparams(("arbitrary",)),
        name="tail_router",
    )(of, od, x2d, wo, g1, b1, wr, br, cin)


EXPERT_BLOCK = 256
DISPATCH_TOKENS = 128


def _dispatch_kernel(dest_ref, x_ref, xs_in, xs_out, sem, *, td):
    del xs_in

    def row_copy(src_row, dst_row):
        return pltpu.make_async_copy(x_ref.at[pl.ds(src_row, 1), :], xs_out.at[pl.ds(dst_row, 1), :], sem)

    def issue(t, c):
        for j in range(TOP_K_EXPERTS):
            row_copy(t, dest_ref[t * TOP_K_EXPERTS + j]).start()
        return c

    lax.fori_loop(0, td, issue, 0)

    def drain(t, c):
        for j in range(TOP_K_EXPERTS):
            row_copy(0, 0).wait()
        return c

    lax.fori_loop(0, td, drain, 0)


def _dispatch(dest_flat, x1, xs):
    rows, d = x1.shape
    td = DISPATCH_TOKENS
    return pl.pallas_call(
        functools.partial(_dispatch_kernel, td=td),
        grid=(rows // td,),
        in_specs=[pl.BlockSpec((td * TOP_K_EXPERTS,), lambda i: (i,), memory_space=pltpu.SMEM),
                  pl.BlockSpec((td, d), lambda i: (i, 0)), pl.BlockSpec(memory_space=pl.ANY)],
        out_specs=pl.BlockSpec(memory_space=pl.ANY),
        out_shape=jax.ShapeDtypeStruct(xs.shape, xs.dtype),
        scratch_shapes=[pltpu.SemaphoreType.DMA(())],
        input_output_aliases={2: 0},
        compiler_params=_cparams(("arbitrary",)),
        name="moe_dispatch",
    )(dest_flat, x1, xs)


def _ffn_kernel(be_ref, first_ref, nused_ref, xs_ref, wg_ref, wu_ref, wd_ref, bg_ref, bu_ref, bd_ref,
                ys_ref, wgb_ref, wub_ref, wdb_ref):
    b = pl.program_id(0)

    @pl.when(first_ref[b] == 1)
    def _():
        wgb_ref[...] = wg_ref[...].astype(BF16)
        wub_ref[...] = wu_ref[...].astype(BF16)
        wdb_ref[...] = wd_ref[...].astype(BF16)

    @pl.when(b < nused_ref[0])
    def _():
        xb = xs_ref[...].astype(BF16)
        g = jnp.dot(xb, wgb_ref[...], preferred_element_type=F32) + bg_ref[...]
        u = jnp.dot(xb, wub_ref[...], preferred_element_type=F32) + bu_ref[...]
        g = jnp.minimum(g, SWIGLU_LIMIT)
        u = jnp.clip(u, -SWIGLU_LIMIT, SWIGLU_LIMIT)
        h = g * (1.0 / (1.0 + jnp.exp(-SWIGLU_ALPHA * g))) * (u + 1.0)
        ys_ref[...] = jnp.dot(h.astype(BF16), wdb_ref[...], preferred_element_type=F32) + bd_ref[...]

    @pl.when(b >= nused_ref[0])
    def _():
        ys_ref[...] = jnp.zeros(ys_ref.shape, F32)


def _expert_ffn(block_e, is_first, n_used, xs, wg, wu, wd, bg, bu, bd):
    p, d = xs.shape
    bs = EXPERT_BLOCK
    de = wg.shape[2]
    w_spec = lambda shape: pl.BlockSpec((None,) + shape, lambda b, be, fi, nu: (be[b], 0, 0))
    return pl.pallas_call(
        _ffn_kernel,
        grid_spec=pltpu.PrefetchScalarGridSpec(
            num_scalar_prefetch=3,
            grid=(p // bs,),
            in_specs=[pl.BlockSpec((bs, d), lambda b, be, fi, nu: (b, 0)),
                      w_spec((d, de)), w_spec((d, de)), w_spec((de, d)),
                      w_spec((1, de)), w_spec((1, de)), w_spec((1, d))],
            out_specs=pl.BlockSpec((bs, d), lambda b, be, fi, nu: (b, 0)),
            scratch_shapes=[pltpu.VMEM((d, de), BF16), pltpu.VMEM((d, de), BF16), pltpu.VMEM((de, d), BF16)]),
        out_shape=jax.ShapeDtypeStruct((p, d), F32),
        compiler_params=_cparams(("arbitrary",)),
        name="moe_ffn",
    )(block_e, is_first, n_used, xs, wg, wu, wd, bg, bu, bd)


def _combine_kernel(dest_ref, ys_hbm, gate_ref, x1_ref, g_ref, b_ref, o_ref, buf_ref, sem, *, tc, alpha):
    def row_copy(src_row, j, t):
        return pltpu.make_async_copy(ys_hbm.at[pl.ds(src_row, 1), :], buf_ref.at[j, pl.ds(t, 1), :], sem)

    def issue(t, c):
        for j in range(TOP_K_EXPERTS):
            row_copy(dest_ref[t * TOP_K_EXPERTS + j], j, t).start()
        return c

    lax.fori_loop(0, tc, issue, 0)

    def drain(t, c):
        for j in range(TOP_K_EXPERTS):
            row_copy(0, j, t).wait()
        return c

    lax.fori_loop(0, tc, drain, 0)

    gates = gate_ref[...]
    moe = gates[:, 0:1] * buf_ref[0]
    for j in range(1, TOP_K_EXPERTS):
        moe = moe + gates[:, j:j + 1] * buf_ref[j]
    o_ref[...] = _layer_norm(alpha * x1_ref[...] + moe, g_ref[...], b_ref[...])


def _combine(dest_flat, ys, gates, x1, g2, b2, alpha):
    rows, d = x1.shape
    tc = DISPATCH_TOKENS
    return pl.pallas_call(
        functools.partial(_combine_kernel, tc=tc, alpha=alpha),
        grid=(rows // tc,),
        in_specs=[pl.BlockSpec((tc * TOP_K_EXPERTS,), lambda i: (i,), memory_space=pltpu.SMEM),
                  pl.BlockSpec(memory_space=pl.ANY),
                  pl.BlockSpec((tc, LANES), lambda i: (i, 0)),
                  pl.BlockSpec((tc, d), lambda i: (i, 0)),
                  pl.BlockSpec((1, d), lambda i: (0, 0)),
                  pl.BlockSpec((1, d), lambda i: (0, 0))],
        out_specs=pl.BlockSpec((tc, d), lambda i: (i, 0)),
        out_shape=jax.ShapeDtypeStruct((rows, d), F32),
        scratch_shapes=[pltpu.VMEM((TOP_K_EXPERTS, tc, d), F32), pltpu.SemaphoreType.DMA(())],
        compiler_params=_cparams(("arbitrary",)),
        name="moe_combine",
    )(dest_flat, ys, gates, x1, g2, b2)


def _prep_w_in(w, b_f):
    d = w.shape[0]
    sizes = (W_FOX, W_FOX, W_FOX, H_FOX, W_DSA, W_DSA, W_DSA, H_IDX * D_IDX, D_IDX, H_IDX)
    parts, acc = [], 0
    for s in sizes:
        parts.append(w[:, acc:acc + s])
        acc += s
    fq, fk, fv, fl, dq, dk, dv, iq, ik, iw = parts
    small = jnp.concatenate([ik, fl, iw, jnp.zeros((d, LANES - D_IDX - H_FOX - H_IDX), w.dtype)], axis=1)
    wp = jnp.concatenate([fq, fk, fv, dq, dk, dv, iq, small], axis=1).astype(BF16)
    bf_row = jnp.zeros((1, LANES), F32).at[0, SM_LF:SM_LF + H_FOX].set(b_f.astype(F32))
    return wp, bf_row


def _rope_tables(pos):
    half = HEAD_DIM // 2
    inv = ROPE_THETA ** (-jnp.arange(half, dtype=F32) / half)
    ang = pos.astype(F32)[:, None] * inv[None, :]
    cos, sin = jnp.cos(ang), jnp.sin(ang)
    cos_t = jnp.concatenate([cos, cos, cos, cos], axis=1)
    sin_t = jnp.concatenate([-sin, sin, -sin, sin], axis=1)
    return cos_t, sin_t


def _pad_rows(a, n):
    return jnp.pad(a, ((0, 0), (0, n - a.shape[1]), (0, 0)))


def kernel(x_prompt, x_sample, cache_fox_k, cache_fox_v, cache_fox_logf, cache_dsa_k, cache_dsa_v,
           cache_dsa_kidx, page_table, w_in, b_f, w_o, ln1_g, ln1_b, w_router, b_router, w_gate, b_gate,
           w_up, b_up, w_down, b_down, ln2_g, ln2_b):
    batch, seq, d = x_prompt.shape
    dec_b, n_q, _ = x_sample.shape
    depth = w_in.shape[0]
    n_pool = cache_fox_k.shape[1]
    n_pages = page_table.shape[1]
    past = n_pages * PAGE_SIZE
    alpha = (2.0 * depth) ** 0.25
    rows_p, rows_s = batch * seq, dec_b * n_q

    cos_p, sin_p = _rope_tables(jnp.arange(seq))
    cos_s, sin_s = _rope_tables(past + (jnp.arange(rows_s) % n_q))

    xp = x_prompt.reshape(rows_p, d)
    xs_ = x_sample.reshape(rows_s, d)
    outs_p = [[] for _ in range(6)]
    outs_s = [[] for _ in range(6)]
    for l in range(depth):
        wp, bf_row = _prep_w_in(w_in[l], b_f[l])
        wo = w_o[l].astype(BF16)
        wr = jnp.pad(w_router[l], ((0, 0), (0, LANES - N_EXPERTS))).astype(BF16)
        br = jnp.full((1, LANES), NEG_INF, F32).at[0, :N_EXPERTS].set(b_router[l].astype(F32))
        g1, b1 = ln1_g[l].reshape(1, d), ln1_b[l].reshape(1, d)
        g2, b2 = ln2_g[l].reshape(1, d), ln2_b[l].reshape(1, d)

        pp = _in_proj(xp, wp, bf_row, cos_p, sin_p, tm=256)
        o_fox = _fox_prompt(pp["fq"], pp["fkb"], pp["fvb"], pp["lft"], batch, seq, tq=256)
        o_dsa = _dsa_prompt(pp["iq"], pp["iw"], pp["ikb"], pp["dq"], pp["dkb"], pp["dvb"], batch, seq,
                            tq=128, ck=512)

        ps = _in_proj(xs_, wp, bf_row, cos_s, sin_s, tm=rows_s)
        pool_fk = cache_fox_k[l].transpose(0, 2, 3, 1)
        pool_fv = cache_fox_v[l].transpose(0, 2, 3, 1)
        pool_lft = jnp.swapaxes(cache_fox_logf[l], 1, 2)
        pool_dk = cache_dsa_k[l].transpose(0, 2, 3, 1)
        pool_dv = cache_dsa_v[l].transpose(0, 2, 3, 1)
        pool_ki = jnp.swapaxes(cache_dsa_kidx[l], 1, 2)
        seq3 = lambda a: a.reshape(dec_b, n_q, a.shape[-1])
        new_t = lambda a: jnp.pad(jnp.swapaxes(seq3(a), 1, 2), ((0, 0), (0, 0), (0, PAGE_SIZE - n_q)))
        so_fox = _sample_fox(page_table, seq3(ps["fq"]), new_t(ps["fk"]), new_t(ps["fv"]), new_t(ps["lf"]),
                             pool_fk, pool_fv, pool_lft)
        iqx = seq3(ps["iq"]).reshape(dec_b, n_q, H_IDX, D_IDX).transpose(0, 2, 1, 3)
        iqx = jnp.pad(iqx, ((0, 0), (0, 0), (0, H_FOX - n_q), (0, 0))).reshape(dec_b, H_IDX * H_FOX, D_IDX)
        iwx = jnp.pad(seq3(ps["iw"]).transpose(0, 2, 1), ((0, 0), (0, 0), (0, H_FOX - n_q)))
        iwx = iwx.reshape(dec_b, H_IDX * H_FOX, 1)
        sel_bias, sel_bias_new = _sample_index(page_table, iqx, iwx, new_t(ps["ik"]), pool_ki, n_q)
        so_dsa = _sample_dsa(page_table, seq3(ps["dq"]), new_t(ps["dk"]), new_t(ps["dv"]), sel_bias,
                             sel_bias_new, pool_dk, pool_dv)

        cin = jnp.zeros((1, LANES), F32)
        x1_p, e_p, gate_p, rank_p, cnt_p = _tail(o_fox, o_dsa, xp, wo, g1, b1, wr, br, cin, alpha, tm=256)
        x1_s, e_s, gate_s, rank_s, cnt = _tail(so_fox.reshape(rows_s, W_FOX), so_dsa.reshape(rows_s, W_DSA),
                                               xs_, wo, g1, b1, wr, br, cnt_p, alpha, tm=rows_s)

        bs = EXPERT_BLOCK
        counts = cnt[0, :N_EXPERTS].astype(I32)
        padded = (counts + bs - 1) // bs * bs
        pad_end = jnp.cumsum(padded)
        pad_start = pad_end - padded
        n_assign = (rows_p + rows_s) * TOP_K_EXPERTS
        n_blocks = -(-(n_assign + N_EXPERTS * (bs - 1)) // bs)
        blk = jnp.arange(n_blocks, dtype=I32)
        n_used = (pad_end[-1] // bs).astype(I32)
        block_e = jnp.sum((pad_end[None, :] <= (blk * bs)[:, None]).astype(I32), axis=1)
        block_e = jnp.minimum(block_e, N_EXPERTS - 1)
        block_e = jnp.where(blk < n_used, block_e, block_e[jnp.maximum(n_used - 1, 0)])
        is_first = jnp.concatenate([jnp.ones((1,), I32), (block_e[1:] != block_e[:-1]).astype(I32)])
        dest_p = (pad_start[e_p[:, :TOP_K_EXPERTS]] + rank_p[:, :TOP_K_EXPERTS]).reshape(-1)
        dest_s = (pad_start[e_s[:, :TOP_K_EXPERTS]] + rank_s[:, :TOP_K_EXPERTS]).reshape(-1)

        xs_sorted = jnp.zeros((n_blocks * bs, d), F32)
        xs_sorted = _dispatch(dest_p, x1_p, xs_sorted)
        xs_sorted = _dispatch(dest_s, x1_s, xs_sorted)
        ys = _expert_ffn(block_e, is_first, n_used.reshape(1), xs_sorted, w_gate[l], w_up[l], w_down[l],
                         b_gate[l].reshape(N_EXPERTS, 1, -1), b_up[l].reshape(N_EXPERTS, 1, -1),
                         b_down[l].reshape(N_EXPERTS, 1, -1))
        xp_new = _combine(dest_p, ys, gate_p, x1_p, g2, b2, alpha)
        xs_new = _combine(dest_s, ys, gate_s, x1_s, g2, b2, alpha)

        for acc, a in zip(outs_p, (pp["fk"].reshape(batch, seq, H_FOX, HEAD_DIM),
                                   pp["fv"].reshape(batch, seq, H_FOX, HEAD_DIM),
                                   pp["lf"].reshape(batch, seq, H_FOX),
                                   pp["dk"].reshape(batch, seq, H_DSA, HEAD_DIM),
                                   pp["dv"].reshape(batch, seq, H_DSA, HEAD_DIM),
                                   pp["ik"].reshape(batch, seq, D_IDX))):
            acc.append(a)
        for acc, a in zip(outs_s, (ps["fk"].reshape(dec_b, n_q, H_FOX, HEAD_DIM),
                                   ps["fv"].reshape(dec_b, n_q, H_FOX, HEAD_DIM),
                                   ps["lf"].reshape(dec_b, n_q, H_FOX),
                                   ps["dk"].reshape(dec_b, n_q, H_DSA, HEAD_DIM),
                                   ps["dv"].reshape(dec_b, n_q, H_DSA, HEAD_DIM),
                                   ps["ik"].reshape(dec_b, n_q, D_IDX))):
            acc.append(a)
        xp, xs_ = xp_new, xs_new

    stack = lambda lst: [jnp.stack(r, axis=0) for r in lst]
    return (xp.reshape(batch, seq, d), xs_.reshape(dec_b, n_q, d), *stack(outs_p), *stack(outs_s))
```

```python
import functools

import jax
import jax.numpy as jnp
from jax import lax
from jax.experimental import pallas as pl
from jax.experimental.pallas import tpu as pltpu

F32, BF16, I32 = jnp.float32, jnp.bfloat16, jnp.int32
NEG_INF = float("-inf")

HEAD_DIM = 64
H_FOX = 8
H_DSA = 8
W_FOX = H_FOX * HEAD_DIM
W_DSA = H_DSA * HEAD_DIM
H_IDX = 16
D_IDX = 64
TOPK_MAX = 256
PAGE_SIZE = 128
ROPE_THETA = 10000.0
ATTN_SCALE = HEAD_DIM ** -0.5
IDX_SCALE = D_IDX ** -0.5
IDX_W_SCALE = H_IDX ** -0.5
N_EXPERTS = 32
TOP_K_EXPERTS = 4
SWIGLU_LIMIT = 7.0
SWIGLU_ALPHA = 1.702
LN_EPS = 1e-5

LANES = 128
KEY_NEG_INF = -2139095041
INT_MIN = -2 ** 31
VMEM_LIMIT = 56 * 1024 * 1024

NT_DIMS = (((1,), (1,)), ((), ()))


def _cparams(sem):
    return pltpu.CompilerParams(dimension_semantics=sem, vmem_limit_bytes=VMEM_LIMIT)


def _split3(x):
    hi = x.astype(BF16)
    r1 = x - hi.astype(F32)
    mid = r1.astype(BF16)
    lo = (r1 - mid.astype(F32)).astype(BF16)
    return hi, mid, lo


def _lane_cumsum(x, upper):
    hi, mid, lo = _split3(x)
    return (jnp.dot(hi, upper, preferred_element_type=F32)
            + jnp.dot(mid, upper, preferred_element_type=F32)
            + jnp.dot(lo, upper, preferred_element_type=F32))


def _upper_tri(n):
    r = lax.broadcasted_iota(I32, (n, n), 0)
    c = lax.broadcasted_iota(I32, (n, n), 1)
    return (r <= c).astype(BF16)


def _sort_key(x):
    bits = pltpu.bitcast(x, I32)
    return bits ^ ((bits >> 31) & 0x7FFFFFFF)


C_FQ, C_FK, C_FV, C_DQ, C_DK, C_DV, C_IQ, C_SM = 0, 512, 1024, 1536, 2048, 2560, 3072, 4096
N_PROJ = 4224
SM_IK, SM_LF, SM_IW = 0, 64, 72


def _in_proj_kernel(x_ref, w_ref, cos_ref, sin_ref, bf_ref,
                    fqt_ref, fkb_ref, fkt_ref, fvt_ref, fvtc_ref, cb_ref, lft_ref,
                    dqt_ref, dkb_ref, dkt_ref, dvt_ref, dvtc_ref, iqt_ref, ikb_ref, ikt_ref, iwt_ref,
                    carry_ref, *, tiles_per_seq):
    @pl.when(pl.program_id(0) % tiles_per_seq == 0)
    def _():
        carry_ref[...] = jnp.zeros(carry_ref.shape, F32)

    xb = x_ref[...].astype(BF16)
    tm = xb.shape[0]
    cos = cos_ref[...]
    sin = sin_ref[...]
    lane = lax.broadcasted_iota(I32, (tm, LANES), 1)
    first_half = (lane & (HEAD_DIM - 1)) < HEAD_DIM // 2

    def mm(c0, n):
        return jnp.dot(xb, w_ref[:, c0:c0 + n], preferred_element_type=F32)

    def rope(y):
        outs = []
        for c in range(0, y.shape[1], LANES):
            yc = y[:, c:c + LANES]
            partner = jnp.where(first_half, pltpu.roll(yc, LANES - HEAD_DIM // 2, 1),
                                pltpu.roll(yc, HEAD_DIM // 2, 1))
            outs.append(yc * cos + partner * sin)
        return outs[0] if len(outs) == 1 else jnp.concatenate(outs, axis=1)

    y = mm(C_FQ, W_FOX)
    fqt_ref[...] = (y * ATTN_SCALE).T.astype(BF16)
    y = mm(C_FK, W_FOX)
    fkb_ref[...] = y.astype(BF16)
    fkt_ref[...] = y.T
    yt = mm(C_FV, W_FOX).T
    fvt_ref[...] = yt
    fvtc_ref[...] = yt.astype(BF16)
    y = rope(mm(C_DQ, W_DSA))
    dqt_ref[...] = (y * ATTN_SCALE).T.astype(BF16)
    y = rope(mm(C_DK, W_DSA))
    dkb_ref[...] = y.astype(BF16)
    dkt_ref[...] = y.T
    yt = mm(C_DV, W_DSA).T
    dvt_ref[...] = yt
    dvtc_ref[...] = yt.astype(BF16)
    y = rope(mm(C_IQ, H_IDX * D_IDX))
    iqt_ref[...] = (y * IDX_SCALE).T.astype(BF16)
    z = mm(C_SM, LANES)
    r = rope(z)
    ikb_ref[...] = r[:, SM_IK:SM_IK + D_IDX].astype(BF16)
    zb = z + bf_ref[...]
    lsg = -(jnp.maximum(-zb, 0.0) + jnp.log1p(jnp.exp(-jnp.abs(zb))))
    small_t = jnp.where(lane < SM_LF, r, jnp.where(lane < SM_IW, lsg, z * IDX_W_SCALE)).T
    ikt_ref[...] = small_t[SM_IK:SM_IK + D_IDX, :]
    lft_ref[...] = small_t[SM_LF:SM_LF + H_FOX, :]
    iwt_ref[...] = small_t[SM_IW:SM_IW + H_IDX, :]

    hi, mid, lo = _split3(jnp.where((lane >= SM_LF) & (lane < SM_IW), lsg, 0.0))
    rr = lax.broadcasted_iota(I32, (tm, tm), 0)
    cc = lax.broadcasted_iota(I32, (tm, tm), 1)
    lower = (cc <= rr).astype(BF16)
    parts = jnp.dot(lower, jnp.concatenate([hi, mid, lo], axis=1), preferred_element_type=F32)
    cs = parts[:, :LANES] + parts[:, LANES:2 * LANES] + parts[:, 2 * LANES:] + carry_ref[...]
    carry_ref[...] = cs[tm - 1:tm, :]
    for h in range(H_FOX):
        cb_ref[h] = jnp.broadcast_to(cs[:, SM_LF + h:SM_LF + h + 1], (tm, LANES))


FOX_KV_CHUNK = 256
DSA_KV_CHUNK = 512


def _in_proj(x2d, wp, bf_row, cos_t, sin_t, n_seq, seq, tm):
    rows, d = x2d.shape
    tps = seq // tm
    ckf, ckd = min(FOX_KV_CHUNK, seq), min(DSA_KV_CHUNK, seq)
    n_tab = cos_t.shape[0] // tm
    t_spec = lambda w: pl.BlockSpec((None, w, tm), lambda i: (i // tps, 0, i % tps))
    n_spec = lambda w: pl.BlockSpec((tm, w), lambda i: (i, 0))
    c_spec = lambda w, ck: pl.BlockSpec((None, None, w, tm),
                                        lambda i: (i // tps, (i % tps) * tm // ck, 0, (i % tps) % (ck // tm)))
    t_shape = lambda w, dt: jax.ShapeDtypeStruct((n_seq, w, seq), dt)
    n_shape = lambda w, dt: jax.ShapeDtypeStruct((rows, w), dt)
    c_shape = lambda w, ck: jax.ShapeDtypeStruct((n_seq, seq // ck, w, ck), BF16)
    wi = H_IDX * D_IDX
    outs = [("fqt", t_spec(W_FOX), t_shape(W_FOX, BF16)), ("fkb", n_spec(W_FOX), n_shape(W_FOX, BF16)),
            ("fkt", t_spec(W_FOX), t_shape(W_FOX, F32)), ("fvt", t_spec(W_FOX), t_shape(W_FOX, F32)),
            ("fvtc", c_spec(W_FOX, ckf), c_shape(W_FOX, ckf)),
            ("cb", pl.BlockSpec((H_FOX, tm, LANES), lambda i: (0, i, 0)),
             jax.ShapeDtypeStruct((H_FOX, rows, LANES), F32)),
            ("lft", t_spec(H_FOX), t_shape(H_FOX, F32)),
            ("dqt", t_spec(W_DSA), t_shape(W_DSA, BF16)), ("dkb", n_spec(W_DSA), n_shape(W_DSA, BF16)),
            ("dkt", t_spec(W_DSA), t_shape(W_DSA, F32)), ("dvt", t_spec(W_DSA), t_shape(W_DSA, F32)),
            ("dvtc", c_spec(W_DSA, ckd), c_shape(W_DSA, ckd)),
            ("iqt", t_spec(wi), t_shape(wi, BF16)), ("ikb", n_spec(D_IDX), n_shape(D_IDX, BF16)),
            ("ikt", t_spec(D_IDX), t_shape(D_IDX, F32)), ("iwt", t_spec(H_IDX), t_shape(H_IDX, F32))]
    res = pl.pallas_call(
        functools.partial(_in_proj_kernel, tiles_per_seq=tps),
        grid=(rows // tm,),
        in_specs=[pl.BlockSpec((tm, d), lambda i: (i, 0)),
                  pl.BlockSpec((d, N_PROJ), lambda i: (0, 0)),
                  pl.BlockSpec((tm, LANES), lambda i: (i % n_tab, 0)),
                  pl.BlockSpec((tm, LANES), lambda i: (i % n_tab, 0)),
                  pl.BlockSpec((1, LANES), lambda i: (0, 0))],
        out_specs=[o[1] for o in outs],
        out_shape=[o[2] for o in outs],
        scratch_shapes=[pltpu.VMEM((1, LANES), F32)],
        compiler_params=_cparams(("arbitrary",)),
        name="in_proj",
    )(x2d, wp, cos_t, sin_t, bf_row)
    return {o[0]: r for o, r in zip(outs, res)}


SUBLANES = 8


def _over_sublanes(x8, reduce_fn):
    return jnp.broadcast_to(reduce_fn(x8, axis=0, keepdims=True), x8.shape)


def _by_sublane_tile(x):
    return x.reshape(x.shape[0] // SUBLANES, SUBLANES, x.shape[1])


def _flash_update(h, s, vt, m_ref, l_ref, acc_ref):
    tk, tq = s.shape
    s3 = _by_sublane_tile(s)
    m_old = m_ref[h]
    m_new = jnp.maximum(m_old, _over_sublanes(jnp.max(s3, axis=0), jnp.max))
    alpha = jnp.exp(m_old - m_new)
    p3 = jnp.exp(s3 - m_new[None])
    l_ref[h] = alpha * l_ref[h] + _over_sublanes(jnp.sum(p3, axis=0), jnp.sum)
    pv = jnp.dot(vt, p3.reshape(tk, tq).astype(BF16), preferred_element_type=F32)
    acc_ref[h] = (_by_sublane_tile(acc_ref[h]) * alpha[None]).reshape(HEAD_DIM, tq) + pv
    m_ref[h] = m_new


def _flash_init(m_ref, l_ref, acc_ref):
    m_ref[...] = jnp.full(m_ref.shape, -1e30, F32)
    l_ref[...] = jnp.zeros(l_ref.shape, F32)
    acc_ref[...] = jnp.zeros(acc_ref.shape, F32)


def _flash_finish(o_ref, l_ref, acc_ref, n_heads):
    tq = acc_ref.shape[2]
    for pair in range(n_heads // 2):
        parts = [(_by_sublane_tile(acc_ref[h]) / l_ref[h][None]).reshape(HEAD_DIM, tq)
                 for h in (2 * pair, 2 * pair + 1)]
        o_ref[:, pair * LANES:(pair + 1) * LANES] = jnp.concatenate(parts, axis=0).T.astype(BF16)


def _flash_scratch(n_heads, tq):
    return [pltpu.VMEM((n_heads, SUBLANES, tq), F32), pltpu.VMEM((n_heads, SUBLANES, tq), F32),
            pltpu.VMEM((n_heads, HEAD_DIM, tq), F32), pltpu.VMEM((n_heads, LANES, tq), BF16)]


def _stage_pair_masked_q(qt_ref, qpm_ref, n_heads):
    for h in range(n_heads):
        pair = h // 2
        rows = qt_ref[pair * LANES:(pair + 1) * LANES, :].astype(F32)
        r = lax.broadcasted_iota(I32, rows.shape, 0)
        qpm_ref[h] = jnp.where(r // HEAD_DIM == h % 2, rows, 0.0).astype(BF16)


def _fox_kernel(qt_ref, k_ref, vt_ref, cb_ref, o_ref, m_ref, l_ref, acc_ref, qpm_ref, *, tq):
    i = pl.program_id(1)
    _stage_pair_masked_q(qt_ref, qpm_ref, H_FOX)
    _flash_init(m_ref, l_ref, acc_ref)
    key_row = lax.broadcasted_iota(I32, (tq, tq), 0)
    query_col = lax.broadcasted_iota(I32, (tq, tq), 1)
    visible = key_row <= query_col

    def kv_step(j, masked):
        start = pl.multiple_of(j * tq, tq)
        for h in range(H_FOX):
            pair = h // 2
            s = jnp.dot(k_ref[pl.ds(start, tq), pair * LANES:(pair + 1) * LANES], qpm_ref[h],
                        preferred_element_type=F32)
            s = s - jnp.concatenate([cb_ref[h, pl.ds(start, tq), :]] * (tq // LANES), axis=1)
            if masked:
                s = jnp.where(visible, s, NEG_INF)
            _flash_update(h, s, vt_ref[j, h * HEAD_DIM:(h + 1) * HEAD_DIM, :], m_ref, l_ref, acc_ref)

    def body(j, c):
        kv_step(j, False)
        return c

    lax.fori_loop(0, i, body, 0)
    kv_step(i, True)
    _flash_finish(o_ref, l_ref, acc_ref, H_FOX)


def _fox_prompt(qt, kb, vtc, cb, batch, seq):
    tq = FOX_KV_CHUNK
    nq = seq // tq
    return pl.pallas_call(
        functools.partial(_fox_kernel, tq=tq),
        grid=(batch, nq),
        in_specs=[pl.BlockSpec((None, W_FOX, tq), lambda b, i: (b, 0, i)),
                  pl.BlockSpec((seq, W_FOX), lambda b, i: (b, 0)),
                  pl.BlockSpec((None, nq, W_FOX, tq), lambda b, i: (b, 0, 0, 0)),
                  pl.BlockSpec((H_FOX, seq, LANES), lambda b, i: (0, b, 0))],
        out_specs=pl.BlockSpec((tq, W_FOX), lambda b, i: (b * nq + i, 0)),
        out_shape=jax.ShapeDtypeStruct((batch * seq, W_FOX), BF16),
        scratch_shapes=_flash_scratch(H_FOX, tq),
        compiler_params=_cparams(("arbitrary", "arbitrary")),
        name="fox_prompt",
    )(qt, kb, vtc, cb)


def _dsa_kernel(iqt_ref, iwt_ref, ik_ref, qt_ref, k_ref, vt_ref, o_ref, key_ref, bias_ref,
                m_ref, l_ref, acc_ref, qpm_ref, *, tq, ck, topk):
    i = pl.program_id(1)
    q0 = i * tq
    n_chunks = (q0 + tq + ck - 1) // ck
    key_row = lax.broadcasted_iota(I32, (ck, tq), 0)
    query_pos = q0 + lax.broadcasted_iota(I32, (ck, tq), 1)
    iw8 = [jnp.broadcast_to(iwt_ref[h:h + 1, :], (SUBLANES, tq)) for h in range(H_IDX)]

    def score_chunk(c, _):
        start = pl.multiple_of(c * ck, ck)
        ikc = ik_ref[pl.ds(start, ck), :]
        sc = jnp.zeros((ck // SUBLANES, SUBLANES, tq), F32)
        for h in range(H_IDX):
            rel = jnp.dot(ikc, iqt_ref[h * D_IDX:(h + 1) * D_IDX, :], preferred_element_type=F32)
            sc = sc + _by_sublane_tile(jnp.maximum(rel, 0.0)) * iw8[h][None]
        sc = jnp.where(key_row + start <= query_pos, sc.reshape(ck, tq), NEG_INF)
        key_ref[c] = _sort_key(sc)
        return 0

    lax.fori_loop(0, n_chunks, score_chunk, 0)

    def count(pred):
        def body(c, acc):
            return acc + jnp.sum(pred(_by_sublane_tile(key_ref[c])).astype(F32), axis=0)
        return _over_sublanes(lax.fori_loop(0, n_chunks, body, jnp.zeros((SUBLANES, tq), F32)), jnp.sum)

    def bit_step(bi, thr):
        cand = thr + lax.shift_left(jnp.int32(1), 31 - bi)
        return jnp.where(count(lambda k: k >= cand[None]) >= topk, cand, thr)

    thr = lax.fori_loop(0, 32, bit_step, jnp.full((SUBLANES, tq), INT_MIN, I32))
    need = topk - count(lambda k: k > thr[None])
    ties = count(lambda k: k == thr[None])
    excess = jnp.max(jnp.where((ties > need) & (thr > KEY_NEG_INF), 1.0, 0.0)) > 0.0

    @pl.when(jnp.logical_not(excess))
    def _():
        def fill(c, _):
            k = _by_sublane_tile(key_ref[c])
            bias_ref[c] = jnp.where((k >= thr[None]) & (k > KEY_NEG_INF), 0.0, NEG_INF).reshape(ck, tq)
            return 0
        lax.fori_loop(0, n_chunks, fill, 0)

    @pl.when(excess)
    def _():
        rr = lax.broadcasted_iota(I32, (ck, ck), 0)
        cc = lax.broadcasted_iota(I32, (ck, ck), 1)
        strict_lower = (cc < rr).astype(BF16)

        def fill(c, before):
            k = _by_sublane_tile(key_ref[c])
            tie = k == thr[None]
            tie_f = tie.astype(F32)
            earlier = jnp.dot(strict_lower, tie_f.reshape(ck, tq).astype(BF16), preferred_element_type=F32)
            earlier = _by_sublane_tile(earlier) + before[None]
            sel = (k > thr[None]) | (tie & (earlier < need[None]))
            bias_ref[c] = jnp.where(sel & (k > KEY_NEG_INF), 0.0, NEG_INF).reshape(ck, tq)
            return before + _over_sublanes(jnp.sum(tie_f, axis=0), jnp.sum)
        lax.fori_loop(0, n_chunks, fill, jnp.zeros((SUBLANES, tq), F32))

    _stage_pair_masked_q(qt_ref, qpm_ref, H_DSA)
    _flash_init(m_ref, l_ref, acc_ref)

    def kv_step(c, carry):
        start = pl.multiple_of(c * ck, ck)
        bias = bias_ref[c]
        for h in range(H_DSA):
            pair = h // 2
            s = jnp.dot(k_ref[pl.ds(start, ck), pair * LANES:(pair + 1) * LANES], qpm_ref[h],
                        preferred_element_type=F32) + bias
            _flash_update(h, s, vt_ref[c, h * HEAD_DIM:(h + 1) * HEAD_DIM, :], m_ref, l_ref, acc_ref)
        return carry

    lax.fori_loop(0, n_chunks, kv_step, 0)
    _flash_finish(o_ref, l_ref, acc_ref, H_DSA)


def _dsa_prompt(iqt, iwt, ikb, dqt, dkb, dvtc, batch, seq, tq):
    ck = dvtc.shape[3]
    nq = seq // tq
    topk = min(TOPK_MAX, seq // 4)
    return pl.pallas_call(
        functools.partial(_dsa_kernel, tq=tq, ck=ck, topk=topk),
        grid=(batch, nq),
        in_specs=[pl.BlockSpec((None, H_IDX * D_IDX, tq), lambda b, i: (b, 0, i)),
                  pl.BlockSpec((None, H_IDX, tq), lambda b, i: (b, 0, i)),
                  pl.BlockSpec((seq, D_IDX), lambda b, i: (b, 0)),
                  pl.BlockSpec((None, W_DSA, tq), lambda b, i: (b, 0, i)),
                  pl.BlockSpec((seq, W_DSA), lambda b, i: (b, 0)),
                  pl.BlockSpec((None, seq // ck, W_DSA, ck), lambda b, i: (b, 0, 0, 0))],
        out_specs=pl.BlockSpec((tq, W_DSA), lambda b, i: (b * nq + i, 0)),
        out_shape=jax.ShapeDtypeStruct((batch * seq, W_DSA), BF16),
        scratch_shapes=[pltpu.VMEM((seq // ck, ck, tq), I32), pltpu.VMEM((seq // ck, ck, tq), F32)]
        + _flash_scratch(H_DSA, tq),
        compiler_params=_cparams(("arbitrary", "arbitrary")),
        name="dsa_prompt",
    )(iqt, iwt, ikb, dqt, dkb, dvtc)


KV_PAGES_PER_STEP = 16
IDX_PAGES_PER_STEP = 32


def _head_mask(rows, width):
    r = lax.broadcasted_iota(I32, (rows, width), 0)
    c = lax.broadcasted_iota(I32, (rows, width), 1)
    return (c // HEAD_DIM) == (r % H_FOX)


def _block_diag_q(q, n_q):
    hm = _head_mask(H_FOX, q.shape[1])
    qf = q.astype(F32)
    parts = [jnp.where(hm, jnp.broadcast_to(qf[t:t + 1, :], hm.shape), 0.0) for t in range(n_q)]
    return jnp.concatenate(parts, axis=0).astype(BF16)


def _rows_to_groups(x, n_q):
    return jnp.concatenate([jnp.broadcast_to(x[t:t + 1, :], (H_FOX, x.shape[1])) for t in range(n_q)], axis=0)


def _softmax_page(s, vt_bf16, m_ref, l_ref, acc_ref):
    m = m_ref[...]
    m_new = jnp.maximum(m, jnp.max(s, axis=1, keepdims=True))
    alpha = jnp.exp(m - m_new)
    p = jnp.exp(s - m_new)
    l_ref[...] = alpha * l_ref[...] + jnp.sum(p, axis=1, keepdims=True)
    acc_ref[...] = alpha * acc_ref[...] + lax.dot_general(p.astype(BF16), vt_bf16, NT_DIMS,
                                                          preferred_element_type=F32)
    m_ref[...] = m_new


def _stack_pages(refs):
    return jnp.concatenate([r[...].reshape(-1, PAGE_SIZE) for r in refs], axis=1).astype(BF16)


def _finish_decode(o_ref, l_ref, acc_ref, n_q):
    o = acc_ref[...] / l_ref[...]
    o = jnp.where(_head_mask(o.shape[0], o.shape[1]), o, 0.0)
    o_ref[...] = jnp.sum(o.reshape(n_q, H_FOX, o.shape[1]), axis=1).astype(BF16)


def _sample_fox_kernel(pt_ref, q_ref, kn_ref, vn_ref, lfn_ref, *rest, n_q, pps):
    k_refs, v_refs, lf_refs = rest[:pps], rest[pps:2 * pps], rest[2 * pps:3 * pps]
    o_ref, m_ref, l_ref, acc_ref, cc_ref, qbd_ref = rest[3 * pps:]
    step = pl.program_id(1)
    rows = n_q * H_FOX

    @pl.when(step == 0)
    def _():
        m_ref[...] = jnp.full(m_ref.shape, -1e30, F32)
        l_ref[...] = jnp.zeros(l_ref.shape, F32)
        acc_ref[...] = jnp.zeros(acc_ref.shape, F32)
        cc_ref[...] = jnp.zeros(cc_ref.shape, F32)
        qbd_ref[...] = _block_diag_q(q_ref[...], n_q)

    upper = _upper_tri(PAGE_SIZE)

    within = _lane_cumsum(jnp.concatenate([r[...] for r in lf_refs], axis=0), upper)
    off = cc_ref[...]
    parts = []
    for p in range(pps):
        cp = within[p * H_FOX:(p + 1) * H_FOX] + off
        parts.append(cp)
        off = cp[:, PAGE_SIZE - 1:PAGE_SIZE]
    cc_ref[...] = off
    cbias = jnp.concatenate(parts, axis=1)
    s = jnp.dot(qbd_ref[...], _stack_pages(k_refs), preferred_element_type=F32)
    s = s - jnp.concatenate([cbias] * n_q, axis=0)
    _softmax_page(s, _stack_pages(v_refs), m_ref, l_ref, acc_ref)

    @pl.when(step == pl.num_programs(1) - 1)
    def _():
        r = lax.broadcasted_iota(I32, (rows, PAGE_SIZE), 0)
        c = lax.broadcasted_iota(I32, (rows, PAGE_SIZE), 1)
        cn = _lane_cumsum(lfn_ref[...], upper) + cc_ref[...]
        sn = jnp.dot(qbd_ref[...], kn_ref[...].astype(BF16), preferred_element_type=F32)
        sn = sn - jnp.concatenate([cn] * n_q, axis=0)
        sn = jnp.where(c <= r // H_FOX, sn, NEG_INF)
        _softmax_page(sn, vn_ref[...].astype(BF16), m_ref, l_ref, acc_ref)
        _finish_decode(o_ref, l_ref, acc_ref, n_q)


def _page_spec(block, p, pps):
    nd = len(block)
    return pl.BlockSpec((None,) + block, lambda b, s, pt: (pt[b, s * pps + p],) + (0,) * nd)


def _sample_fox(page_table, q, k_new, v_new, lft_new, pool_k, pool_v, pool_lft):
    dec_b, n_q, w = q.shape
    n_pages = page_table.shape[1]
    pps = min(KV_PAGES_PER_STEP, n_pages)
    rows = n_q * H_FOX
    seq_spec = lambda blk: pl.BlockSpec((None,) + blk, lambda b, s, pt: (b,) + (0,) * len(blk))
    kv_page = (H_FOX, HEAD_DIM, PAGE_SIZE)
    in_specs = ([seq_spec((n_q, w)), seq_spec((w, PAGE_SIZE)), seq_spec((w, PAGE_SIZE)),
                 seq_spec((H_FOX, PAGE_SIZE))]
                + [_page_spec(kv_page, p, pps) for p in range(pps)]
                + [_page_spec(kv_page, p, pps) for p in range(pps)]
                + [_page_spec((H_FOX, PAGE_SIZE), p, pps) for p in range(pps)])
    return pl.pallas_call(
        functools.partial(_sample_fox_kernel, n_q=n_q, pps=pps),
        grid_spec=pltpu.PrefetchScalarGridSpec(
            num_scalar_prefetch=1,
            grid=(dec_b, n_pages // pps),
            in_specs=in_specs,
            out_specs=pl.BlockSpec((None, n_q, w), lambda b, s, pt: (b, 0, 0)),
            scratch_shapes=[pltpu.VMEM((rows, 1), F32), pltpu.VMEM((rows, 1), F32),
                            pltpu.VMEM((rows, w), F32), pltpu.VMEM((H_FOX, 1), F32),
                            pltpu.VMEM((rows, w), BF16)]),
        out_shape=jax.ShapeDtypeStruct((dec_b, n_q, w), BF16),
        compiler_params=_cparams(("arbitrary", "arbitrary")),
        name="sample_fox",
    )(page_table, q, k_new, v_new, lft_new, *([pool_k] * pps), *([pool_v] * pps), *([pool_lft] * pps))


def _sample_index_kernel(pt_ref, iq_ref, iw_ref, kn_ref, *rest, pps, n_q, topk):
    k_refs = rest[:pps]
    bias_ref, biasn_ref, key_ref, keyn_ref = rest[pps:]
    step = pl.program_id(1)
    n_steps = key_ref.shape[0]
    iq = iq_ref[...]
    iw = iw_ref[...]

    def scores(kt):
        rel = jnp.dot(iq, kt, preferred_element_type=F32)
        w = iw * jnp.maximum(rel, 0.0)
        return jnp.sum(w.reshape(H_IDX, H_FOX, kt.shape[1]), axis=0)

    key_ref[step] = _sort_key(scores(_stack_pages(k_refs)))

    @pl.when(step == n_steps - 1)
    def _():
        r = lax.broadcasted_iota(I32, (H_FOX, PAGE_SIZE), 0)
        c = lax.broadcasted_iota(I32, (H_FOX, PAGE_SIZE), 1)
        keyn_ref[...] = _sort_key(jnp.where((c <= r) & (r < n_q), scores(kn_ref[...].astype(BF16)), NEG_INF))

        def count(pred):
            past = jnp.sum(jnp.sum(pred(key_ref[...]).astype(F32), axis=0), axis=1, keepdims=True)
            return past + jnp.sum(pred(keyn_ref[...]).astype(F32), axis=1, keepdims=True)

        def bit_step(bi, thr):
            cand = thr + lax.shift_left(jnp.int32(1), 31 - bi)
            return jnp.where(count(lambda k: k >= cand) >= topk, cand, thr)

        thr = lax.fori_loop(0, 32, bit_step, jnp.full((H_FOX, 1), INT_MIN, I32))
        need = topk - count(lambda k: k > thr)
        ties = count(lambda k: k == thr)
        excess = jnp.max(jnp.where((ties > need) & (thr > KEY_NEG_INF), 1.0, 0.0)) > 0.0

        @pl.when(jnp.logical_not(excess))
        def _():
            k = key_ref[...]
            bias_ref[...] = jnp.where((k >= thr) & (k > KEY_NEG_INF), 0.0, NEG_INF)
            k = keyn_ref[...]
            biasn_ref[...] = jnp.where((k >= thr) & (k > KEY_NEG_INF), 0.0, NEG_INF)

        @pl.when(excess)
        def _():
            upper = _upper_tri(PAGE_SIZE)

            def tie_block(k, before):
                tie = k == thr
                incl = jnp.dot(tie.astype(BF16), upper, preferred_element_type=F32) + before
                sel = (k > thr) | (tie & (incl - tie.astype(F32) < need))
                return jnp.where(sel & (k > KEY_NEG_INF), 0.0, NEG_INF), incl[:, PAGE_SIZE - 1:PAGE_SIZE]

            def fill(c, before):
                for p in range(pps):
                    lanes = slice(p * PAGE_SIZE, (p + 1) * PAGE_SIZE)
                    b, before = tie_block(key_ref[c, :, lanes], before)
                    bias_ref[c, :, lanes] = b
                return before

            before = lax.fori_loop(0, n_steps, fill, jnp.zeros((H_FOX, 1), F32))
            biasn_ref[...] = tie_block(keyn_ref[...], before)[0]


def _sample_index(page_table, iqx, iwx, kidx_new, pool_kidx, n_q):
    dec_b = iqx.shape[0]
    n_pages = page_table.shape[1]
    pps = min(IDX_PAGES_PER_STEP, n_pages)
    n_steps = n_pages // pps
    step_w = pps * PAGE_SIZE
    topk = min(TOPK_MAX, (n_pages * PAGE_SIZE + n_q) // 4)
    seq_spec = lambda blk: pl.BlockSpec((None,) + blk, lambda b, s, pt: (b,) + (0,) * len(blk))
    in_specs = ([seq_spec((H_IDX * H_FOX, D_IDX)), seq_spec((H_IDX * H_FOX, 1)), seq_spec((D_IDX, PAGE_SIZE))]
                + [_page_spec((D_IDX, PAGE_SIZE), p, pps) for p in range(pps)])
    return pl.pallas_call(
        functools.partial(_sample_index_kernel, pps=pps, n_q=n_q, topk=topk),
        grid_spec=pltpu.PrefetchScalarGridSpec(
            num_scalar_prefetch=1,
            grid=(dec_b, n_steps),
            in_specs=in_specs,
            out_specs=[pl.BlockSpec((None, n_steps, H_FOX, step_w), lambda b, s, pt: (b, 0, 0, 0)),
                       pl.BlockSpec((None, H_FOX, PAGE_SIZE), lambda b, s, pt: (b, 0, 0))],
            scratch_shapes=[pltpu.VMEM((n_steps, H_FOX, step_w), I32), pltpu.VMEM((H_FOX, PAGE_SIZE), I32)]),
        out_shape=[jax.ShapeDtypeStruct((dec_b, n_steps, H_FOX, step_w), F32),
                   jax.ShapeDtypeStruct((dec_b, H_FOX, PAGE_SIZE), F32)],
        compiler_params=_cparams(("arbitrary", "arbitrary")),
        name="sample_index",
    )(page_table, iqx, iwx, kidx_new, *([pool_kidx] * pps))


def _sample_dsa_kernel(pt_ref, q_ref, kn_ref, vn_ref, bias_ref, biasn_ref, *rest, n_q, pps):
    k_refs, v_refs = rest[:pps], rest[pps:2 * pps]
    o_ref, m_ref, l_ref, acc_ref, qbd_ref = rest[2 * pps:]
    step = pl.program_id(1)

    @pl.when(step == 0)
    def _():
        m_ref[...] = jnp.full(m_ref.shape, -1e30, F32)
        l_ref[...] = jnp.zeros(l_ref.shape, F32)
        acc_ref[...] = jnp.zeros(acc_ref.shape, F32)
        qbd_ref[...] = _block_diag_q(q_ref[...], n_q)

    def attend(kt, vt, bias):
        s = jnp.dot(qbd_ref[...], kt, preferred_element_type=F32) + _rows_to_groups(bias, n_q)
        _softmax_page(s, vt, m_ref, l_ref, acc_ref)

    attend(_stack_pages(k_refs), _stack_pages(v_refs), bias_ref[0])

    @pl.when(step == pl.num_programs(1) - 1)
    def _():
        attend(kn_ref[...].astype(BF16), vn_ref[...].astype(BF16), biasn_ref[...])
        _finish_decode(o_ref, l_ref, acc_ref, n_q)


def _sample_dsa(page_table, q, k_new, v_new, bias, bias_new, pool_k, pool_v):
    dec_b, n_q, w = q.shape
    n_pages = page_table.shape[1]
    pps = min(KV_PAGES_PER_STEP, n_pages)
    rows = n_q * H_DSA
    kv_page = (H_DSA, HEAD_DIM, PAGE_SIZE)
    per_bias_chunk = bias.shape[3] // (pps * PAGE_SIZE)
    seq_spec = lambda blk: pl.BlockSpec((None,) + blk, lambda b, s, pt: (b,) + (0,) * len(blk))
    in_specs = ([seq_spec((n_q, w)), seq_spec((w, PAGE_SIZE)), seq_spec((w, PAGE_SIZE)),
                 pl.BlockSpec((None, 1, H_FOX, pps * PAGE_SIZE),
                              lambda b, s, pt: (b, s // per_bias_chunk, 0, s % per_bias_chunk)),
                 seq_spec((H_FOX, PAGE_SIZE))]
                + [_page_spec(kv_page, p, pps) for p in range(pps)]
                + [_page_spec(kv_page, p, pps) for p in range(pps)])
    return pl.pallas_call(
        functools.partial(_sample_dsa_kernel, n_q=n_q, pps=pps),
        grid_spec=pltpu.PrefetchScalarGridSpec(
            num_scalar_prefetch=1,
            grid=(dec_b, n_pages // pps),
            in_specs=in_specs,
            out_specs=pl.BlockSpec((None, n_q, w), lambda b, s, pt: (b, 0, 0)),
            scratch_shapes=[pltpu.VMEM((rows, 1), F32), pltpu.VMEM((rows, 1), F32),
                            pltpu.VMEM((rows, w), F32), pltpu.VMEM((rows, w), BF16)]),
        out_shape=jax.ShapeDtypeStruct((dec_b, n_q, w), BF16),
        compiler_params=_cparams(("arbitrary", "arbitrary")),
        name="sample_dsa",
    )(page_table, q, k_new, v_new, bias, bias_new, *([pool_k] * pps), *([pool_v] * pps))


def _layer_norm(r, g, b):
    mu = jnp.mean(r, axis=1, keepdims=True)
    d = r - mu
    var = jnp.mean(d * d, axis=1, keepdims=True)
    return d * lax.rsqrt(var + LN_EPS) * g + b


def _tail_kernel(of_ref, od_ref, x_ref, wo_ref, g_ref, b_ref, wr_ref, br_ref, cin_ref,
                 x1_ref, e_ref, gate_ref, rank_ref, cnt_ref, carry_ref, *, alpha):
    i = pl.program_id(0)
    tm = x_ref.shape[0]

    @pl.when(i == 0)
    def _():
        carry_ref[...] = cin_ref[...]

    r = (alpha * x_ref[...]
         + jnp.dot(of_ref[...], wo_ref[0:W_FOX, :], preferred_element_type=F32)
         + jnp.dot(od_ref[...], wo_ref[W_FOX:W_FOX + W_DSA, :], preferred_element_type=F32))
    x1 = _layer_norm(r, g_ref[...], b_ref[...])
    x1_ref[...] = x1

    logits = jnp.dot(x1.astype(BF16), wr_ref[...], preferred_element_type=F32) + br_ref[...]
    lane = lax.broadcasted_iota(I32, (tm, LANES), 1)
    lane_f = lane.astype(F32)
    vals, idxs, hots = [], [], []
    lg = logits
    for _ in range(TOP_K_EXPERTS):
        m = jnp.max(lg, axis=1, keepdims=True)
        idx = jnp.min(jnp.where(lg == m, lane_f, float(LANES)), axis=1, keepdims=True)
        hot = lane_f == idx
        vals.append(m)
        idxs.append(idx)
        hots.append(hot)
        lg = jnp.where(hot, NEG_INF, lg)
    exps = [jnp.exp(v - vals[0]) for v in vals]
    den = exps[0]
    for e in exps[1:]:
        den = den + e
    sel = jnp.zeros((tm, LANES), F32)
    for hot in hots:
        sel = sel + hot.astype(F32)
    rr = lax.broadcasted_iota(I32, (tm, tm), 0)
    cc = lax.broadcasted_iota(I32, (tm, tm), 1)
    strict_lower = (cc < rr).astype(BF16)
    before = jnp.dot(strict_lower, sel.astype(BF16), preferred_element_type=F32) + carry_ref[...]
    e_out = jnp.zeros((tm, LANES), F32)
    g_out = jnp.zeros((tm, LANES), F32)
    r_out = jnp.zeros((tm, LANES), F32)
    for j in range(TOP_K_EXPERTS):
        rank = jnp.sum(jnp.where(hots[j], before, 0.0), axis=1, keepdims=True)
        e_out = jnp.where(lane == j, idxs[j], e_out)
        g_out = jnp.where(lane == j, exps[j] / den, g_out)
        r_out = jnp.where(lane == j, rank, r_out)
    e_ref[...] = e_out.astype(I32)
    gate_ref[...] = g_out
    rank_ref[...] = r_out.astype(I32)
    carry_ref[...] = carry_ref[...] + jnp.sum(sel, axis=0, keepdims=True)
    cnt_ref[...] = carry_ref[...]


def _tail(of, od, x2d, wo, g1, b1, wr, br, cin, alpha, tm):
    rows, d = x2d.shape
    row_spec = lambda w: pl.BlockSpec((tm, w), lambda i: (i, 0))
    full = lambda a: pl.BlockSpec(a.shape, lambda i: (0,) * a.ndim)
    return pl.pallas_call(
        functools.partial(_tail_kernel, alpha=alpha),
        grid=(rows // tm,),
        in_specs=[row_spec(W_FOX), row_spec(W_DSA), row_spec(d), full(wo), full(g1), full(b1),
                  full(wr), full(br), full(cin)],
        out_specs=[row_spec(d), row_spec(LANES), row_spec(LANES), row_spec(LANES),
                   pl.BlockSpec((1, LANES), lambda i: (0, 0))],
        out_shape=[jax.ShapeDtypeStruct((rows, d), F32), jax.ShapeDtypeStruct((rows, LANES), I32),
                   jax.ShapeDtypeStruct((rows, LANES), F32), jax.ShapeDtypeStruct((rows, LANES), I32),
                   jax.ShapeDtypeStruct((1, LANES), F32)],
        scratch_shapes=[pltpu.VMEM((1, LANES), F32)],
        compiler_params=_cparams(("arbitrary",)),
        name="tail_router",
    )(of, od, x2d, wo, g1, b1, wr, br, cin)


EXPERT_BLOCK = 256
DISPATCH_TOKENS = 128


def _dispatch_kernel(dest_ref, x_ref, xs_in, xs_out, sem, *, td):
    del xs_in

    def row_copy(src_row, dst_row):
        return pltpu.make_async_copy(x_ref.at[pl.ds(src_row, 1), :], xs_out.at[pl.ds(dst_row, 1), :], sem)

    def issue(t, c):
        for j in range(TOP_K_EXPERTS):
            row_copy(t, dest_ref[t * TOP_K_EXPERTS + j]).start()
        return c

    lax.fori_loop(0, td, issue, 0)

    def drain(t, c):
        for j in range(TOP_K_EXPERTS):
            row_copy(0, 0).wait()
        return c

    lax.fori_loop(0, td, drain, 0)


def _dispatch(dest_flat, x1, xs):
    rows, d = x1.shape
    td = DISPATCH_TOKENS
    return pl.pallas_call(
        functools.partial(_dispatch_kernel, td=td),
        grid=(rows // td,),
        in_specs=[pl.BlockSpec((td * TOP_K_EXPERTS,), lambda i: (i,), memory_space=pltpu.SMEM),
                  pl.BlockSpec((td, d), lambda i: (i, 0)), pl.BlockSpec(memory_space=pl.ANY)],
        out_specs=pl.BlockSpec(memory_space=pl.ANY),
        out_shape=jax.ShapeDtypeStruct(xs.shape, xs.dtype),
        scratch_shapes=[pltpu.SemaphoreType.DMA(())],
        input_output_aliases={2: 0},
        compiler_params=_cparams(("arbitrary",)),
        name="moe_dispatch",
    )(dest_flat, x1, xs)


def _ffn_kernel(be_ref, first_ref, nused_ref, xs_ref, wg_ref, wu_ref, wd_ref, bg_ref, bu_ref, bd_ref,
                ys_ref, wgb_ref, wub_ref, wdb_ref):
    b = pl.program_id(0)

    @pl.when(first_ref[b] == 1)
    def _():
        wgb_ref[...] = wg_ref[...].astype(BF16)
        wub_ref[...] = wu_ref[...].astype(BF16)
        wdb_ref[...] = wd_ref[...].astype(BF16)

    @pl.when(b < nused_ref[0])
    def _():
        xb = xs_ref[...].astype(BF16)
        g = jnp.dot(xb, wgb_ref[...], preferred_element_type=F32) + bg_ref[...]
        u = jnp.dot(xb, wub_ref[...], preferred_element_type=F32) + bu_ref[...]
        g = jnp.minimum(g, SWIGLU_LIMIT)
        u = jnp.clip(u, -SWIGLU_LIMIT, SWIGLU_LIMIT)
        h = g * (1.0 / (1.0 + jnp.exp(-SWIGLU_ALPHA * g))) * (u + 1.0)
        ys_ref[...] = jnp.dot(h.astype(BF16), wdb_ref[...], preferred_element_type=F32) + bd_ref[...]

    @pl.when(b >= nused_ref[0])
    def _():
        ys_ref[...] = jnp.zeros(ys_ref.shape, F32)


def _expert_ffn(block_e, is_first, n_used, xs, wg, wu, wd, bg, bu, bd):
    p, d = xs.shape
    bs = EXPERT_BLOCK
    de = wg.shape[2]
    w_spec = lambda shape: pl.BlockSpec((None,) + shape, lambda b, be, fi, nu: (be[b], 0, 0))
    return pl.pallas_call(
        _ffn_kernel,
        grid_spec=pltpu.PrefetchScalarGridSpec(
            num_scalar_prefetch=3,
            grid=(p // bs,),
            in_specs=[pl.BlockSpec((bs, d), lambda b, be, fi, nu: (b, 0)),
                      w_spec((d, de)), w_spec((d, de)), w_spec((de, d)),
                      w_spec((1, de)), w_spec((1, de)), w_spec((1, d))],
            out_specs=pl.BlockSpec((bs, d), lambda b, be, fi, nu: (b, 0)),
            scratch_shapes=[pltpu.VMEM((d, de), BF16), pltpu.VMEM((d, de), BF16), pltpu.VMEM((de, d), BF16)]),
        out_shape=jax.ShapeDtypeStruct((p, d), F32),
        compiler_params=_cparams(("arbitrary",)),
        name="moe_ffn",
    )(block_e, is_first, n_used, xs, wg, wu, wd, bg, bu, bd)


def _combine_kernel(dest_ref, ys_hbm, gate_ref, x1_ref, g_ref, b_ref, o_ref, buf_ref, sem, *, tc, alpha):
    def row_copy(src_row, j, t):
        return pltpu.make_async_copy(ys_hbm.at[pl.ds(src_row, 1), :], buf_ref.at[j, pl.ds(t, 1), :], sem)

    def issue(t, c):
        for j in range(TOP_K_EXPERTS):
            row_copy(dest_ref[t * TOP_K_EXPERTS + j], j, t).start()
        return c

    lax.fori_loop(0, tc, issue, 0)

    def drain(t, c):
        for j in range(TOP_K_EXPERTS):
            row_copy(0, j, t).wait()
        return c

    lax.fori_loop(0, tc, drain, 0)

    gates = gate_ref[...]
    moe = gates[:, 0:1] * buf_ref[0]
    for j in range(1, TOP_K_EXPERTS):
        moe = moe + gates[:, j:j + 1] * buf_ref[j]
    o_ref[...] = _layer_norm(alpha * x1_ref[...] + moe, g_ref[...], b_ref[...])


def _combine(dest_flat, ys, gates, x1, g2, b2, alpha):
    rows, d = x1.shape
    tc = DISPATCH_TOKENS
    return pl.pallas_call(
        functools.partial(_combine_kernel, tc=tc, alpha=alpha),
        grid=(rows // tc,),
        in_specs=[pl.BlockSpec((tc * TOP_K_EXPERTS,), lambda i: (i,), memory_space=pltpu.SMEM),
                  pl.BlockSpec(memory_space=pl.ANY),
                  pl.BlockSpec((tc, LANES), lambda i: (i, 0)),
                  pl.BlockSpec((tc, d), lambda i: (i, 0)),
                  pl.BlockSpec((1, d), lambda i: (0, 0)),
                  pl.BlockSpec((1, d), lambda i: (0, 0))],
        out_specs=pl.BlockSpec((tc, d), lambda i: (i, 0)),
        out_shape=jax.ShapeDtypeStruct((rows, d), F32),
        scratch_shapes=[pltpu.VMEM((TOP_K_EXPERTS, tc, d), F32), pltpu.SemaphoreType.DMA(())],
        compiler_params=_cparams(("arbitrary",)),
        name="moe_combine",
    )(dest_flat, ys, gates, x1, g2, b2)


def _prep_w_in(w, b_f):
    d = w.shape[0]
    sizes = (W_FOX, W_FOX, W_FOX, H_FOX, W_DSA, W_DSA, W_DSA, H_IDX * D_IDX, D_IDX, H_IDX)
    parts, acc = [], 0
    for s in sizes:
        parts.append(w[:, acc:acc + s])
        acc += s
    fq, fk, fv, fl, dq, dk, dv, iq, ik, iw = parts
    small = jnp.concatenate([ik, fl, iw, jnp.zeros((d, LANES - D_IDX - H_FOX - H_IDX), w.dtype)], axis=1)
    wp = jnp.concatenate([fq, fk, fv, dq, dk, dv, iq, small], axis=1).astype(BF16)
    bf_row = jnp.zeros((1, LANES), F32).at[0, SM_LF:SM_LF + H_FOX].set(b_f.astype(F32))
    return wp, bf_row


def _rope_tables(pos):
    half = HEAD_DIM // 2
    inv = ROPE_THETA ** (-jnp.arange(half, dtype=F32) / half)
    ang = pos.astype(F32)[:, None] * inv[None, :]
    cos, sin = jnp.cos(ang), jnp.sin(ang)
    cos_t = jnp.concatenate([cos, cos, cos, cos], axis=1)
    sin_t = jnp.concatenate([-sin, sin, -sin, sin], axis=1)
    return cos_t, sin_t


def _pad_rows(a, n):
    return jnp.pad(a, ((0, 0), (0, n - a.shape[1]), (0, 0)))


def kernel(x_prompt, x_sample, cache_fox_k, cache_fox_v, cache_fox_logf, cache_dsa_k, cache_dsa_v,
           cache_dsa_kidx, page_table, w_in, b_f, w_o, ln1_g, ln1_b, w_router, b_router, w_gate, b_gate,
           w_up, b_up, w_down, b_down, ln2_g, ln2_b):
    batch, seq, d = x_prompt.shape
    dec_b, n_q, _ = x_sample.shape
    depth = w_in.shape[0]
    n_pool = cache_fox_k.shape[1]
    n_pages = page_table.shape[1]
    past = n_pages * PAGE_SIZE
    alpha = (2.0 * depth) ** 0.25
    rows_p, rows_s = batch * seq, dec_b * n_q

    cos_p, sin_p = _rope_tables(jnp.arange(seq))
    cos_s, sin_s = _rope_tables(past + (jnp.arange(rows_s) % n_q))

    xp = x_prompt.reshape(rows_p, d)
    xs_ = x_sample.reshape(rows_s, d)
    outs_p = [[] for _ in range(6)]
    outs_s = [[] for _ in range(6)]
    for l in range(depth):
        wp, bf_row = _prep_w_in(w_in[l], b_f[l])
        wo = w_o[l].astype(BF16)
        wr = jnp.pad(w_router[l], ((0, 0), (0, LANES - N_EXPERTS))).astype(BF16)
        br = jnp.full((1, LANES), NEG_INF, F32).at[0, :N_EXPERTS].set(b_router[l].astype(F32))
        g1, b1 = ln1_g[l].reshape(1, d), ln1_b[l].reshape(1, d)
        g2, b2 = ln2_g[l].reshape(1, d), ln2_b[l].reshape(1, d)

        pp = _in_proj(xp, wp, bf_row, cos_p, sin_p, batch, seq, tm=256)
        o_fox = _fox_prompt(pp["fqt"], pp["fkb"], pp["fvtc"], pp["cb"], batch, seq)
        o_dsa = _dsa_prompt(pp["iqt"], pp["iwt"], pp["ikb"], pp["dqt"], pp["dkb"], pp["dvtc"], batch, seq,
                            tq=256)

        ps = _in_proj(xs_, wp, bf_row, cos_s, sin_s, 1, rows_s, tm=rows_s)
        ps = {k: v[0] for k, v in ps.items() if v.ndim == 3}
        pool_fk = cache_fox_k[l].transpose(0, 2, 3, 1)
        pool_fv = cache_fox_v[l].transpose(0, 2, 3, 1)
        pool_lft = jnp.swapaxes(cache_fox_logf[l], 1, 2)
        pool_dk = cache_dsa_k[l].transpose(0, 2, 3, 1)
        pool_dv = cache_dsa_v[l].transpose(0, 2, 3, 1)
        pool_ki = jnp.swapaxes(cache_dsa_kidx[l], 1, 2)
        per_seq = lambda a: a.reshape(a.shape[0], dec_b, n_q).transpose(1, 0, 2)
        new_t = lambda a: jnp.pad(per_seq(a), ((0, 0), (0, 0), (0, PAGE_SIZE - n_q)))
        rows_q = lambda a: a.T.reshape(dec_b, n_q, a.shape[0])
        so_fox = _sample_fox(page_table, rows_q(ps["fqt"]), new_t(ps["fkt"]), new_t(ps["fvt"]),
                             new_t(ps["lft"]), pool_fk, pool_fv, pool_lft)
        iqx = ps["iqt"].reshape(H_IDX, D_IDX, dec_b, n_q).transpose(2, 0, 3, 1)
        iqx = jnp.pad(iqx, ((0, 0), (0, 0), (0, H_FOX - n_q), (0, 0))).reshape(dec_b, H_IDX * H_FOX, D_IDX)
        iwx = jnp.pad(per_seq(ps["iwt"]), ((0, 0), (0, 0), (0, H_FOX - n_q)))
        iwx = iwx.reshape(dec_b, H_IDX * H_FOX, 1)
        sel_bias, sel_bias_new = _sample_index(page_table, iqx, iwx, new_t(ps["ikt"]), pool_ki, n_q)
        so_dsa = _sample_dsa(page_table, rows_q(ps["dqt"]), new_t(ps["dkt"]), new_t(ps["dvt"]), sel_bias,
                             sel_bias_new, pool_dk, pool_dv)

        cin = jnp.zeros((1, LANES), F32)
        x1_p, e_p, gate_p, rank_p, cnt_p = _tail(o_fox, o_dsa, xp, wo, g1, b1, wr, br, cin, alpha, tm=256)
        x1_s, e_s, gate_s, rank_s, cnt = _tail(so_fox.reshape(rows_s, W_FOX), so_dsa.reshape(rows_s, W_DSA),
                                               xs_, wo, g1, b1, wr, br, cnt_p, alpha, tm=rows_s)

        bs = EXPERT_BLOCK
        counts = cnt[0, :N_EXPERTS].astype(I32)
        padded = (counts + bs - 1) // bs * bs
        pad_end = jnp.cumsum(padded)
        pad_start = pad_end - padded
        n_assign = (rows_p + rows_s) * TOP_K_EXPERTS
        n_blocks = -(-(n_assign + N_EXPERTS * (bs - 1)) // bs)
        blk = jnp.arange(n_blocks, dtype=I32)
        n_used = (pad_end[-1] // bs).astype(I32)
        block_e = jnp.sum((pad_end[None, :] <= (blk * bs)[:, None]).astype(I32), axis=1)
        block_e = jnp.minimum(block_e, N_EXPERTS - 1)
        block_e = jnp.where(blk < n_used, block_e, block_e[jnp.maximum(n_used - 1, 0)])
        is_first = jnp.concatenate([jnp.ones((1,), I32), (block_e[1:] != block_e[:-1]).astype(I32)])
        dest_p = (pad_start[e_p[:, :TOP_K_EXPERTS]] + rank_p[:, :TOP_K_EXPERTS]).reshape(-1)
        dest_s = (pad_start[e_s[:, :TOP_K_EXPERTS]] + rank_s[:, :TOP_K_EXPERTS]).reshape(-1)

        xs_sorted = jnp.zeros((n_blocks * bs, d), F32)
        xs_sorted = _dispatch(dest_p, x1_p, xs_sorted)
        xs_sorted = _dispatch(dest_s, x1_s, xs_sorted)
        ys = _expert_ffn(block_e, is_first, n_used.reshape(1), xs_sorted, w_gate[l], w_up[l], w_down[l],
                         b_gate[l].reshape(N_EXPERTS, 1, -1), b_up[l].reshape(N_EXPERTS, 1, -1),
                         b_down[l].reshape(N_EXPERTS, 1, -1))
        xp_new = _combine(dest_p, ys, gate_p, x1_p, g2, b2, alpha)
        xs_new = _combine(dest_s, ys, gate_s, x1_s, g2, b2, alpha)

        heads_p = lambda a: a.reshape(batch, -1, HEAD_DIM, seq).transpose(0, 3, 1, 2)
        heads_s = lambda a: a.reshape(-1, HEAD_DIM, dec_b, n_q).transpose(2, 3, 0, 1)
        for acc, a in zip(outs_p, (heads_p(pp["fkt"]), heads_p(pp["fvt"]), pp["lft"].transpose(0, 2, 1),
                                   heads_p(pp["dkt"]), heads_p(pp["dvt"]), pp["ikt"].transpose(0, 2, 1))):
            acc.append(a)
        for acc, a in zip(outs_s, (heads_s(ps["fkt"]), heads_s(ps["fvt"]), per_seq(ps["lft"]).transpose(0, 2, 1),
                                   heads_s(ps["dkt"]), heads_s(ps["dvt"]), per_seq(ps["ikt"]).transpose(0, 2, 1))):
            acc.append(a)
        xp, xs_ = xp_new, xs_new

    stack = lambda lst: [jnp.stack(r, axis=0) for r in lst]
    return (xp.reshape(batch, seq, d), xs_.reshape(dec_b, n_q, d), *stack(outs_p), *stack(outs_s))
```

```python
import functools

import jax
import jax.numpy as jnp
from jax import lax
from jax.experimental import pallas as pl
from jax.experimental.pallas import tpu as pltpu

F32, BF16, I32 = jnp.float32, jnp.bfloat16, jnp.int32
NEG_INF = float("-inf")

HEAD_DIM = 64
H_FOX = 8
H_DSA = 8
W_FOX = H_FOX * HEAD_DIM
W_DSA = H_DSA * HEAD_DIM
H_IDX = 16
D_IDX = 64
TOPK_MAX = 256
PAGE_SIZE = 128
ROPE_THETA = 10000.0
ATTN_SCALE = HEAD_DIM ** -0.5
IDX_SCALE = D_IDX ** -0.5
IDX_W_SCALE = H_IDX ** -0.5
N_EXPERTS = 32
TOP_K_EXPERTS = 4
SWIGLU_LIMIT = 7.0
SWIGLU_ALPHA = 1.702
LN_EPS = 1e-5

LANES = 128
KEY_NEG_INF = -2139095041
INT_MIN = -2 ** 31
VMEM_LIMIT = 56 * 1024 * 1024

NT_DIMS = (((1,), (1,)), ((), ()))


def _cparams(sem):
    return pltpu.CompilerParams(dimension_semantics=sem, vmem_limit_bytes=VMEM_LIMIT)


def _split3(x):
    hi = x.astype(BF16)
    r1 = x - hi.astype(F32)
    mid = r1.astype(BF16)
    lo = (r1 - mid.astype(F32)).astype(BF16)
    return hi, mid, lo


def _lane_cumsum(x, upper):
    hi, mid, lo = _split3(x)
    return (jnp.dot(hi, upper, preferred_element_type=F32)
            + jnp.dot(mid, upper, preferred_element_type=F32)
            + jnp.dot(lo, upper, preferred_element_type=F32))


def _upper_tri(n):
    r = lax.broadcasted_iota(I32, (n, n), 0)
    c = lax.broadcasted_iota(I32, (n, n), 1)
    return (r <= c).astype(BF16)


def _sort_key(x):
    bits = pltpu.bitcast(x, I32)
    return bits ^ ((bits >> 31) & 0x7FFFFFFF)


C_FQ, C_FK, C_FV, C_DQ, C_DK, C_DV, C_IQ, C_SM = 0, 512, 1024, 1536, 2048, 2560, 3072, 4096
N_PROJ = 4224
SM_IK, SM_LF, SM_IW = 0, 64, 72


def _in_proj_kernel(x_ref, w_ref, cos_ref, sin_ref, bf_ref,
                    fqt_ref, fkb_ref, fkt_ref, fvt_ref, fvtc_ref, cb_ref, lft_ref,
                    dqt_ref, dkb_ref, dkt_ref, dvt_ref, dvtc_ref, iqt_ref, ikb_ref, ikt_ref, iwt_ref,
                    carry_ref, *, tiles_per_seq):
    @pl.when(pl.program_id(0) % tiles_per_seq == 0)
    def _():
        carry_ref[...] = jnp.zeros(carry_ref.shape, F32)

    xb = x_ref[...].astype(BF16)
    tm = xb.shape[0]
    cos = cos_ref[...]
    sin = sin_ref[...]
    lane = lax.broadcasted_iota(I32, (tm, LANES), 1)
    first_half = (lane & (HEAD_DIM - 1)) < HEAD_DIM // 2

    def mm(c0, n):
        return jnp.dot(xb, w_ref[:, c0:c0 + n], preferred_element_type=F32)

    def rope(y):
        outs = []
        for c in range(0, y.shape[1], LANES):
            yc = y[:, c:c + LANES]
            partner = jnp.where(first_half, pltpu.roll(yc, LANES - HEAD_DIM // 2, 1),
                                pltpu.roll(yc, HEAD_DIM // 2, 1))
            outs.append(yc * cos + partner * sin)
        return outs[0] if len(outs) == 1 else jnp.concatenate(outs, axis=1)

    y = mm(C_FQ, W_FOX)
    fqt_ref[...] = (y * ATTN_SCALE).T.astype(BF16)
    y = mm(C_FK, W_FOX)
    fkb_ref[...] = y.astype(BF16)
    fkt_ref[...] = y.T
    yt = mm(C_FV, W_FOX).T
    fvt_ref[...] = yt
    fvtc_ref[...] = yt.astype(BF16)
    y = rope(mm(C_DQ, W_DSA))
    dqt_ref[...] = (y * ATTN_SCALE).T.astype(BF16)
    y = rope(mm(C_DK, W_DSA))
    dkb_ref[...] = y.astype(BF16)
    dkt_ref[...] = y.T
    yt = mm(C_DV, W_DSA).T
    dvt_ref[...] = yt
    dvtc_ref[...] = yt.astype(BF16)
    y = rope(mm(C_IQ, H_IDX * D_IDX))
    iqt_ref[...] = (y * IDX_SCALE).T.astype(BF16)
    z = mm(C_SM, LANES)
    r = rope(z)
    ikb_ref[...] = r[:, SM_IK:SM_IK + D_IDX].astype(BF16)
    zb = z + bf_ref[...]
    lsg = -(jnp.maximum(-zb, 0.0) + jnp.log1p(jnp.exp(-jnp.abs(zb))))
    small_t = jnp.where(lane < SM_LF, r, jnp.where(lane < SM_IW, lsg, z * IDX_W_SCALE)).T
    ikt_ref[...] = small_t[SM_IK:SM_IK + D_IDX, :]
    lft_ref[...] = small_t[SM_LF:SM_LF + H_FOX, :]
    iwt_ref[...] = small_t[SM_IW:SM_IW + H_IDX, :]

    hi, mid, lo = _split3(jnp.where((lane >= SM_LF) & (lane < SM_IW), lsg, 0.0))
    rr = lax.broadcasted_iota(I32, (tm, tm), 0)
    cc = lax.broadcasted_iota(I32, (tm, tm), 1)
    lower = (cc <= rr).astype(BF16)
    parts = jnp.dot(lower, jnp.concatenate([hi, mid, lo], axis=1), preferred_element_type=F32)
    cs = parts[:, :LANES] + parts[:, LANES:2 * LANES] + parts[:, 2 * LANES:] + carry_ref[...]
    carry_ref[...] = cs[tm - 1:tm, :]
    for h in range(H_FOX):
        cb_ref[h] = jnp.broadcast_to(cs[:, SM_LF + h:SM_LF + h + 1], (tm, LANES))


FOX_KV_CHUNK = 256
DSA_KV_CHUNK = 512


def _in_proj(x2d, wp, bf_row, cos_t, sin_t, n_seq, seq, tm):
    rows, d = x2d.shape
    tps = seq // tm
    ckf, ckd = min(FOX_KV_CHUNK, seq), min(DSA_KV_CHUNK, seq)
    n_tab = cos_t.shape[0] // tm
    t_spec = lambda w: pl.BlockSpec((None, w, tm), lambda i: (i // tps, 0, i % tps))
    n_spec = lambda w: pl.BlockSpec((tm, w), lambda i: (i, 0))
    c_spec = lambda w, ck: pl.BlockSpec((None, None, w, tm),
                                        lambda i: (i // tps, (i % tps) * tm // ck, 0, (i % tps) % (ck // tm)))
    t_shape = lambda w, dt: jax.ShapeDtypeStruct((n_seq, w, seq), dt)
    n_shape = lambda w, dt: jax.ShapeDtypeStruct((rows, w), dt)
    c_shape = lambda w, ck: jax.ShapeDtypeStruct((n_seq, seq // ck, w, ck), BF16)
    wi = H_IDX * D_IDX
    outs = [("fqt", t_spec(W_FOX), t_shape(W_FOX, BF16)), ("fkb", n_spec(W_FOX), n_shape(W_FOX, BF16)),
            ("fkt", t_spec(W_FOX), t_shape(W_FOX, F32)), ("fvt", t_spec(W_FOX), t_shape(W_FOX, F32)),
            ("fvtc", c_spec(W_FOX, ckf), c_shape(W_FOX, ckf)),
            ("cb", pl.BlockSpec((H_FOX, tm, LANES), lambda i: (0, i, 0)),
             jax.ShapeDtypeStruct((H_FOX, rows, LANES), F32)),
            ("lft", t_spec(H_FOX), t_shape(H_FOX, F32)),
            ("dqt", t_spec(W_DSA), t_shape(W_DSA, BF16)), ("dkb", n_spec(W_DSA), n_shape(W_DSA, BF16)),
            ("dkt", t_spec(W_DSA), t_shape(W_DSA, F32)), ("dvt", t_spec(W_DSA), t_shape(W_DSA, F32)),
            ("dvtc", c_spec(W_DSA, ckd), c_shape(W_DSA, ckd)),
            ("iqt", t_spec(wi), t_shape(wi, BF16)), ("ikb", n_spec(D_IDX), n_shape(D_IDX, BF16)),
            ("ikt", t_spec(D_IDX), t_shape(D_IDX, F32)), ("iwt", t_spec(H_IDX), t_shape(H_IDX, F32))]
    res = pl.pallas_call(
        functools.partial(_in_proj_kernel, tiles_per_seq=tps),
        grid=(rows // tm,),
        in_specs=[pl.BlockSpec((tm, d), lambda i: (i, 0)),
                  pl.BlockSpec((d, N_PROJ), lambda i: (0, 0)),
                  pl.BlockSpec((tm, LANES), lambda i: (i % n_tab, 0)),
                  pl.BlockSpec((tm, LANES), lambda i: (i % n_tab, 0)),
                  pl.BlockSpec((1, LANES), lambda i: (0, 0))],
        out_specs=[o[1] for o in outs],
        out_shape=[o[2] for o in outs],
        scratch_shapes=[pltpu.VMEM((1, LANES), F32)],
        compiler_params=_cparams(("arbitrary",)),
        name="in_proj",
    )(x2d, wp, cos_t, sin_t, bf_row)
    return {o[0]: r for o, r in zip(outs, res)}


SUBLANES = 8


def _over_sublanes(x8, reduce_fn):
    return jnp.broadcast_to(reduce_fn(x8, axis=0, keepdims=True), x8.shape)


def _by_sublane_tile(x):
    return x.reshape(x.shape[0] // SUBLANES, SUBLANES, x.shape[1])


def _flash_step(n_heads, score_fn, vt_fn, st):
    m_ref, l_ref, acc_ref, _, s_ref, p_ref, alpha_ref = st
    for h in range(n_heads):
        s_ref[h] = score_fn(h)
    for h in range(n_heads):
        s3 = _by_sublane_tile(s_ref[h])
        m_old = m_ref[h]
        m_new = jnp.maximum(m_old, _over_sublanes(jnp.max(s3, axis=0), jnp.max))
        alpha = jnp.exp(m_old - m_new)
        p3 = jnp.exp(s3 - m_new[None])
        l_ref[h] = alpha * l_ref[h] + _over_sublanes(jnp.sum(p3, axis=0), jnp.sum)
        p_ref[h] = p3.reshape(s_ref.shape[1:]).astype(BF16)
        alpha_ref[h] = alpha
        m_ref[h] = m_new
    for h in range(n_heads):
        pv = jnp.dot(vt_fn(h), p_ref[h], preferred_element_type=F32)
        acc_ref[h] = (_by_sublane_tile(acc_ref[h]) * alpha_ref[h][None]).reshape(acc_ref.shape[1:]) + pv


def _flash_init(m_ref, l_ref, acc_ref):
    m_ref[...] = jnp.full(m_ref.shape, -1e30, F32)
    l_ref[...] = jnp.zeros(l_ref.shape, F32)
    acc_ref[...] = jnp.zeros(acc_ref.shape, F32)


def _flash_finish(o_ref, l_ref, acc_ref, n_heads):
    tq = acc_ref.shape[2]
    for pair in range(n_heads // 2):
        parts = [(_by_sublane_tile(acc_ref[h]) / l_ref[h][None]).reshape(HEAD_DIM, tq)
                 for h in (2 * pair, 2 * pair + 1)]
        o_ref[:, pair * LANES:(pair + 1) * LANES] = jnp.concatenate(parts, axis=0).T.astype(BF16)


def _flash_scratch(n_heads, tk, tq):
    return [pltpu.VMEM((n_heads, SUBLANES, tq), F32), pltpu.VMEM((n_heads, SUBLANES, tq), F32),
            pltpu.VMEM((n_heads, HEAD_DIM, tq), F32), pltpu.VMEM((n_heads, LANES, tq), BF16),
            pltpu.VMEM((n_heads, tk, tq), F32), pltpu.VMEM((n_heads, tk, tq), BF16),
            pltpu.VMEM((n_heads, SUBLANES, tq), F32)]


def _stage_pair_masked_q(qt_ref, qpm_ref, n_heads):
    for h in range(n_heads):
        pair = h // 2
        rows = qt_ref[pair * LANES:(pair + 1) * LANES, :].astype(F32)
        r = lax.broadcasted_iota(I32, rows.shape, 0)
        qpm_ref[h] = jnp.where(r // HEAD_DIM == h % 2, rows, 0.0).astype(BF16)


def _fox_kernel(qt_ref, k_ref, vt_ref, cb_ref, o_ref, *st, tq):
    m_ref, l_ref, acc_ref, qpm_ref = st[:4]
    i = pl.program_id(1)
    _stage_pair_masked_q(qt_ref, qpm_ref, H_FOX)
    _flash_init(m_ref, l_ref, acc_ref)
    key_row = lax.broadcasted_iota(I32, (tq, tq), 0)
    query_col = lax.broadcasted_iota(I32, (tq, tq), 1)
    visible = key_row <= query_col

    def kv_step(j, masked):
        start = pl.multiple_of(j * tq, tq)

        def scores(h):
            pair = h // 2
            s = jnp.dot(k_ref[pl.ds(start, tq), pair * LANES:(pair + 1) * LANES], qpm_ref[h],
                        preferred_element_type=F32)
            s = s - jnp.concatenate([cb_ref[h, pl.ds(start, tq), :]] * (tq // LANES), axis=1)
            return jnp.where(visible, s, NEG_INF) if masked else s

        _flash_step(H_FOX, scores, lambda h: vt_ref[j, h * HEAD_DIM:(h + 1) * HEAD_DIM, :], st)

    def body(j, c):
        kv_step(j, False)
        return c

    lax.fori_loop(0, i, body, 0)
    kv_step(i, True)
    _flash_finish(o_ref, l_ref, acc_ref, H_FOX)


def _fox_prompt(qt, kb, vtc, cb, batch, seq):
    tq = FOX_KV_CHUNK
    nq = seq // tq
    return pl.pallas_call(
        functools.partial(_fox_kernel, tq=tq),
        grid=(batch, nq),
        in_specs=[pl.BlockSpec((None, W_FOX, tq), lambda b, i: (b, 0, i)),
                  pl.BlockSpec((seq, W_FOX), lambda b, i: (b, 0)),
                  pl.BlockSpec((None, nq, W_FOX, tq), lambda b, i: (b, 0, 0, 0)),
                  pl.BlockSpec((H_FOX, seq, LANES), lambda b, i: (0, b, 0))],
        out_specs=pl.BlockSpec((tq, W_FOX), lambda b, i: (b * nq + i, 0)),
        out_shape=jax.ShapeDtypeStruct((batch * seq, W_FOX), BF16),
        scratch_shapes=_flash_scratch(H_FOX, tq, tq),
        compiler_params=_cparams(("arbitrary", "arbitrary")),
        name="fox_prompt",
    )(qt, kb, vtc, cb)


def _dsa_kernel(iqt_ref, iwt_ref, ik_ref, qt_ref, k_ref, vt_ref, o_ref, key_ref, bias_ref, *st,
                tq, ck, topk):
    m_ref, l_ref, acc_ref, qpm_ref = st[:4]
    i = pl.program_id(1)
    q0 = i * tq
    n_chunks = (q0 + tq + ck - 1) // ck
    key_row = lax.broadcasted_iota(I32, (ck, tq), 0)
    query_pos = q0 + lax.broadcasted_iota(I32, (ck, tq), 1)
    iw8 = [jnp.broadcast_to(iwt_ref[h:h + 1, :], (SUBLANES, tq)) for h in range(H_IDX)]

    def score_chunk(c, _):
        start = pl.multiple_of(c * ck, ck)
        ikc = ik_ref[pl.ds(start, ck), :]
        sc = jnp.zeros((ck // SUBLANES, SUBLANES, tq), F32)
        for h in range(H_IDX):
            rel = jnp.dot(ikc, iqt_ref[h * D_IDX:(h + 1) * D_IDX, :], preferred_element_type=F32)
            sc = sc + _by_sublane_tile(jnp.maximum(rel, 0.0)) * iw8[h][None]
        sc = jnp.where(key_row + start <= query_pos, sc.reshape(ck, tq), NEG_INF)
        key_ref[c] = _sort_key(sc)
        return 0

    lax.fori_loop(0, n_chunks, score_chunk, 0)

    def count(pred):
        def body(c, acc):
            return acc + jnp.sum(pred(_by_sublane_tile(key_ref[c])).astype(F32), axis=0)
        return _over_sublanes(lax.fori_loop(0, n_chunks, body, jnp.zeros((SUBLANES, tq), F32)), jnp.sum)

    def bit_step(state):
        bi, thr, n_ge = state
        cand = thr + lax.shift_left(jnp.int32(1), 31 - bi)
        n_cand = count(lambda k: k >= cand[None])
        take = n_cand >= topk
        return bi + 1, jnp.where(take, cand, thr), jnp.where(take, n_cand, n_ge)

    def searching(state):
        bi, _, n_ge = state
        return (bi < 32) & (jnp.max(n_ge) > topk)

    everything = jnp.full((SUBLANES, tq), INT_MIN, I32)
    _, thr, _ = lax.while_loop(searching, bit_step,
                               (jnp.int32(0), everything, count(lambda k: k >= everything[None])))
    need = topk - count(lambda k: k > thr[None])
    ties = count(lambda k: k == thr[None])
    excess = jnp.max(jnp.where((ties > need) & (thr > KEY_NEG_INF), 1.0, 0.0)) > 0.0

    @pl.when(jnp.logical_not(excess))
    def _():
        def fill(c, _):
            k = _by_sublane_tile(key_ref[c])
            bias_ref[c] = jnp.where((k >= thr[None]) & (k > KEY_NEG_INF), 0.0, NEG_INF).reshape(ck, tq)
            return 0
        lax.fori_loop(0, n_chunks, fill, 0)

    @pl.when(excess)
    def _():
        rr = lax.broadcasted_iota(I32, (ck, ck), 0)
        cc = lax.broadcasted_iota(I32, (ck, ck), 1)
        strict_lower = (cc < rr).astype(BF16)

        def fill(c, before):
            k = _by_sublane_tile(key_ref[c])
            tie = k == thr[None]
            tie_f = tie.astype(F32)
            earlier = jnp.dot(strict_lower, tie_f.reshape(ck, tq).astype(BF16), preferred_element_type=F32)
            earlier = _by_sublane_tile(earlier) + before[None]
            sel = (k > thr[None]) | (tie & (earlier < need[None]))
            bias_ref[c] = jnp.where(sel & (k > KEY_NEG_INF), 0.0, NEG_INF).reshape(ck, tq)
            return before + _over_sublanes(jnp.sum(tie_f, axis=0), jnp.sum)
        lax.fori_loop(0, n_chunks, fill, jnp.zeros((SUBLANES, tq), F32))

    _stage_pair_masked_q(qt_ref, qpm_ref, H_DSA)
    _flash_init(m_ref, l_ref, acc_ref)

    def kv_step(c, carry):
        start = pl.multiple_of(c * ck, ck)
        def scores(h):
            pair = h // 2
            return jnp.dot(k_ref[pl.ds(start, ck), pair * LANES:(pair + 1) * LANES], qpm_ref[h],
                           preferred_element_type=F32) + bias_ref[c]

        _flash_step(H_DSA, scores, lambda h: vt_ref[c, h * HEAD_DIM:(h + 1) * HEAD_DIM, :], st)
        return carry

    lax.fori_loop(0, n_chunks, kv_step, 0)
    _flash_finish(o_ref, l_ref, acc_ref, H_DSA)


def _dsa_prompt(iqt, iwt, ikb, dqt, dkb, dvtc, batch, seq, tq):
    ck = dvtc.shape[3]
    nq = seq // tq
    topk = min(TOPK_MAX, seq // 4)
    return pl.pallas_call(
        functools.partial(_dsa_kernel, tq=tq, ck=ck, topk=topk),
        grid=(batch, nq),
        in_specs=[pl.BlockSpec((None, H_IDX * D_IDX, tq), lambda b, i: (b, 0, i)),
                  pl.BlockSpec((None, H_IDX, tq), lambda b, i: (b, 0, i)),
                  pl.BlockSpec((seq, D_IDX), lambda b, i: (b, 0)),
                  pl.BlockSpec((None, W_DSA, tq), lambda b, i: (b, 0, i)),
                  pl.BlockSpec((seq, W_DSA), lambda b, i: (b, 0)),
                  pl.BlockSpec((None, seq // ck, W_DSA, ck), lambda b, i: (b, 0, 0, 0))],
        out_specs=pl.BlockSpec((tq, W_DSA), lambda b, i: (b * nq + i, 0)),
        out_shape=jax.ShapeDtypeStruct((batch * seq, W_DSA), BF16),
        scratch_shapes=[pltpu.VMEM((seq // ck, ck, tq), I32), pltpu.VMEM((seq // ck, ck, tq), F32)]
        + _flash_scratch(H_DSA, ck, tq),
        compiler_params=_cparams(("arbitrary", "arbitrary")),
        name="dsa_prompt",
    )(iqt, iwt, ikb, dqt, dkb, dvtc)


KV_PAGES_PER_STEP = 16
IDX_PAGES_PER_STEP = 32


def _head_mask(rows, width):
    r = lax.broadcasted_iota(I32, (rows, width), 0)
    c = lax.broadcasted_iota(I32, (rows, width), 1)
    return (c // HEAD_DIM) == (r % H_FOX)


def _block_diag_q(q, n_q):
    hm = _head_mask(H_FOX, q.shape[1])
    qf = q.astype(F32)
    parts = [jnp.where(hm, jnp.broadcast_to(qf[t:t + 1, :], hm.shape), 0.0) for t in range(n_q)]
    return jnp.concatenate(parts, axis=0).astype(BF16)


def _rows_to_groups(x, n_q):
    return jnp.concatenate([jnp.broadcast_to(x[t:t + 1, :], (H_FOX, x.shape[1])) for t in range(n_q)], axis=0)


def _softmax_page(s, vt_bf16, m_ref, l_ref, acc_ref):
    m = m_ref[...]
    m_new = jnp.maximum(m, jnp.max(s, axis=1, keepdims=True))
    alpha = jnp.exp(m - m_new)
    p = jnp.exp(s - m_new)
    l_ref[...] = alpha * l_ref[...] + jnp.sum(p, axis=1, keepdims=True)
    acc_ref[...] = alpha * acc_ref[...] + lax.dot_general(p.astype(BF16), vt_bf16, NT_DIMS,
                                                          preferred_element_type=F32)
    m_ref[...] = m_new


def _stack_pages(refs):
    return jnp.concatenate([r[...].reshape(-1, PAGE_SIZE) for r in refs], axis=1).astype(BF16)


def _finish_decode(o_ref, l_ref, acc_ref, n_q):
    o = acc_ref[...] / l_ref[...]
    o = jnp.where(_head_mask(o.shape[0], o.shape[1]), o, 0.0)
    o_ref[...] = jnp.sum(o.reshape(n_q, H_FOX, o.shape[1]), axis=1).astype(BF16)


def _sample_fox_kernel(pt_ref, q_ref, kn_ref, vn_ref, lfn_ref, *rest, n_q, pps):
    k_refs, v_refs, lf_refs = rest[:pps], rest[pps:2 * pps], rest[2 * pps:3 * pps]
    o_ref, m_ref, l_ref, acc_ref, cc_ref, qbd_ref = rest[3 * pps:]
    step = pl.program_id(1)
    rows = n_q * H_FOX

    @pl.when(step == 0)
    def _():
        m_ref[...] = jnp.full(m_ref.shape, -1e30, F32)
        l_ref[...] = jnp.zeros(l_ref.shape, F32)
        acc_ref[...] = jnp.zeros(acc_ref.shape, F32)
        cc_ref[...] = jnp.zeros(cc_ref.shape, F32)
        qbd_ref[...] = _block_diag_q(q_ref[...], n_q)

    upper = _upper_tri(PAGE_SIZE)

    within = _lane_cumsum(jnp.concatenate([r[...] for r in lf_refs], axis=0), upper)
    off = cc_ref[...]
    parts = []
    for p in range(pps):
        cp = within[p * H_FOX:(p + 1) * H_FOX] + off
        parts.append(cp)
        off = cp[:, PAGE_SIZE - 1:PAGE_SIZE]
    cc_ref[...] = off
    cbias = jnp.concatenate(parts, axis=1)
    s = jnp.dot(qbd_ref[...], _stack_pages(k_refs), preferred_element_type=F32)
    s = s - jnp.concatenate([cbias] * n_q, axis=0)
    _softmax_page(s, _stack_pages(v_refs), m_ref, l_ref, acc_ref)

    @pl.when(step == pl.num_programs(1) - 1)
    def _():
        r = lax.broadcasted_iota(I32, (rows, PAGE_SIZE), 0)
        c = lax.broadcasted_iota(I32, (rows, PAGE_SIZE), 1)
        cn = _lane_cumsum(lfn_ref[...], upper) + cc_ref[...]
        sn = jnp.dot(qbd_ref[...], kn_ref[...].astype(BF16), preferred_element_type=F32)
        sn = sn - jnp.concatenate([cn] * n_q, axis=0)
        sn = jnp.where(c <= r // H_FOX, sn, NEG_INF)
        _softmax_page(sn, vn_ref[...].astype(BF16), m_ref, l_ref, acc_ref)
        _finish_decode(o_ref, l_ref, acc_ref, n_q)


def _page_spec(block, p, pps):
    nd = len(block)
    return pl.BlockSpec((None,) + block, lambda b, s, pt: (pt[b, s * pps + p],) + (0,) * nd)


def _sample_fox(page_table, q, k_new, v_new, lft_new, pool_k, pool_v, pool_lft):
    dec_b, n_q, w = q.shape
    n_pages = page_table.shape[1]
    pps = min(KV_PAGES_PER_STEP, n_pages)
    rows = n_q * H_FOX
    seq_spec = lambda blk: pl.BlockSpec((None,) + blk, lambda b, s, pt: (b,) + (0,) * len(blk))
    kv_page = (H_FOX, HEAD_DIM, PAGE_SIZE)
    in_specs = ([seq_spec((n_q, w)), seq_spec((w, PAGE_SIZE)), seq_spec((w, PAGE_SIZE)),
                 seq_spec((H_FOX, PAGE_SIZE))]
                + [_page_spec(kv_page, p, pps) for p in range(pps)]
                + [_page_spec(kv_page, p, pps) for p in range(pps)]
                + [_page_spec((H_FOX, PAGE_SIZE), p, pps) for p in range(pps)])
    return pl.pallas_call(
        functools.partial(_sample_fox_kernel, n_q=n_q, pps=pps),
        grid_spec=pltpu.PrefetchScalarGridSpec(
            num_scalar_prefetch=1,
            grid=(dec_b, n_pages // pps),
            in_specs=in_specs,
            out_specs=pl.BlockSpec((None, n_q, w), lambda b, s, pt: (b, 0, 0)),
            scratch_shapes=[pltpu.VMEM((rows, 1), F32), pltpu.VMEM((rows, 1), F32),
                            pltpu.VMEM((rows, w), F32), pltpu.VMEM((H_FOX, 1), F32),
                            pltpu.VMEM((rows, w), BF16)]),
        out_shape=jax.ShapeDtypeStruct((dec_b, n_q, w), BF16),
        compiler_params=_cparams(("arbitrary", "arbitrary")),
        name="sample_fox",
    )(page_table, q, k_new, v_new, lft_new, *([pool_k] * pps), *([pool_v] * pps), *([pool_lft] * pps))


def _sample_index_kernel(pt_ref, iq_ref, iw_ref, kn_ref, *rest, pps, n_q, topk):
    k_refs = rest[:pps]
    bias_ref, biasn_ref, key_ref, keyn_ref = rest[pps:]
    step = pl.program_id(1)
    n_steps = key_ref.shape[0]
    iq = iq_ref[...]
    iw = iw_ref[...]

    def scores(kt):
        rel = jnp.dot(iq, kt, preferred_element_type=F32)
        w = iw * jnp.maximum(rel, 0.0)
        by_parity = jnp.sum(w.reshape(H_IDX // 2, SUBLANES, kt.shape[1]), axis=0)
        return by_parity + pltpu.roll(by_parity, SUBLANES // 2, 0)

    key_ref[step] = _sort_key(scores(_stack_pages(k_refs)))

    @pl.when(step == n_steps - 1)
    def _():
        r = lax.broadcasted_iota(I32, (H_FOX, PAGE_SIZE), 0)
        c = lax.broadcasted_iota(I32, (H_FOX, PAGE_SIZE), 1)
        keyn_ref[...] = _sort_key(jnp.where((c <= r) & (r < n_q), scores(kn_ref[...].astype(BF16)), NEG_INF))

        def count(pred):
            past = jnp.sum(jnp.sum(pred(key_ref[...]).astype(F32), axis=0), axis=1, keepdims=True)
            return past + jnp.sum(pred(keyn_ref[...]).astype(F32), axis=1, keepdims=True)

        def bit_step(bi, thr):
            cand = thr + lax.shift_left(jnp.int32(1), 31 - bi)
            return jnp.where(count(lambda k: k >= cand) >= topk, cand, thr)

        thr = lax.fori_loop(0, 32, bit_step, jnp.full((H_FOX, 1), INT_MIN, I32))
        need = topk - count(lambda k: k > thr)
        ties = count(lambda k: k == thr)
        excess = jnp.max(jnp.where((ties > need) & (thr > KEY_NEG_INF), 1.0, 0.0)) > 0.0

        @pl.when(jnp.logical_not(excess))
        def _():
            k = key_ref[...]
            bias_ref[...] = jnp.where((k >= thr) & (k > KEY_NEG_INF), 0.0, NEG_INF)
            k = keyn_ref[...]
            biasn_ref[...] = jnp.where((k >= thr) & (k > KEY_NEG_INF), 0.0, NEG_INF)

        @pl.when(excess)
        def _():
            upper = _upper_tri(PAGE_SIZE)

            def tie_block(k, before):
                tie = k == thr
                incl = jnp.dot(tie.astype(BF16), upper, preferred_element_type=F32) + before
                sel = (k > thr) | (tie & (incl - tie.astype(F32) < need))
                return jnp.where(sel & (k > KEY_NEG_INF), 0.0, NEG_INF), incl[:, PAGE_SIZE - 1:PAGE_SIZE]

            def fill(c, before):
                for p in range(pps):
                    lanes = slice(p * PAGE_SIZE, (p + 1) * PAGE_SIZE)
                    b, before = tie_block(key_ref[c, :, lanes], before)
                    bias_ref[c, :, lanes] = b
                return before

            before = lax.fori_loop(0, n_steps, fill, jnp.zeros((H_FOX, 1), F32))
            biasn_ref[...] = tie_block(keyn_ref[...], before)[0]


def _sample_index(page_table, iqx, iwx, kidx_new, pool_kidx, n_q):
    dec_b = iqx.shape[0]
    n_pages = page_table.shape[1]
    pps = min(IDX_PAGES_PER_STEP, n_pages)
    n_steps = n_pages // pps
    step_w = pps * PAGE_SIZE
    topk = min(TOPK_MAX, (n_pages * PAGE_SIZE + n_q) // 4)
    seq_spec = lambda blk: pl.BlockSpec((None,) + blk, lambda b, s, pt: (b,) + (0,) * len(blk))
    iq_rows = iqx.shape[1]
    in_specs = ([seq_spec((iq_rows, D_IDX)), seq_spec((iq_rows, 1)), seq_spec((D_IDX, PAGE_SIZE))]
                + [_page_spec((D_IDX, PAGE_SIZE), p, pps) for p in range(pps)])
    return pl.pallas_call(
        functools.partial(_sample_index_kernel, pps=pps, n_q=n_q, topk=topk),
        grid_spec=pltpu.PrefetchScalarGridSpec(
            num_scalar_prefetch=1,
            grid=(dec_b, n_steps),
            in_specs=in_specs,
            out_specs=[pl.BlockSpec((None, n_steps, H_FOX, step_w), lambda b, s, pt: (b, 0, 0, 0)),
                       pl.BlockSpec((None, H_FOX, PAGE_SIZE), lambda b, s, pt: (b, 0, 0))],
            scratch_shapes=[pltpu.VMEM((n_steps, H_FOX, step_w), I32), pltpu.VMEM((H_FOX, PAGE_SIZE), I32)]),
        out_shape=[jax.ShapeDtypeStruct((dec_b, n_steps, H_FOX, step_w), F32),
                   jax.ShapeDtypeStruct((dec_b, H_FOX, PAGE_SIZE), F32)],
        compiler_params=_cparams(("arbitrary", "arbitrary")),
        name="sample_index",
    )(page_table, iqx, iwx, kidx_new, *([pool_kidx] * pps))


def _sample_dsa_kernel(pt_ref, q_ref, kn_ref, vn_ref, bias_ref, biasn_ref, *rest, n_q, pps):
    k_refs, v_refs = rest[:pps], rest[pps:2 * pps]
    o_ref, m_ref, l_ref, acc_ref, qbd_ref = rest[2 * pps:]
    step = pl.program_id(1)

    @pl.when(step == 0)
    def _():
        m_ref[...] = jnp.full(m_ref.shape, -1e30, F32)
        l_ref[...] = jnp.zeros(l_ref.shape, F32)
        acc_ref[...] = jnp.zeros(acc_ref.shape, F32)
        qbd_ref[...] = _block_diag_q(q_ref[...], n_q)

    def attend(kt, vt, bias):
        s = jnp.dot(qbd_ref[...], kt, preferred_element_type=F32) + _rows_to_groups(bias, n_q)
        _softmax_page(s, vt, m_ref, l_ref, acc_ref)

    attend(_stack_pages(k_refs), _stack_pages(v_refs), bias_ref[0])

    @pl.when(step == pl.num_programs(1) - 1)
    def _():
        attend(kn_ref[...].astype(BF16), vn_ref[...].astype(BF16), biasn_ref[...])
        _finish_decode(o_ref, l_ref, acc_ref, n_q)


def _sample_dsa(page_table, q, k_new, v_new, bias, bias_new, pool_k, pool_v):
    dec_b, n_q, w = q.shape
    n_pages = page_table.shape[1]
    pps = min(KV_PAGES_PER_STEP, n_pages)
    rows = n_q * H_DSA
    kv_page = (H_DSA, HEAD_DIM, PAGE_SIZE)
    per_bias_chunk = bias.shape[3] // (pps * PAGE_SIZE)
    seq_spec = lambda blk: pl.BlockSpec((None,) + blk, lambda b, s, pt: (b,) + (0,) * len(blk))
    in_specs = ([seq_spec((n_q, w)), seq_spec((w, PAGE_SIZE)), seq_spec((w, PAGE_SIZE)),
                 pl.BlockSpec((None, 1, H_FOX, pps * PAGE_SIZE),
                              lambda b, s, pt: (b, s // per_bias_chunk, 0, s % per_bias_chunk)),
                 seq_spec((H_FOX, PAGE_SIZE))]
                + [_page_spec(kv_page, p, pps) for p in range(pps)]
                + [_page_spec(kv_page, p, pps) for p in range(pps)])
    return pl.pallas_call(
        functools.partial(_sample_dsa_kernel, n_q=n_q, pps=pps),
        grid_spec=pltpu.PrefetchScalarGridSpec(
            num_scalar_prefetch=1,
            grid=(dec_b, n_pages // pps),
            in_specs=in_specs,
            out_specs=pl.BlockSpec((None, n_q, w), lambda b, s, pt: (b, 0, 0)),
            scratch_shapes=[pltpu.VMEM((rows, 1), F32), pltpu.VMEM((rows, 1), F32),
                            pltpu.VMEM((rows, w), F32), pltpu.VMEM((rows, w), BF16)]),
        out_shape=jax.ShapeDtypeStruct((dec_b, n_q, w), BF16),
        compiler_params=_cparams(("arbitrary", "arbitrary")),
        name="sample_dsa",
    )(page_table, q, k_new, v_new, bias, bias_new, *([pool_k] * pps), *([pool_v] * pps))


def _layer_norm(r, g, b):
    mu = jnp.mean(r, axis=1, keepdims=True)
    d = r - mu
    var = jnp.mean(d * d, axis=1, keepdims=True)
    return d * lax.rsqrt(var + LN_EPS) * g + b


def _tail_kernel(of_ref, od_ref, x_ref, wo_ref, g_ref, b_ref, wr_ref, br_ref, cin_ref,
                 x1_ref, e_ref, gate_ref, rank_ref, cnt_ref, carry_ref, *, alpha):
    i = pl.program_id(0)
    tm = x_ref.shape[0]

    @pl.when(i == 0)
    def _():
        carry_ref[...] = cin_ref[...]

    r = (alpha * x_ref[...]
         + jnp.dot(of_ref[...], wo_ref[0:W_FOX, :], preferred_element_type=F32)
         + jnp.dot(od_ref[...], wo_ref[W_FOX:W_FOX + W_DSA, :], preferred_element_type=F32))
    x1 = _layer_norm(r, g_ref[...], b_ref[...])
    x1_ref[...] = x1

    logits = jnp.dot(x1.astype(BF16), wr_ref[...], preferred_element_type=F32) + br_ref[...]
    lane = lax.broadcasted_iota(I32, (tm, LANES), 1)
    lane_f = lane.astype(F32)
    vals, idxs, hots = [], [], []
    lg = logits
    for _ in range(TOP_K_EXPERTS):
        m = jnp.max(lg, axis=1, keepdims=True)
        idx = jnp.min(jnp.where(lg == m, lane_f, float(LANES)), axis=1, keepdims=True)
        hot = lane_f == idx
        vals.append(m)
        idxs.append(idx)
        hots.append(hot)
        lg = jnp.where(hot, NEG_INF, lg)
    exps = [jnp.exp(v - vals[0]) for v in vals]
    den = exps[0]
    for e in exps[1:]:
        den = den + e
    sel = jnp.zeros((tm, LANES), F32)
    for hot in hots:
        sel = sel + hot.astype(F32)
    rr = lax.broadcasted_iota(I32, (tm, tm), 0)
    cc = lax.broadcasted_iota(I32, (tm, tm), 1)
    strict_lower = (cc < rr).astype(BF16)
    before = jnp.dot(strict_lower, sel.astype(BF16), preferred_element_type=F32) + carry_ref[...]
    e_out = jnp.zeros((tm, LANES), F32)
    g_out = jnp.zeros((tm, LANES), F32)
    r_out = jnp.zeros((tm, LANES), F32)
    for j in range(TOP_K_EXPERTS):
        rank = jnp.sum(jnp.where(hots[j], before, 0.0), axis=1, keepdims=True)
        e_out = jnp.where(lane == j, idxs[j], e_out)
        g_out = jnp.where(lane == j, exps[j] / den, g_out)
        r_out = jnp.where(lane == j, rank, r_out)
    e_ref[...] = e_out.astype(I32)
    gate_ref[...] = g_out
    rank_ref[...] = r_out.astype(I32)
    carry_ref[...] = carry_ref[...] + jnp.sum(sel, axis=0, keepdims=True)
    cnt_ref[...] = carry_ref[...]


def _tail(of, od, x2d, wo, g1, b1, wr, br, cin, alpha, tm):
    rows, d = x2d.shape
    row_spec = lambda w: pl.BlockSpec((tm, w), lambda i: (i, 0))
    full = lambda a: pl.BlockSpec(a.shape, lambda i: (0,) * a.ndim)
    return pl.pallas_call(
        functools.partial(_tail_kernel, alpha=alpha),
        grid=(rows // tm,),
        in_specs=[row_spec(W_FOX), row_spec(W_DSA), row_spec(d), full(wo), full(g1), full(b1),
                  full(wr), full(br), full(cin)],
        out_specs=[row_spec(d), row_spec(LANES), row_spec(LANES), row_spec(LANES),
                   pl.BlockSpec((1, LANES), lambda i: (0, 0))],
        out_shape=[jax.ShapeDtypeStruct((rows, d), F32), jax.ShapeDtypeStruct((rows, LANES), I32),
                   jax.ShapeDtypeStruct((rows, LANES), F32), jax.ShapeDtypeStruct((rows, LANES), I32),
                   jax.ShapeDtypeStruct((1, LANES), F32)],
        scratch_shapes=[pltpu.VMEM((1, LANES), F32)],
        compiler_params=_cparams(("arbitrary",)),
        name="tail_router",
    )(of, od, x2d, wo, g1, b1, wr, br, cin)


EXPERT_BLOCK = 256
DISPATCH_TOKENS = 128


def _dispatch_kernel(dest_ref, x_ref, xs_in, xs_out, sem, *, td):
    del xs_in

    def row_copy(src_row, dst_row):
        return pltpu.make_async_copy(x_ref.at[pl.ds(src_row, 1), :], xs_out.at[pl.ds(dst_row, 1), :], sem)

    def issue(t, c):
        for j in range(TOP_K_EXPERTS):
            row_copy(t, dest_ref[t * TOP_K_EXPERTS + j]).start()
        return c

    lax.fori_loop(0, td, issue, 0)

    def drain(t, c):
        for j in range(TOP_K_EXPERTS):
            row_copy(0, 0).wait()
        return c

    lax.fori_loop(0, td, drain, 0)


def _dispatch(dest_flat, x1, xs):
    rows, d = x1.shape
    td = DISPATCH_TOKENS
    return pl.pallas_call(
        functools.partial(_dispatch_kernel, td=td),
        grid=(rows // td,),
        in_specs=[pl.BlockSpec((td * TOP_K_EXPERTS,), lambda i: (i,), memory_space=pltpu.SMEM),
                  pl.BlockSpec((td, d), lambda i: (i, 0)), pl.BlockSpec(memory_space=pl.ANY)],
        out_specs=pl.BlockSpec(memory_space=pl.ANY),
        out_shape=jax.ShapeDtypeStruct(xs.shape, xs.dtype),
        scratch_shapes=[pltpu.SemaphoreType.DMA(())],
        input_output_aliases={2: 0},
        compiler_params=_cparams(("arbitrary",)),
        name="moe_dispatch",
    )(dest_flat, x1, xs)


def _ffn_kernel(be_ref, first_ref, nused_ref, xs_ref, wg_ref, wu_ref, wd_ref, bg_ref, bu_ref, bd_ref,
                ys_ref, wgb_ref, wub_ref, wdb_ref):
    b = pl.program_id(0)

    @pl.when(first_ref[b] == 1)
    def _():
        wgb_ref[...] = wg_ref[...].astype(BF16)
        wub_ref[...] = wu_ref[...].astype(BF16)
        wdb_ref[...] = wd_ref[...].astype(BF16)

    @pl.when(b < nused_ref[0])
    def _():
        xb = xs_ref[...].astype(BF16)
        g = jnp.dot(xb, wgb_ref[...], preferred_element_type=F32) + bg_ref[...]
        u = jnp.dot(xb, wub_ref[...], preferred_element_type=F32) + bu_ref[...]
        g = jnp.minimum(g, SWIGLU_LIMIT)
        u = jnp.clip(u, -SWIGLU_LIMIT, SWIGLU_LIMIT)
        h = g * (1.0 / (1.0 + jnp.exp(-SWIGLU_ALPHA * g))) * (u + 1.0)
        ys_ref[...] = jnp.dot(h.astype(BF16), wdb_ref[...], preferred_element_type=F32) + bd_ref[...]

    @pl.when(b >= nused_ref[0])
    def _():
        ys_ref[...] = jnp.zeros(ys_ref.shape, F32)


def _expert_ffn(block_e, is_first, n_used, xs, wg, wu, wd, bg, bu, bd):
    p, d = xs.shape
    bs = EXPERT_BLOCK
    de = wg.shape[2]
    w_spec = lambda shape: pl.BlockSpec((None,) + shape, lambda b, be, fi, nu: (be[b], 0, 0))
    return pl.pallas_call(
        _ffn_kernel,
        grid_spec=pltpu.PrefetchScalarGridSpec(
            num_scalar_prefetch=3,
            grid=(p // bs,),
            in_specs=[pl.BlockSpec((bs, d), lambda b, be, fi, nu: (b, 0)),
                      w_spec((d, de)), w_spec((d, de)), w_spec((de, d)),
                      w_spec((1, de)), w_spec((1, de)), w_spec((1, d))],
            out_specs=pl.BlockSpec((bs, d), lambda b, be, fi, nu: (b, 0)),
            scratch_shapes=[pltpu.VMEM((d, de), BF16), pltpu.VMEM((d, de), BF16), pltpu.VMEM((de, d), BF16)]),
        out_shape=jax.ShapeDtypeStruct((p, d), F32),
        compiler_params=_cparams(("arbitrary",)),
        name="moe_ffn",
    )(block_e, is_first, n_used, xs, wg, wu, wd, bg, bu, bd)


def _combine_kernel(dest_ref, ys_hbm, gate_ref, x1_ref, g_ref, b_ref, o_ref, buf_ref, sem, *, tc, alpha):
    def row_copy(src_row, j, t):
        return pltpu.make_async_copy(ys_hbm.at[pl.ds(src_row, 1), :], buf_ref.at[j, pl.ds(t, 1), :], sem)

    def issue(t, c):
        for j in range(TOP_K_EXPERTS):
            row_copy(dest_ref[t * TOP_K_EXPERTS + j], j, t).start()
        return c

    lax.fori_loop(0, tc, issue, 0)

    def drain(t, c):
        for j in range(TOP_K_EXPERTS):
            row_copy(0, j, t).wait()
        return c

    lax.fori_loop(0, tc, drain, 0)

    gates = gate_ref[...]
    moe = gates[:, 0:1] * buf_ref[0]
    for j in range(1, TOP_K_EXPERTS):
        moe = moe + gates[:, j:j + 1] * buf_ref[j]
    o_ref[...] = _layer_norm(alpha * x1_ref[...] + moe, g_ref[...], b_ref[...])


def _combine(dest_flat, ys, gates, x1, g2, b2, alpha):
    rows, d = x1.shape
    tc = DISPATCH_TOKENS
    return pl.pallas_call(
        functools.partial(_combine_kernel, tc=tc, alpha=alpha),
        grid=(rows // tc,),
        in_specs=[pl.BlockSpec((tc * TOP_K_EXPERTS,), lambda i: (i,), memory_space=pltpu.SMEM),
                  pl.BlockSpec(memory_space=pl.ANY),
                  pl.BlockSpec((tc, LANES), lambda i: (i, 0)),
                  pl.BlockSpec((tc, d), lambda i: (i, 0)),
                  pl.BlockSpec((1, d), lambda i: (0, 0)),
                  pl.BlockSpec((1, d), lambda i: (0, 0))],
        out_specs=pl.BlockSpec((tc, d), lambda i: (i, 0)),
        out_shape=jax.ShapeDtypeStruct((rows, d), F32),
        scratch_shapes=[pltpu.VMEM((TOP_K_EXPERTS, tc, d), F32), pltpu.SemaphoreType.DMA(())],
        compiler_params=_cparams(("arbitrary",)),
        name="moe_combine",
    )(dest_flat, ys, gates, x1, g2, b2)


def _prep_w_in(w, b_f):
    d = w.shape[0]
    sizes = (W_FOX, W_FOX, W_FOX, H_FOX, W_DSA, W_DSA, W_DSA, H_IDX * D_IDX, D_IDX, H_IDX)
    parts, acc = [], 0
    for s in sizes:
        parts.append(w[:, acc:acc + s])
        acc += s
    fq, fk, fv, fl, dq, dk, dv, iq, ik, iw = parts
    small = jnp.concatenate([ik, fl, iw, jnp.zeros((d, LANES - D_IDX - H_FOX - H_IDX), w.dtype)], axis=1)
    wp = jnp.concatenate([fq, fk, fv, dq, dk, dv, iq, small], axis=1).astype(BF16)
    bf_row = jnp.zeros((1, LANES), F32).at[0, SM_LF:SM_LF + H_FOX].set(b_f.astype(F32))
    return wp, bf_row


def _rope_tables(pos):
    half = HEAD_DIM // 2
    inv = ROPE_THETA ** (-jnp.arange(half, dtype=F32) / half)
    ang = pos.astype(F32)[:, None] * inv[None, :]
    cos, sin = jnp.cos(ang), jnp.sin(ang)
    cos_t = jnp.concatenate([cos, cos, cos, cos], axis=1)
    sin_t = jnp.concatenate([-sin, sin, -sin, sin], axis=1)
    return cos_t, sin_t


def _pad_rows(a, n):
    return jnp.pad(a, ((0, 0), (0, n - a.shape[1]), (0, 0)))


def kernel(x_prompt, x_sample, cache_fox_k, cache_fox_v, cache_fox_logf, cache_dsa_k, cache_dsa_v,
           cache_dsa_kidx, page_table, w_in, b_f, w_o, ln1_g, ln1_b, w_router, b_router, w_gate, b_gate,
           w_up, b_up, w_down, b_down, ln2_g, ln2_b):
    batch, seq, d = x_prompt.shape
    dec_b, n_q, _ = x_sample.shape
    depth = w_in.shape[0]
    n_pool = cache_fox_k.shape[1]
    n_pages = page_table.shape[1]
    past = n_pages * PAGE_SIZE
    alpha = (2.0 * depth) ** 0.25
    rows_p, rows_s = batch * seq, dec_b * n_q

    cos_p, sin_p = _rope_tables(jnp.arange(seq))
    cos_s, sin_s = _rope_tables(past + (jnp.arange(rows_s) % n_q))

    xp = x_prompt.reshape(rows_p, d)
    xs_ = x_sample.reshape(rows_s, d)
    outs_p = [[] for _ in range(6)]
    outs_s = [[] for _ in range(6)]
    for l in range(depth):
        wp, bf_row = _prep_w_in(w_in[l], b_f[l])
        wo = w_o[l].astype(BF16)
        wr = jnp.pad(w_router[l], ((0, 0), (0, LANES - N_EXPERTS))).astype(BF16)
        br = jnp.full((1, LANES), NEG_INF, F32).at[0, :N_EXPERTS].set(b_router[l].astype(F32))
        g1, b1 = ln1_g[l].reshape(1, d), ln1_b[l].reshape(1, d)
        g2, b2 = ln2_g[l].reshape(1, d), ln2_b[l].reshape(1, d)

        pp = _in_proj(xp, wp, bf_row, cos_p, sin_p, batch, seq, tm=256)
        o_fox = _fox_prompt(pp["fqt"], pp["fkb"], pp["fvtc"], pp["cb"], batch, seq)
        o_dsa = _dsa_prompt(pp["iqt"], pp["iwt"], pp["ikb"], pp["dqt"], pp["dkb"], pp["dvtc"], batch, seq,
                            tq=256)

        ps = _in_proj(xs_, wp, bf_row, cos_s, sin_s, 1, rows_s, tm=rows_s)
        ps = {k: v[0] for k, v in ps.items() if v.ndim == 3}
        pool_fk = cache_fox_k[l].transpose(0, 2, 3, 1)
        pool_fv = cache_fox_v[l].transpose(0, 2, 3, 1)
        pool_lft = jnp.swapaxes(cache_fox_logf[l], 1, 2)
        pool_dk = cache_dsa_k[l].transpose(0, 2, 3, 1)
        pool_dv = cache_dsa_v[l].transpose(0, 2, 3, 1)
        pool_ki = jnp.swapaxes(cache_dsa_kidx[l], 1, 2)
        per_seq = lambda a: a.reshape(a.shape[0], dec_b, n_q).transpose(1, 0, 2)
        new_t = lambda a: jnp.pad(per_seq(a), ((0, 0), (0, 0), (0, PAGE_SIZE - n_q)))
        rows_q = lambda a: a.T.reshape(dec_b, n_q, a.shape[0])
        so_fox = _sample_fox(page_table, rows_q(ps["fqt"]), new_t(ps["fkt"]), new_t(ps["fvt"]),
                             new_t(ps["lft"]), pool_fk, pool_fv, pool_lft)
        half = SUBLANES // 2
        assert n_q <= half
        iqx = ps["iqt"].reshape(H_IDX, D_IDX, dec_b, n_q).transpose(2, 0, 3, 1)
        iqx = jnp.pad(iqx, ((0, 0), (0, 0), (0, half - n_q), (0, 0))).reshape(dec_b, H_IDX * half, D_IDX)
        iwx = jnp.pad(per_seq(ps["iwt"]), ((0, 0), (0, 0), (0, half - n_q)))
        iwx = iwx.reshape(dec_b, H_IDX * half, 1)
        sel_bias, sel_bias_new = _sample_index(page_table, iqx, iwx, new_t(ps["ikt"]), pool_ki, n_q)
        so_dsa = _sample_dsa(page_table, rows_q(ps["dqt"]), new_t(ps["dkt"]), new_t(ps["dvt"]), sel_bias,
                             sel_bias_new, pool_dk, pool_dv)

        cin = jnp.zeros((1, LANES), F32)
        x1_p, e_p, gate_p, rank_p, cnt_p = _tail(o_fox, o_dsa, xp, wo, g1, b1, wr, br, cin, alpha, tm=256)
        x1_s, e_s, gate_s, rank_s, cnt = _tail(so_fox.reshape(rows_s, W_FOX), so_dsa.reshape(rows_s, W_DSA),
                                               xs_, wo, g1, b1, wr, br, cnt_p, alpha, tm=rows_s)

        bs = EXPERT_BLOCK
        counts = cnt[0, :N_EXPERTS].astype(I32)
        padded = (counts + bs - 1) // bs * bs
        pad_end = jnp.cumsum(padded)
        pad_start = pad_end - padded
        n_assign = (rows_p + rows_s) * TOP_K_EXPERTS
        n_blocks = -(-(n_assign + N_EXPERTS * (bs - 1)) // bs)
        blk = jnp.arange(n_blocks, dtype=I32)
        n_used = (pad_end[-1] // bs).astype(I32)
        block_e = jnp.sum((pad_end[None, :] <= (blk * bs)[:, None]).astype(I32), axis=1)
        block_e = jnp.minimum(block_e, N_EXPERTS - 1)
        block_e = jnp.where(blk < n_used, block_e, block_e[jnp.maximum(n_used - 1, 0)])
        is_first = jnp.concatenate([jnp.ones((1,), I32), (block_e[1:] != block_e[:-1]).astype(I32)])
        dest_p = (pad_start[e_p[:, :TOP_K_EXPERTS]] + rank_p[:, :TOP_K_EXPERTS]).reshape(-1)
        dest_s = (pad_start[e_s[:, :TOP_K_EXPERTS]] + rank_s[:, :TOP_K_EXPERTS]).reshape(-1)

        xs_sorted = jnp.zeros((n_blocks * bs, d), F32)
        xs_sorted = _dispatch(dest_p, x1_p, xs_sorted)
        xs_sorted = _dispatch(dest_s, x1_s, xs_sorted)
        ys = _expert_ffn(block_e, is_first, n_used.reshape(1), xs_sorted, w_gate[l], w_up[l], w_down[l],
                         b_gate[l].reshape(N_EXPERTS, 1, -1), b_up[l].reshape(N_EXPERTS, 1, -1),
                         b_down[l].reshape(N_EXPERTS, 1, -1))
        xp_new = _combine(dest_p, ys, gate_p, x1_p, g2, b2, alpha)
        xs_new = _combine(dest_s, ys, gate_s, x1_s, g2, b2, alpha)

        heads_p = lambda a: a.reshape(batch, -1, HEAD_DIM, seq).transpose(0, 3, 1, 2)
        heads_s = lambda a: a.reshape(-1, HEAD_DIM, dec_b, n_q).transpose(2, 3, 0, 1)
        for acc, a in zip(outs_p, (heads_p(pp["fkt"]), heads_p(pp["fvt"]), pp["lft"].transpose(0, 2, 1),
                                   heads_p(pp["dkt"]), heads_p(pp["dvt"]), pp["ikt"].transpose(0, 2, 1))):
            acc.append(a)
        for acc, a in zip(outs_s, (heads_s(ps["fkt"]), heads_s(ps["fvt"]), per_seq(ps["lft"]).transpose(0, 2, 1),
                                   heads_s(ps["dkt"]), heads_s(ps["dvt"]), per_seq(ps["ikt"]).transpose(0, 2, 1))):
            acc.append(a)
        xp, xs_ = xp_new, xs_new

    stack = lambda lst: [jnp.stack(r, axis=0) for r in lst]
    return (xp.reshape(batch, seq, d), xs_.reshape(dec_b, n_q, d), *stack(outs_p), *stack(outs_s))
```

```python
import functools

import jax
import jax.numpy as jnp
from jax import lax
from jax.experimental import pallas as pl
from jax.experimental.pallas import tpu as pltpu

F32, BF16, I32 = jnp.float32, jnp.bfloat16, jnp.int32
NEG_INF = float("-inf")

HEAD_DIM = 64
H_FOX = 8
H_DSA = 8
W_FOX = H_FOX * HEAD_DIM
W_DSA = H_DSA * HEAD_DIM
H_IDX = 16
D_IDX = 64
TOPK_MAX = 256
PAGE_SIZE = 128
ROPE_THETA = 10000.0
ATTN_SCALE = HEAD_DIM ** -0.5
IDX_SCALE = D_IDX ** -0.5
IDX_W_SCALE = H_IDX ** -0.5
N_EXPERTS = 32
TOP_K_EXPERTS = 4
SWIGLU_LIMIT = 7.0
SWIGLU_ALPHA = 1.702
LN_EPS = 1e-5

LANES = 128
KEY_NEG_INF = -2139095041
INT_MIN = -2 ** 31
VMEM_LIMIT = 56 * 1024 * 1024

NT_DIMS = (((1,), (1,)), ((), ()))


def _cparams(sem):
    return pltpu.CompilerParams(dimension_semantics=sem, vmem_limit_bytes=VMEM_LIMIT)


def _split3(x):
    hi = x.astype(BF16)
    r1 = x - hi.astype(F32)
    mid = r1.astype(BF16)
    lo = (r1 - mid.astype(F32)).astype(BF16)
    return hi, mid, lo


def _lane_cumsum(x, upper):
    hi, mid, lo = _split3(x)
    return (jnp.dot(hi, upper, preferred_element_type=F32)
            + jnp.dot(mid, upper, preferred_element_type=F32)
            + jnp.dot(lo, upper, preferred_element_type=F32))


def _upper_tri(n):
    r = lax.broadcasted_iota(I32, (n, n), 0)
    c = lax.broadcasted_iota(I32, (n, n), 1)
    return (r <= c).astype(BF16)


def _sort_key(x):
    bits = pltpu.bitcast(x, I32)
    return bits ^ ((bits >> 31) & 0x7FFFFFFF)


C_FQ, C_FK, C_FV, C_DQ, C_DK, C_DV, C_IQ, C_SM = 0, 512, 1024, 1536, 2048, 2560, 3072, 4096
N_PROJ = 4224
SM_IK, SM_LF, SM_IW = 0, 64, 72


def _in_proj_kernel(x_ref, w_ref, cos_ref, sin_ref, bf_ref,
                    fqt_ref, fkb_ref, fkt_ref, fvt_ref, fvtc_ref, cb_ref, lft_ref,
                    dqt_ref, dkb_ref, dkt_ref, dvt_ref, dvtc_ref, iqt_ref, ikb_ref, ikt_ref, iwt_ref,
                    carry_ref, *, tiles_per_seq):
    @pl.when(pl.program_id(0) % tiles_per_seq == 0)
    def _():
        carry_ref[...] = jnp.zeros(carry_ref.shape, F32)

    xb = x_ref[...].astype(BF16)
    tm = xb.shape[0]
    cos = cos_ref[...]
    sin = sin_ref[...]
    lane = lax.broadcasted_iota(I32, (tm, LANES), 1)
    first_half = (lane & (HEAD_DIM - 1)) < HEAD_DIM // 2

    def mm(c0, n):
        return jnp.dot(xb, w_ref[:, c0:c0 + n], preferred_element_type=F32)

    def rope(y):
        outs = []
        for c in range(0, y.shape[1], LANES):
            yc = y[:, c:c + LANES]
            partner = jnp.where(first_half, pltpu.roll(yc, LANES - HEAD_DIM // 2, 1),
                                pltpu.roll(yc, HEAD_DIM // 2, 1))
            outs.append(yc * cos + partner * sin)
        return outs[0] if len(outs) == 1 else jnp.concatenate(outs, axis=1)

    y = mm(C_FQ, W_FOX)
    fqt_ref[...] = (y * ATTN_SCALE).T.astype(BF16)
    y = mm(C_FK, W_FOX)
    fkb_ref[...] = y.astype(BF16)
    fkt_ref[...] = y.T
    yt = mm(C_FV, W_FOX).T
    fvt_ref[...] = yt
    fvtc_ref[...] = yt.astype(BF16)
    y = rope(mm(C_DQ, W_DSA))
    dqt_ref[...] = (y * ATTN_SCALE).T.astype(BF16)
    y = rope(mm(C_DK, W_DSA))
    dkb_ref[...] = y.astype(BF16)
    dkt_ref[...] = y.T
    yt = mm(C_DV, W_DSA).T
    dvt_ref[...] = yt
    dvtc_ref[...] = yt.astype(BF16)
    y = rope(mm(C_IQ, H_IDX * D_IDX))
    iqt_ref[...] = (y * IDX_SCALE).T.astype(BF16)
    z = mm(C_SM, LANES)
    r = rope(z)
    ikb_ref[...] = r[:, SM_IK:SM_IK + D_IDX].astype(BF16)
    zb = z + bf_ref[...]
    lsg = -(jnp.maximum(-zb, 0.0) + jnp.log1p(jnp.exp(-jnp.abs(zb))))
    small_t = jnp.where(lane < SM_LF, r, jnp.where(lane < SM_IW, lsg, z * IDX_W_SCALE)).T
    ikt_ref[...] = small_t[SM_IK:SM_IK + D_IDX, :]
    lft_ref[...] = small_t[SM_LF:SM_LF + H_FOX, :]
    iwt_ref[...] = small_t[SM_IW:SM_IW + H_IDX, :]

    hi, mid, lo = _split3(jnp.where((lane >= SM_LF) & (lane < SM_IW), lsg, 0.0))
    rr = lax.broadcasted_iota(I32, (tm, tm), 0)
    cc = lax.broadcasted_iota(I32, (tm, tm), 1)
    lower = (cc <= rr).astype(BF16)
    parts = jnp.dot(lower, jnp.concatenate([hi, mid, lo], axis=1), preferred_element_type=F32)
    cs = parts[:, :LANES] + parts[:, LANES:2 * LANES] + parts[:, 2 * LANES:] + carry_ref[...]
    carry_ref[...] = cs[tm - 1:tm, :]
    for h in range(H_FOX):
        cb_ref[h] = jnp.broadcast_to(cs[:, SM_LF + h:SM_LF + h + 1], (tm, LANES))


FOX_KV_CHUNK = 256
DSA_KV_CHUNK = 512


def _in_proj(x2d, wp, bf_row, cos_t, sin_t, n_seq, seq, tm):
    rows, d = x2d.shape
    tps = seq // tm
    ckf, ckd = min(FOX_KV_CHUNK, seq), min(DSA_KV_CHUNK, seq)
    n_tab = cos_t.shape[0] // tm
    t_spec = lambda w: pl.BlockSpec((None, w, tm), lambda i: (i // tps, 0, i % tps))
    n_spec = lambda w: pl.BlockSpec((tm, w), lambda i: (i, 0))
    c_spec = lambda w, ck: pl.BlockSpec((None, None, w, tm),
                                        lambda i: (i // tps, (i % tps) * tm // ck, 0, (i % tps) % (ck // tm)))
    t_shape = lambda w, dt: jax.ShapeDtypeStruct((n_seq, w, seq), dt)
    n_shape = lambda w, dt: jax.ShapeDtypeStruct((rows, w), dt)
    c_shape = lambda w, ck: jax.ShapeDtypeStruct((n_seq, seq // ck, w, ck), BF16)
    wi = H_IDX * D_IDX
    outs = [("fqt", t_spec(W_FOX), t_shape(W_FOX, BF16)), ("fkb", n_spec(W_FOX), n_shape(W_FOX, BF16)),
            ("fkt", t_spec(W_FOX), t_shape(W_FOX, F32)), ("fvt", t_spec(W_FOX), t_shape(W_FOX, F32)),
            ("fvtc", c_spec(W_FOX, ckf), c_shape(W_FOX, ckf)),
            ("cb", pl.BlockSpec((H_FOX, tm, LANES), lambda i: (0, i, 0)),
             jax.ShapeDtypeStruct((H_FOX, rows, LANES), F32)),
            ("lft", t_spec(H_FOX), t_shape(H_FOX, F32)),
            ("dqt", t_spec(W_DSA), t_shape(W_DSA, BF16)), ("dkb", n_spec(W_DSA), n_shape(W_DSA, BF16)),
            ("dkt", t_spec(W_DSA), t_shape(W_DSA, F32)), ("dvt", t_spec(W_DSA), t_shape(W_DSA, F32)),
            ("dvtc", c_spec(W_DSA, ckd), c_shape(W_DSA, ckd)),
            ("iqt", t_spec(wi), t_shape(wi, BF16)), ("ikb", n_spec(D_IDX), n_shape(D_IDX, BF16)),
            ("ikt", t_spec(D_IDX), t_shape(D_IDX, F32)), ("iwt", t_spec(H_IDX), t_shape(H_IDX, F32))]
    res = pl.pallas_call(
        functools.partial(_in_proj_kernel, tiles_per_seq=tps),
        grid=(rows // tm,),
        in_specs=[pl.BlockSpec((tm, d), lambda i: (i, 0)),
                  pl.BlockSpec((d, N_PROJ), lambda i: (0, 0)),
                  pl.BlockSpec((tm, LANES), lambda i: (i % n_tab, 0)),
                  pl.BlockSpec((tm, LANES), lambda i: (i % n_tab, 0)),
                  pl.BlockSpec((1, LANES), lambda i: (0, 0))],
        out_specs=[o[1] for o in outs],
        out_shape=[o[2] for o in outs],
        scratch_shapes=[pltpu.VMEM((1, LANES), F32)],
        compiler_params=_cparams(("arbitrary",)),
        name="in_proj",
    )(x2d, wp, cos_t, sin_t, bf_row)
    return {o[0]: r for o, r in zip(outs, res)}


SUBLANES = 8


def _over_sublanes(x8, reduce_fn):
    return jnp.broadcast_to(reduce_fn(x8, axis=0, keepdims=True), x8.shape)


def _by_sublane_tile(x):
    return x.reshape(x.shape[0] // SUBLANES, SUBLANES, x.shape[1])


def _flash_step(n_heads, score_fn, vt_fn, st):
    m_ref, l_ref, acc_ref, _, s_ref, p_ref, alpha_ref = st
    for h in range(n_heads):
        s_ref[h] = score_fn(h)
    for h in range(n_heads):
        s3 = _by_sublane_tile(s_ref[h])
        m_old = m_ref[h]
        m_new = jnp.maximum(m_old, _over_sublanes(jnp.max(s3, axis=0), jnp.max))
        alpha = jnp.exp(m_old - m_new)
        p3 = jnp.exp(s3 - m_new[None])
        l_ref[h] = alpha * l_ref[h] + _over_sublanes(jnp.sum(p3, axis=0), jnp.sum)
        p_ref[h] = p3.reshape(s_ref.shape[1:]).astype(BF16)
        alpha_ref[h] = alpha
        m_ref[h] = m_new
    for h in range(n_heads):
        pv = jnp.dot(vt_fn(h), p_ref[h], preferred_element_type=F32)
        acc_ref[h] = (_by_sublane_tile(acc_ref[h]) * alpha_ref[h][None]).reshape(acc_ref.shape[1:]) + pv


def _flash_init(m_ref, l_ref, acc_ref):
    m_ref[...] = jnp.full(m_ref.shape, -1e30, F32)
    l_ref[...] = jnp.zeros(l_ref.shape, F32)
    acc_ref[...] = jnp.zeros(acc_ref.shape, F32)


def _flash_finish(o_ref, l_ref, acc_ref, n_heads):
    tq = acc_ref.shape[2]
    for pair in range(n_heads // 2):
        parts = [(_by_sublane_tile(acc_ref[h]) / l_ref[h][None]).reshape(HEAD_DIM, tq)
                 for h in (2 * pair, 2 * pair + 1)]
        o_ref[:, pair * LANES:(pair + 1) * LANES] = jnp.concatenate(parts, axis=0).T.astype(BF16)


def _flash_scratch(n_heads, tk, tq):
    return [pltpu.VMEM((n_heads, SUBLANES, tq), F32), pltpu.VMEM((n_heads, SUBLANES, tq), F32),
            pltpu.VMEM((n_heads, HEAD_DIM, tq), F32), pltpu.VMEM((n_heads, LANES, tq), BF16),
            pltpu.VMEM((n_heads, tk, tq), F32), pltpu.VMEM((n_heads, tk, tq), BF16),
            pltpu.VMEM((n_heads, SUBLANES, tq), F32)]


def _stage_pair_masked_q(qt_ref, qpm_ref, n_heads):
    for h in range(n_heads):
        pair = h // 2
        rows = qt_ref[pair * LANES:(pair + 1) * LANES, :].astype(F32)
        r = lax.broadcasted_iota(I32, rows.shape, 0)
        qpm_ref[h] = jnp.where(r // HEAD_DIM == h % 2, rows, 0.0).astype(BF16)


def _fox_kernel(qt_ref, k_ref, vt_ref, cb_ref, o_ref, *st, tq):
    m_ref, l_ref, acc_ref, qpm_ref = st[:4]
    i = pl.program_id(1)
    _stage_pair_masked_q(qt_ref, qpm_ref, H_FOX)
    _flash_init(m_ref, l_ref, acc_ref)
    key_row = lax.broadcasted_iota(I32, (tq, tq), 0)
    query_col = lax.broadcasted_iota(I32, (tq, tq), 1)
    visible = key_row <= query_col

    def kv_step(j, masked):
        start = pl.multiple_of(j * tq, tq)

        def scores(h):
            pair = h // 2
            s = jnp.dot(k_ref[pl.ds(start, tq), pair * LANES:(pair + 1) * LANES], qpm_ref[h],
                        preferred_element_type=F32)
            s = s - jnp.concatenate([cb_ref[h, pl.ds(start, tq), :]] * (tq // LANES), axis=1)
            return jnp.where(visible, s, NEG_INF) if masked else s

        _flash_step(H_FOX, scores, lambda h: vt_ref[j, h * HEAD_DIM:(h + 1) * HEAD_DIM, :], st)

    def body(j, c):
        kv_step(j, False)
        return c

    lax.fori_loop(0, i, body, 0)
    kv_step(i, True)
    _flash_finish(o_ref, l_ref, acc_ref, H_FOX)


def _fox_prompt(qt, kb, vtc, cb, batch, seq):
    tq = FOX_KV_CHUNK
    nq = seq // tq
    return pl.pallas_call(
        functools.partial(_fox_kernel, tq=tq),
        grid=(batch, nq),
        in_specs=[pl.BlockSpec((None, W_FOX, tq), lambda b, i: (b, 0, i)),
                  pl.BlockSpec((seq, W_FOX), lambda b, i: (b, 0)),
                  pl.BlockSpec((None, nq, W_FOX, tq), lambda b, i: (b, 0, 0, 0)),
                  pl.BlockSpec((H_FOX, seq, LANES), lambda b, i: (0, b, 0))],
        out_specs=pl.BlockSpec((tq, W_FOX), lambda b, i: (b * nq + i, 0)),
        out_shape=jax.ShapeDtypeStruct((batch * seq, W_FOX), BF16),
        scratch_shapes=_flash_scratch(H_FOX, tq, tq),
        compiler_params=_cparams(("arbitrary", "arbitrary")),
        name="fox_prompt",
    )(qt, kb, vtc, cb)


def _dsa_kernel(iqt_ref, iwt_ref, ik_ref, qt_ref, k_ref, vt_ref, o_ref, key_ref, bias_ref, *st,
                tq, ck, topk):
    m_ref, l_ref, acc_ref, qpm_ref = st[:4]
    i = pl.program_id(1)
    q0 = i * tq
    n_chunks = (q0 + tq + ck - 1) // ck
    key_row = lax.broadcasted_iota(I32, (ck, tq), 0)
    query_pos = q0 + lax.broadcasted_iota(I32, (ck, tq), 1)
    iw8 = [jnp.broadcast_to(iwt_ref[h:h + 1, :], (SUBLANES, tq)) for h in range(H_IDX)]

    def score_chunk(c, _):
        start = pl.multiple_of(c * ck, ck)
        ikc = ik_ref[pl.ds(start, ck), :]
        sc = jnp.zeros((ck // SUBLANES, SUBLANES, tq), F32)
        for h in range(H_IDX):
            rel = jnp.dot(ikc, iqt_ref[h * D_IDX:(h + 1) * D_IDX, :], preferred_element_type=F32)
            sc = sc + _by_sublane_tile(jnp.maximum(rel, 0.0)) * iw8[h][None]
        sc = jnp.where(key_row + start <= query_pos, sc.reshape(ck, tq), NEG_INF)
        key_ref[c] = _sort_key(sc)
        return 0

    lax.fori_loop(0, n_chunks, score_chunk, 0)

    def count(pred):
        def body(c, acc):
            return acc + jnp.sum(pred(_by_sublane_tile(key_ref[c])).astype(F32), axis=0)
        return _over_sublanes(lax.fori_loop(0, n_chunks, body, jnp.zeros((SUBLANES, tq), F32)), jnp.sum)

    def extreme(pick, reduce_fn, fill):
        def body(c, acc):
            k = _by_sublane_tile(key_ref[c])
            return pick(acc, reduce_fn(jnp.where(k > KEY_NEG_INF, k, fill), axis=0))
        init = jnp.full((SUBLANES, tq), fill, I32)
        return _over_sublanes(lax.fori_loop(0, n_chunks, body, init), reduce_fn)

    lo0 = extreme(jnp.minimum, jnp.min, jnp.int32(2 ** 31 - 1))
    hi0 = extreme(jnp.maximum, jnp.max, jnp.int32(KEY_NEG_INF)) + 1

    def bisect(state):
        it, lo, hi, n_lo, _ = state
        mid = (lo >> 1) + (hi >> 1) + (lo & hi & 1)
        n_mid = count(lambda k: k >= mid[None])
        take = n_mid >= topk
        lo_new = jnp.where(take, mid, lo)
        n_new = jnp.where(take, n_mid, n_lo)
        active = (n_new > topk) & (mid != lo)
        return it + 1, lo_new, jnp.where(take, hi, mid), n_new, jnp.max(active.astype(F32))

    def searching(state):
        return (state[0] < 34) & (state[4] > 0)

    n0 = count(lambda k: k >= lo0[None])
    _, thr, _, _, _ = lax.while_loop(searching, bisect,
                                     (jnp.int32(0), lo0, hi0, n0, jnp.max((n0 > topk).astype(F32))))
    need = topk - count(lambda k: k > thr[None])
    ties = count(lambda k: k == thr[None])
    excess = jnp.max(jnp.where((ties > need) & (thr > KEY_NEG_INF), 1.0, 0.0)) > 0.0

    @pl.when(jnp.logical_not(excess))
    def _():
        def fill(c, _):
            k = _by_sublane_tile(key_ref[c])
            bias_ref[c] = jnp.where((k >= thr[None]) & (k > KEY_NEG_INF), 0.0, NEG_INF).reshape(ck, tq)
            return 0
        lax.fori_loop(0, n_chunks, fill, 0)

    @pl.when(excess)
    def _():
        rr = lax.broadcasted_iota(I32, (ck, ck), 0)
        cc = lax.broadcasted_iota(I32, (ck, ck), 1)
        strict_lower = (cc < rr).astype(BF16)

        def fill(c, before):
            k = _by_sublane_tile(key_ref[c])
            tie = k == thr[None]
            tie_f = tie.astype(F32)
            earlier = jnp.dot(strict_lower, tie_f.reshape(ck, tq).astype(BF16), preferred_element_type=F32)
            earlier = _by_sublane_tile(earlier) + before[None]
            sel = (k > thr[None]) | (tie & (earlier < need[None]))
            bias_ref[c] = jnp.where(sel & (k > KEY_NEG_INF), 0.0, NEG_INF).reshape(ck, tq)
            return before + _over_sublanes(jnp.sum(tie_f, axis=0), jnp.sum)
        lax.fori_loop(0, n_chunks, fill, jnp.zeros((SUBLANES, tq), F32))

    _stage_pair_masked_q(qt_ref, qpm_ref, H_DSA)
    _flash_init(m_ref, l_ref, acc_ref)

    def kv_step(c, carry):
        start = pl.multiple_of(c * ck, ck)
        def scores(h):
            pair = h // 2
            return jnp.dot(k_ref[pl.ds(start, ck), pair * LANES:(pair + 1) * LANES], qpm_ref[h],
                           preferred_element_type=F32) + bias_ref[c]

        _flash_step(H_DSA, scores, lambda h: vt_ref[c, h * HEAD_DIM:(h + 1) * HEAD_DIM, :], st)
        return carry

    lax.fori_loop(0, n_chunks, kv_step, 0)
    _flash_finish(o_ref, l_ref, acc_ref, H_DSA)


def _dsa_prompt(iqt, iwt, ikb, dqt, dkb, dvtc, batch, seq, tq):
    ck = dvtc.shape[3]
    nq = seq // tq
    topk = min(TOPK_MAX, seq // 4)
    return pl.pallas_call(
        functools.partial(_dsa_kernel, tq=tq, ck=ck, topk=topk),
        grid=(batch, nq),
        in_specs=[pl.BlockSpec((None, H_IDX * D_IDX, tq), lambda b, i: (b, 0, i)),
                  pl.BlockSpec((None, H_IDX, tq), lambda b, i: (b, 0, i)),
                  pl.BlockSpec((seq, D_IDX), lambda b, i: (b, 0)),
                  pl.BlockSpec((None, W_DSA, tq), lambda b, i: (b, 0, i)),
                  pl.BlockSpec((seq, W_DSA), lambda b, i: (b, 0)),
                  pl.BlockSpec((None, seq // ck, W_DSA, ck), lambda b, i: (b, 0, 0, 0))],
        out_specs=pl.BlockSpec((tq, W_DSA), lambda b, i: (b * nq + i, 0)),
        out_shape=jax.ShapeDtypeStruct((batch * seq, W_DSA), BF16),
        scratch_shapes=[pltpu.VMEM((seq // ck, ck, tq), I32), pltpu.VMEM((seq // ck, ck, tq), F32)]
        + _flash_scratch(H_DSA, ck, tq),
        compiler_params=_cparams(("arbitrary", "arbitrary")),
        name="dsa_prompt",
    )(iqt, iwt, ikb, dqt, dkb, dvtc)


KV_PAGES_PER_STEP = 16
IDX_PAGES_PER_STEP = 32


def _head_mask(rows, width):
    r = lax.broadcasted_iota(I32, (rows, width), 0)
    c = lax.broadcasted_iota(I32, (rows, width), 1)
    return (c // HEAD_DIM) == (r % H_FOX)


def _block_diag_q(q, n_q):
    hm = _head_mask(H_FOX, q.shape[1])
    qf = q.astype(F32)
    parts = [jnp.where(hm, jnp.broadcast_to(qf[t:t + 1, :], hm.shape), 0.0) for t in range(n_q)]
    return jnp.concatenate(parts, axis=0).astype(BF16)


def _rows_to_groups(x, n_q):
    return jnp.concatenate([jnp.broadcast_to(x[t:t + 1, :], (H_FOX, x.shape[1])) for t in range(n_q)], axis=0)


DECODE_SPLITS = 2


def _softmax_page(g, s, vt_bf16, m_ref, l_ref, acc_ref):
    m = m_ref[g]
    m_new = jnp.maximum(m, jnp.max(s, axis=1, keepdims=True))
    alpha = jnp.exp(m - m_new)
    p = jnp.exp(s - m_new)
    l_ref[g] = alpha * l_ref[g] + jnp.sum(p, axis=1, keepdims=True)
    acc_ref[g] = alpha * acc_ref[g] + lax.dot_general(p.astype(BF16), vt_bf16, NT_DIMS,
                                                      preferred_element_type=F32)
    m_ref[g] = m_new


def _decode_scratch(rows, w):
    g = DECODE_SPLITS
    return [pltpu.VMEM((g, rows, 1), F32), pltpu.VMEM((g, rows, 1), F32), pltpu.VMEM((g, rows, w), F32)]


def _stack_pages(refs):
    return jnp.concatenate([r[...].reshape(-1, PAGE_SIZE) for r in refs], axis=1).astype(BF16)


def _finish_decode(o_ref, m_ref, l_ref, acc_ref, n_q):
    m = m_ref[0]
    for g in range(1, DECODE_SPLITS):
        m = jnp.maximum(m, m_ref[g])
    l = jnp.zeros_like(m)
    acc = jnp.zeros(acc_ref.shape[1:], F32)
    for g in range(DECODE_SPLITS):
        w = jnp.exp(m_ref[g] - m)
        l = l + w * l_ref[g]
        acc = acc + w * acc_ref[g]
    o = acc / l
    o = jnp.where(_head_mask(o.shape[0], o.shape[1]), o, 0.0)
    o_ref[...] = jnp.sum(o.reshape(n_q, H_FOX, o.shape[1]), axis=1).astype(BF16)


def _sample_fox_kernel(pt_ref, q_ref, kn_ref, vn_ref, lfn_ref, *rest, n_q, pps):
    k_refs, v_refs, lf_refs = rest[:pps], rest[pps:2 * pps], rest[2 * pps:3 * pps]
    o_ref, m_ref, l_ref, acc_ref, cc_ref, qbd_ref = rest[3 * pps:]
    step = pl.program_id(1)
    rows = n_q * H_FOX

    @pl.when(step == 0)
    def _():
        m_ref[...] = jnp.full(m_ref.shape, -1e30, F32)
        l_ref[...] = jnp.zeros(l_ref.shape, F32)
        acc_ref[...] = jnp.zeros(acc_ref.shape, F32)
        cc_ref[...] = jnp.zeros(cc_ref.shape, F32)
        qbd_ref[...] = _block_diag_q(q_ref[...], n_q)

    upper = _upper_tri(PAGE_SIZE)

    within = _lane_cumsum(jnp.concatenate([r[...] for r in lf_refs], axis=0), upper)
    off = cc_ref[...]
    parts = []
    for p in range(pps):
        cp = within[p * H_FOX:(p + 1) * H_FOX] + off
        parts.append(cp)
        off = cp[:, PAGE_SIZE - 1:PAGE_SIZE]
    cc_ref[...] = off
    per = pps // DECODE_SPLITS
    for g in range(DECODE_SPLITS):
        pages = slice(g * per, (g + 1) * per)
        cbias = jnp.concatenate(parts[pages], axis=1)
        s = jnp.dot(qbd_ref[...], _stack_pages(k_refs[pages]), preferred_element_type=F32)
        s = s - jnp.concatenate([cbias] * n_q, axis=0)
        _softmax_page(g, s, _stack_pages(v_refs[pages]), m_ref, l_ref, acc_ref)

    @pl.when(step == pl.num_programs(1) - 1)
    def _():
        r = lax.broadcasted_iota(I32, (rows, PAGE_SIZE), 0)
        c = lax.broadcasted_iota(I32, (rows, PAGE_SIZE), 1)
        cn = _lane_cumsum(lfn_ref[...], upper) + cc_ref[...]
        sn = jnp.dot(qbd_ref[...], kn_ref[...].astype(BF16), preferred_element_type=F32)
        sn = sn - jnp.concatenate([cn] * n_q, axis=0)
        sn = jnp.where(c <= r // H_FOX, sn, NEG_INF)
        _softmax_page(0, sn, vn_ref[...].astype(BF16), m_ref, l_ref, acc_ref)
        _finish_decode(o_ref, m_ref, l_ref, acc_ref, n_q)


def _page_spec(block, p, pps):
    nd = len(block)
    return pl.BlockSpec((None,) + block, lambda b, s, pt: (pt[b, s * pps + p],) + (0,) * nd)


def _sample_fox(page_table, q, k_new, v_new, lft_new, pool_k, pool_v, pool_lft):
    dec_b, n_q, w = q.shape
    n_pages = page_table.shape[1]
    pps = min(KV_PAGES_PER_STEP, n_pages)
    rows = n_q * H_FOX
    seq_spec = lambda blk: pl.BlockSpec((None,) + blk, lambda b, s, pt: (b,) + (0,) * len(blk))
    kv_page = (H_FOX, HEAD_DIM, PAGE_SIZE)
    in_specs = ([seq_spec((n_q, w)), seq_spec((w, PAGE_SIZE)), seq_spec((w, PAGE_SIZE)),
                 seq_spec((H_FOX, PAGE_SIZE))]
                + [_page_spec(kv_page, p, pps) for p in range(pps)]
                + [_page_spec(kv_page, p, pps) for p in range(pps)]
                + [_page_spec((H_FOX, PAGE_SIZE), p, pps) for p in range(pps)])
    return pl.pallas_call(
        functools.partial(_sample_fox_kernel, n_q=n_q, pps=pps),
        grid_spec=pltpu.PrefetchScalarGridSpec(
            num_scalar_prefetch=1,
            grid=(dec_b, n_pages // pps),
            in_specs=in_specs,
            out_specs=pl.BlockSpec((None, n_q, w), lambda b, s, pt: (b, 0, 0)),
            scratch_shapes=_decode_scratch(rows, w) + [pltpu.VMEM((H_FOX, 1), F32),
                                                       pltpu.VMEM((rows, w), BF16)]),
        out_shape=jax.ShapeDtypeStruct((dec_b, n_q, w), BF16),
        compiler_params=_cparams(("arbitrary", "arbitrary")),
        name="sample_fox",
    )(page_table, q, k_new, v_new, lft_new, *([pool_k] * pps), *([pool_v] * pps), *([pool_lft] * pps))


def _sample_index_kernel(pt_ref, iq_ref, iw_ref, kn_ref, *rest, pps, n_q, topk):
    k_refs = rest[:pps]
    bias_ref, biasn_ref, key_ref, keyn_ref = rest[pps:]
    step = pl.program_id(1)
    n_steps = key_ref.shape[0]
    iq = iq_ref[...]
    iw = iw_ref[...]

    def scores(kt):
        rel = jnp.dot(iq, kt, preferred_element_type=F32)
        w = iw * jnp.maximum(rel, 0.0)
        by_parity = jnp.sum(w.reshape(H_IDX // 2, SUBLANES, kt.shape[1]), axis=0)
        return by_parity + pltpu.roll(by_parity, SUBLANES // 2, 0)

    key_ref[step] = _sort_key(scores(_stack_pages(k_refs)))

    @pl.when(step == n_steps - 1)
    def _():
        r = lax.broadcasted_iota(I32, (H_FOX, PAGE_SIZE), 0)
        c = lax.broadcasted_iota(I32, (H_FOX, PAGE_SIZE), 1)
        keyn_ref[...] = _sort_key(jnp.where((c <= r) & (r < n_q), scores(kn_ref[...].astype(BF16)), NEG_INF))

        def count(pred):
            past = jnp.sum(jnp.sum(pred(key_ref[...]).astype(F32), axis=0), axis=1, keepdims=True)
            return past + jnp.sum(pred(keyn_ref[...]).astype(F32), axis=1, keepdims=True)

        def bit_step(bi, thr):
            cand = thr + lax.shift_left(jnp.int32(1), 31 - bi)
            return jnp.where(count(lambda k: k >= cand) >= topk, cand, thr)

        thr = lax.fori_loop(0, 32, bit_step, jnp.full((H_FOX, 1), INT_MIN, I32))
        need = topk - count(lambda k: k > thr)
        ties = count(lambda k: k == thr)
        excess = jnp.max(jnp.where((ties > need) & (thr > KEY_NEG_INF), 1.0, 0.0)) > 0.0

        @pl.when(jnp.logical_not(excess))
        def _():
            k = key_ref[...]
            bias_ref[...] = jnp.where((k >= thr) & (k > KEY_NEG_INF), 0.0, NEG_INF)
            k = keyn_ref[...]
            biasn_ref[...] = jnp.where((k >= thr) & (k > KEY_NEG_INF), 0.0, NEG_INF)

        @pl.when(excess)
        def _():
            upper = _upper_tri(PAGE_SIZE)

            def tie_block(k, before):
                tie = k == thr
                incl = jnp.dot(tie.astype(BF16), upper, preferred_element_type=F32) + before
                sel = (k > thr) | (tie & (incl - tie.astype(F32) < need))
                return jnp.where(sel & (k > KEY_NEG_INF), 0.0, NEG_INF), incl[:, PAGE_SIZE - 1:PAGE_SIZE]

            def fill(c, before):
                for p in range(pps):
                    lanes = slice(p * PAGE_SIZE, (p + 1) * PAGE_SIZE)
                    b, before = tie_block(key_ref[c, :, lanes], before)
                    bias_ref[c, :, lanes] = b
                return before

            before = lax.fori_loop(0, n_steps, fill, jnp.zeros((H_FOX, 1), F32))
            biasn_ref[...] = tie_block(keyn_ref[...], before)[0]


def _sample_index(page_table, iqx, iwx, kidx_new, pool_kidx, n_q):
    dec_b = iqx.shape[0]
    n_pages = page_table.shape[1]
    pps = min(IDX_PAGES_PER_STEP, n_pages)
    n_steps = n_pages // pps
    step_w = pps * PAGE_SIZE
    topk = min(TOPK_MAX, (n_pages * PAGE_SIZE + n_q) // 4)
    seq_spec = lambda blk: pl.BlockSpec((None,) + blk, lambda b, s, pt: (b,) + (0,) * len(blk))
    iq_rows = iqx.shape[1]
    in_specs = ([seq_spec((iq_rows, D_IDX)), seq_spec((iq_rows, 1)), seq_spec((D_IDX, PAGE_SIZE))]
                + [_page_spec((D_IDX, PAGE_SIZE), p, pps) for p in range(pps)])
    return pl.pallas_call(
        functools.partial(_sample_index_kernel, pps=pps, n_q=n_q, topk=topk),
        grid_spec=pltpu.PrefetchScalarGridSpec(
            num_scalar_prefetch=1,
            grid=(dec_b, n_steps),
            in_specs=in_specs,
            out_specs=[pl.BlockSpec((None, n_steps, H_FOX, step_w), lambda b, s, pt: (b, 0, 0, 0)),
                       pl.BlockSpec((None, H_FOX, PAGE_SIZE), lambda b, s, pt: (b, 0, 0))],
            scratch_shapes=[pltpu.VMEM((n_steps, H_FOX, step_w), I32), pltpu.VMEM((H_FOX, PAGE_SIZE), I32)]),
        out_shape=[jax.ShapeDtypeStruct((dec_b, n_steps, H_FOX, step_w), F32),
                   jax.ShapeDtypeStruct((dec_b, H_FOX, PAGE_SIZE), F32)],
        compiler_params=_cparams(("arbitrary", "arbitrary")),
        name="sample_index",
    )(page_table, iqx, iwx, kidx_new, *([pool_kidx] * pps))


def _sample_dsa_kernel(pt_ref, q_ref, kn_ref, vn_ref, bias_ref, biasn_ref, *rest, n_q, pps):
    k_refs, v_refs = rest[:pps], rest[pps:2 * pps]
    o_ref, m_ref, l_ref, acc_ref, qbd_ref = rest[2 * pps:]
    step = pl.program_id(1)

    @pl.when(step == 0)
    def _():
        m_ref[...] = jnp.full(m_ref.shape, -1e30, F32)
        l_ref[...] = jnp.zeros(l_ref.shape, F32)
        acc_ref[...] = jnp.zeros(acc_ref.shape, F32)
        qbd_ref[...] = _block_diag_q(q_ref[...], n_q)

    def attend(g, kt, vt, bias):
        s = jnp.dot(qbd_ref[...], kt, preferred_element_type=F32) + _rows_to_groups(bias, n_q)
        _softmax_page(g, s, vt, m_ref, l_ref, acc_ref)

    per = pps // DECODE_SPLITS
    for g in range(DECODE_SPLITS):
        pages = slice(g * per, (g + 1) * per)
        attend(g, _stack_pages(k_refs[pages]), _stack_pages(v_refs[pages]),
               bias_ref[0, :, g * per * PAGE_SIZE:(g + 1) * per * PAGE_SIZE])

    @pl.when(step == pl.num_programs(1) - 1)
    def _():
        attend(0, kn_ref[...].astype(BF16), vn_ref[...].astype(BF16), biasn_ref[...])
        _finish_decode(o_ref, m_ref, l_ref, acc_ref, n_q)


def _sample_dsa(page_table, q, k_new, v_new, bias, bias_new, pool_k, pool_v):
    dec_b, n_q, w = q.shape
    n_pages = page_table.shape[1]
    pps = min(KV_PAGES_PER_STEP, n_pages)
    rows = n_q * H_DSA
    kv_page = (H_DSA, HEAD_DIM, PAGE_SIZE)
    per_bias_chunk = bias.shape[3] // (pps * PAGE_SIZE)
    seq_spec = lambda blk: pl.BlockSpec((None,) + blk, lambda b, s, pt: (b,) + (0,) * len(blk))
    in_specs = ([seq_spec((n_q, w)), seq_spec((w, PAGE_SIZE)), seq_spec((w, PAGE_SIZE)),
                 pl.BlockSpec((None, 1, H_FOX, pps * PAGE_SIZE),
                              lambda b, s, pt: (b, s // per_bias_chunk, 0, s % per_bias_chunk)),
                 seq_spec((H_FOX, PAGE_SIZE))]
                + [_page_spec(kv_page, p, pps) for p in range(pps)]
                + [_page_spec(kv_page, p, pps) for p in range(pps)])
    return pl.pallas_call(
        functools.partial(_sample_dsa_kernel, n_q=n_q, pps=pps),
        grid_spec=pltpu.PrefetchScalarGridSpec(
            num_scalar_prefetch=1,
            grid=(dec_b, n_pages // pps),
            in_specs=in_specs,
            out_specs=pl.BlockSpec((None, n_q, w), lambda b, s, pt: (b, 0, 0)),
            scratch_shapes=_decode_scratch(rows, w) + [pltpu.VMEM((rows, w), BF16)]),
        out_shape=jax.ShapeDtypeStruct((dec_b, n_q, w), BF16),
        compiler_params=_cparams(("arbitrary", "arbitrary")),
        name="sample_dsa",
    )(page_table, q, k_new, v_new, bias, bias_new, *([pool_k] * pps), *([pool_v] * pps))


def _layer_norm(r, g, b):
    mu = jnp.mean(r, axis=1, keepdims=True)
    d = r - mu
    var = jnp.mean(d * d, axis=1, keepdims=True)
    return d * lax.rsqrt(var + LN_EPS) * g + b


def _tail_kernel(of_ref, od_ref, x_ref, wo_ref, g_ref, b_ref, wr_ref, br_ref, cin_ref,
                 x1_ref, e_ref, gate_ref, rank_ref, cnt_ref, carry_ref, *, alpha):
    i = pl.program_id(0)
    tm = x_ref.shape[0]

    @pl.when(i == 0)
    def _():
        carry_ref[...] = cin_ref[...]

    r = (alpha * x_ref[...]
         + jnp.dot(of_ref[...], wo_ref[0:W_FOX, :], preferred_element_type=F32)
         + jnp.dot(od_ref[...], wo_ref[W_FOX:W_FOX + W_DSA, :], preferred_element_type=F32))
    x1 = _layer_norm(r, g_ref[...], b_ref[...])
    x1_ref[...] = x1

    logits = jnp.dot(x1.astype(BF16), wr_ref[...], preferred_element_type=F32) + br_ref[...]
    lane = lax.broadcasted_iota(I32, (tm, LANES), 1)
    lane_f = lane.astype(F32)
    vals, idxs, hots = [], [], []
    lg = logits
    for _ in range(TOP_K_EXPERTS):
        m = jnp.max(lg, axis=1, keepdims=True)
        idx = jnp.min(jnp.where(lg == m, lane_f, float(LANES)), axis=1, keepdims=True)
        hot = lane_f == idx
        vals.append(m)
        idxs.append(idx)
        hots.append(hot)
        lg = jnp.where(hot, NEG_INF, lg)
    exps = [jnp.exp(v - vals[0]) for v in vals]
    den = exps[0]
    for e in exps[1:]:
        den = den + e
    sel = jnp.zeros((tm, LANES), F32)
    for hot in hots:
        sel = sel + hot.astype(F32)
    rr = lax.broadcasted_iota(I32, (tm, tm), 0)
    cc = lax.broadcasted_iota(I32, (tm, tm), 1)
    strict_lower = (cc < rr).astype(BF16)
    before = jnp.dot(strict_lower, sel.astype(BF16), preferred_element_type=F32) + carry_ref[...]
    e_out = jnp.zeros((tm, LANES), F32)
    g_out = jnp.zeros((tm, LANES), F32)
    r_out = jnp.zeros((tm, LANES), F32)
    for j in range(TOP_K_EXPERTS):
        rank = jnp.sum(jnp.where(hots[j], before, 0.0), axis=1, keepdims=True)
        e_out = jnp.where(lane == j, idxs[j], e_out)
        g_out = jnp.where(lane == j, exps[j] / den, g_out)
        r_out = jnp.where(lane == j, rank, r_out)
    e_ref[...] = e_out.astype(I32)
    gate_ref[...] = g_out
    rank_ref[...] = r_out.astype(I32)
    carry_ref[...] = carry_ref[...] + jnp.sum(sel, axis=0, keepdims=True)
    cnt_ref[...] = carry_ref[...]


def _tail(of, od, x2d, wo, g1, b1, wr, br, cin, alpha, tm):
    rows, d = x2d.shape
    row_spec = lambda w: pl.BlockSpec((tm, w), lambda i: (i, 0))
    full = lambda a: pl.BlockSpec(a.shape, lambda i: (0,) * a.ndim)
    return pl.pallas_call(
        functools.partial(_tail_kernel, alpha=alpha),
        grid=(rows // tm,),
        in_specs=[row_spec(W_FOX), row_spec(W_DSA), row_spec(d), full(wo), full(g1), full(b1),
                  full(wr), full(br), full(cin)],
        out_specs=[row_spec(d), row_spec(LANES), row_spec(LANES), row_spec(LANES),
                   pl.BlockSpec((1, LANES), lambda i: (0, 0))],
        out_shape=[jax.ShapeDtypeStruct((rows, d), F32), jax.ShapeDtypeStruct((rows, LANES), I32),
                   jax.ShapeDtypeStruct((rows, LANES), F32), jax.ShapeDtypeStruct((rows, LANES), I32),
                   jax.ShapeDtypeStruct((1, LANES), F32)],
        scratch_shapes=[pltpu.VMEM((1, LANES), F32)],
        compiler_params=_cparams(("arbitrary",)),
        name="tail_router",
    )(of, od, x2d, wo, g1, b1, wr, br, cin)


EXPERT_BLOCK = 256
DISPATCH_TOKENS = 128
COMBINE_UNROLL = 4


def _dispatch_kernel(dest_ref, x_ref, xs_in, xs_out, sem, *, td):
    del xs_in

    def row_copy(src_row, dst_row):
        return pltpu.make_async_copy(x_ref.at[pl.ds(src_row, 1), :], xs_out.at[pl.ds(dst_row, 1), :], sem)

    def issue(tt, c):
        for u in range(COMBINE_UNROLL):
            t = tt * COMBINE_UNROLL + u
            for j in range(TOP_K_EXPERTS):
                row_copy(t, dest_ref[t * TOP_K_EXPERTS + j]).start()
        return c

    lax.fori_loop(0, td // COMBINE_UNROLL, issue, 0)

    def drain(tt, c):
        for _ in range(COMBINE_UNROLL * TOP_K_EXPERTS):
            row_copy(0, 0).wait()
        return c

    lax.fori_loop(0, td // COMBINE_UNROLL, drain, 0)


def _dispatch(dest_flat, x1, xs):
    rows, d = x1.shape
    td = DISPATCH_TOKENS
    return pl.pallas_call(
        functools.partial(_dispatch_kernel, td=td),
        grid=(rows // td,),
        in_specs=[pl.BlockSpec((td * TOP_K_EXPERTS,), lambda i: (i,), memory_space=pltpu.SMEM),
                  pl.BlockSpec((td, d), lambda i: (i, 0)), pl.BlockSpec(memory_space=pl.ANY)],
        out_specs=pl.BlockSpec(memory_space=pl.ANY),
        out_shape=jax.ShapeDtypeStruct(xs.shape, xs.dtype),
        scratch_shapes=[pltpu.SemaphoreType.DMA(())],
        input_output_aliases={2: 0},
        compiler_params=_cparams(("arbitrary",)),
        name="moe_dispatch",
    )(dest_flat, x1, xs)


def _ffn_kernel(be_ref, first_ref, nused_ref, xs_ref, wg_ref, wu_ref, wd_ref, bg_ref, bu_ref, bd_ref,
                ys_ref, wgb_ref, wub_ref, wdb_ref):
    b = pl.program_id(0)

    @pl.when(first_ref[b] == 1)
    def _():
        wgb_ref[...] = wg_ref[...].astype(BF16)
        wub_ref[...] = wu_ref[...].astype(BF16)
        wdb_ref[...] = wd_ref[...].astype(BF16)

    @pl.when(b < nused_ref[0])
    def _():
        xb = xs_ref[...].astype(BF16)
        g = jnp.dot(xb, wgb_ref[...], preferred_element_type=F32) + bg_ref[...]
        u = jnp.dot(xb, wub_ref[...], preferred_element_type=F32) + bu_ref[...]
        g = jnp.minimum(g, SWIGLU_LIMIT)
        u = jnp.clip(u, -SWIGLU_LIMIT, SWIGLU_LIMIT)
        h = g * (1.0 / (1.0 + jnp.exp(-SWIGLU_ALPHA * g))) * (u + 1.0)
        ys_ref[...] = jnp.dot(h.astype(BF16), wdb_ref[...], preferred_element_type=F32) + bd_ref[...]

    @pl.when(b >= nused_ref[0])
    def _():
        ys_ref[...] = jnp.zeros(ys_ref.shape, F32)


def _expert_ffn(block_e, is_first, n_used, xs, wg, wu, wd, bg, bu, bd):
    p, d = xs.shape
    bs = EXPERT_BLOCK
    de = wg.shape[2]
    w_spec = lambda shape: pl.BlockSpec((None,) + shape, lambda b, be, fi, nu: (be[b], 0, 0))
    return pl.pallas_call(
        _ffn_kernel,
        grid_spec=pltpu.PrefetchScalarGridSpec(
            num_scalar_prefetch=3,
            grid=(p // bs,),
            in_specs=[pl.BlockSpec((bs, d), lambda b, be, fi, nu: (b, 0)),
                      w_spec((d, de)), w_spec((d, de)), w_spec((de, d)),
                      w_spec((1, de)), w_spec((1, de)), w_spec((1, d))],
            out_specs=pl.BlockSpec((bs, d), lambda b, be, fi, nu: (b, 0)),
            scratch_shapes=[pltpu.VMEM((d, de), BF16), pltpu.VMEM((d, de), BF16), pltpu.VMEM((de, d), BF16)]),
        out_shape=jax.ShapeDtypeStruct((p, d), F32),
        compiler_params=_cparams(("arbitrary",)),
        name="moe_ffn",
    )(block_e, is_first, n_used, xs, wg, wu, wd, bg, bu, bd)


def _combine_kernel(dest_ref, dest_next_ref, ys_hbm, gate_ref, x1_ref, g_ref, b_ref, o_ref, buf_ref, sem,
                    *, tc, alpha):
    i = pl.program_id(0)
    slot = i % 2

    def row_copy(src_row, sl, j, t):
        return pltpu.make_async_copy(ys_hbm.at[pl.ds(src_row, 1), :], buf_ref.at[sl, j, pl.ds(t, 1), :],
                                     sem.at[sl])

    def issue(d_ref, sl):
        def body(tt, c):
            for u in range(COMBINE_UNROLL):
                t = tt * COMBINE_UNROLL + u
                for j in range(TOP_K_EXPERTS):
                    row_copy(d_ref[t * TOP_K_EXPERTS + j], sl, j, t).start()
            return c
        lax.fori_loop(0, tc // COMBINE_UNROLL, body, 0)

    @pl.when(i == 0)
    def _():
        issue(dest_ref, slot)

    @pl.when(i + 1 < pl.num_programs(0))
    def _():
        issue(dest_next_ref, 1 - slot)

    def drain(tt, c):
        for _ in range(COMBINE_UNROLL * TOP_K_EXPERTS):
            row_copy(0, slot, 0, 0).wait()
        return c

    lax.fori_loop(0, tc // COMBINE_UNROLL, drain, 0)

    gates = gate_ref[...]
    moe = gates[:, 0:1] * buf_ref[slot, 0]
    for j in range(1, TOP_K_EXPERTS):
        moe = moe + gates[:, j:j + 1] * buf_ref[slot, j]
    o_ref[...] = _layer_norm(alpha * x1_ref[...] + moe, g_ref[...], b_ref[...])


def _combine(dest_flat, ys, gates, x1, g2, b2, alpha):
    rows, d = x1.shape
    tc = DISPATCH_TOKENS
    last = rows // tc - 1
    return pl.pallas_call(
        functools.partial(_combine_kernel, tc=tc, alpha=alpha),
        grid=(rows // tc,),
        in_specs=[pl.BlockSpec((tc * TOP_K_EXPERTS,), lambda i: (i,), memory_space=pltpu.SMEM),
                  pl.BlockSpec((tc * TOP_K_EXPERTS,), lambda i: (jnp.minimum(i + 1, last),),
                               memory_space=pltpu.SMEM),
                  pl.BlockSpec(memory_space=pl.ANY),
                  pl.BlockSpec((tc, LANES), lambda i: (i, 0)),
                  pl.BlockSpec((tc, d), lambda i: (i, 0)),
                  pl.BlockSpec((1, d), lambda i: (0, 0)),
                  pl.BlockSpec((1, d), lambda i: (0, 0))],
        out_specs=pl.BlockSpec((tc, d), lambda i: (i, 0)),
        out_shape=jax.ShapeDtypeStruct((rows, d), F32),
        scratch_shapes=[pltpu.VMEM((2, TOP_K_EXPERTS, tc, d), F32), pltpu.SemaphoreType.DMA((2,))],
        compiler_params=_cparams(("arbitrary",)),
        name="moe_combine",
    )(dest_flat, dest_flat, ys, gates, x1, g2, b2)


def _prep_w_in(w, b_f):
    d = w.shape[0]
    sizes = (W_FOX, W_FOX, W_FOX, H_FOX, W_DSA, W_DSA, W_DSA, H_IDX * D_IDX, D_IDX, H_IDX)
    parts, acc = [], 0
    for s in sizes:
        parts.append(w[:, acc:acc + s])
        acc += s
    fq, fk, fv, fl, dq, dk, dv, iq, ik, iw = parts
    small = jnp.concatenate([ik, fl, iw, jnp.zeros((d, LANES - D_IDX - H_FOX - H_IDX), w.dtype)], axis=1)
    wp = jnp.concatenate([fq, fk, fv, dq, dk, dv, iq, small], axis=1).astype(BF16)
    bf_row = jnp.zeros((1, LANES), F32).at[0, SM_LF:SM_LF + H_FOX].set(b_f.astype(F32))
    return wp, bf_row


def _rope_tables(pos):
    half = HEAD_DIM // 2
    inv = ROPE_THETA ** (-jnp.arange(half, dtype=F32) / half)
    ang = pos.astype(F32)[:, None] * inv[None, :]
    cos, sin = jnp.cos(ang), jnp.sin(ang)
    cos_t = jnp.concatenate([cos, cos, cos, cos], axis=1)
    sin_t = jnp.concatenate([-sin, sin, -sin, sin], axis=1)
    return cos_t, sin_t


def _pad_rows(a, n):
    return jnp.pad(a, ((0, 0), (0, n - a.shape[1]), (0, 0)))


def kernel(x_prompt, x_sample, cache_fox_k, cache_fox_v, cache_fox_logf, cache_dsa_k, cache_dsa_v,
           cache_dsa_kidx, page_table, w_in, b_f, w_o, ln1_g, ln1_b, w_router, b_router, w_gate, b_gate,
           w_up, b_up, w_down, b_down, ln2_g, ln2_b):
    batch, seq, d = x_prompt.shape
    dec_b, n_q, _ = x_sample.shape
    depth = w_in.shape[0]
    n_pool = cache_fox_k.shape[1]
    n_pages = page_table.shape[1]
    past = n_pages * PAGE_SIZE
    alpha = (2.0 * depth) ** 0.25
    rows_p, rows_s = batch * seq, dec_b * n_q

    cos_p, sin_p = _rope_tables(jnp.arange(seq))
    cos_s, sin_s = _rope_tables(past + (jnp.arange(rows_s) % n_q))

    xp = x_prompt.reshape(rows_p, d)
    xs_ = x_sample.reshape(rows_s, d)
    outs_p = [[] for _ in range(6)]
    outs_s = [[] for _ in range(6)]
    for l in range(depth):
        wp, bf_row = _prep_w_in(w_in[l], b_f[l])
        wo = w_o[l].astype(BF16)
        wr = jnp.pad(w_router[l], ((0, 0), (0, LANES - N_EXPERTS))).astype(BF16)
        br = jnp.full((1, LANES), NEG_INF, F32).at[0, :N_EXPERTS].set(b_router[l].astype(F32))
        g1, b1 = ln1_g[l].reshape(1, d), ln1_b[l].reshape(1, d)
        g2, b2 = ln2_g[l].reshape(1, d), ln2_b[l].reshape(1, d)

        pp = _in_proj(xp, wp, bf_row, cos_p, sin_p, batch, seq, tm=256)
        o_fox = _fox_prompt(pp["fqt"], pp["fkb"], pp["fvtc"], pp["cb"], batch, seq)
        o_dsa = _dsa_prompt(pp["iqt"], pp["iwt"], pp["ikb"], pp["dqt"], pp["dkb"], pp["dvtc"], batch, seq,
                            tq=256)

        ps = _in_proj(xs_, wp, bf_row, cos_s, sin_s, 1, rows_s, tm=rows_s)
        ps = {k: v[0] for k, v in ps.items() if v.ndim == 3}
        pool_fk = cache_fox_k[l].transpose(0, 2, 3, 1)
        pool_fv = cache_fox_v[l].transpose(0, 2, 3, 1)
        pool_lft = jnp.swapaxes(cache_fox_logf[l], 1, 2)
        pool_dk = cache_dsa_k[l].transpose(0, 2, 3, 1)
        pool_dv = cache_dsa_v[l].transpose(0, 2, 3, 1)
        pool_ki = jnp.swapaxes(cache_dsa_kidx[l], 1, 2)
        per_seq = lambda a: a.reshape(a.shape[0], dec_b, n_q).transpose(1, 0, 2)
        new_t = lambda a: jnp.pad(per_seq(a), ((0, 0), (0, 0), (0, PAGE_SIZE - n_q)))
        rows_q = lambda a: a.T.reshape(dec_b, n_q, a.shape[0])
        so_fox = _sample_fox(page_table, rows_q(ps["fqt"]), new_t(ps["fkt"]), new_t(ps["fvt"]),
                             new_t(ps["lft"]), pool_fk, pool_fv, pool_lft)
        half = SUBLANES // 2
        assert n_q <= half
        iqx = ps["iqt"].reshape(H_IDX, D_IDX, dec_b, n_q).transpose(2, 0, 3, 1)
        iqx = jnp.pad(iqx, ((0, 0), (0, 0), (0, half - n_q), (0, 0))).reshape(dec_b, H_IDX * half, D_IDX)
        iwx = jnp.pad(per_seq(ps["iwt"]), ((0, 0), (0, 0), (0, half - n_q)))
        iwx = iwx.reshape(dec_b, H_IDX * half, 1)
        sel_bias, sel_bias_new = _sample_index(page_table, iqx, iwx, new_t(ps["ikt"]), pool_ki, n_q)
        so_dsa = _sample_dsa(page_table, rows_q(ps["dqt"]), new_t(ps["dkt"]), new_t(ps["dvt"]), sel_bias,
                             sel_bias_new, pool_dk, pool_dv)

        cin = jnp.zeros((1, LANES), F32)
        x1_p, e_p, gate_p, rank_p, cnt_p = _tail(o_fox, o_dsa, xp, wo, g1, b1, wr, br, cin, alpha, tm=256)
        x1_s, e_s, gate_s, rank_s, cnt = _tail(so_fox.reshape(rows_s, W_FOX), so_dsa.reshape(rows_s, W_DSA),
                                               xs_, wo, g1, b1, wr, br, cnt_p, alpha, tm=rows_s)

        bs = EXPERT_BLOCK
        counts = cnt[0, :N_EXPERTS].astype(I32)
        padded = (counts + bs - 1) // bs * bs
        pad_end = jnp.cumsum(padded)
        pad_start = pad_end - padded
        n_assign = (rows_p + rows_s) * TOP_K_EXPERTS
        n_blocks = -(-(n_assign + N_EXPERTS * (bs - 1)) // bs)
        blk = jnp.arange(n_blocks, dtype=I32)
        n_used = (pad_end[-1] // bs).astype(I32)
        block_e = jnp.sum((pad_end[None, :] <= (blk * bs)[:, None]).astype(I32), axis=1)
        block_e = jnp.minimum(block_e, N_EXPERTS - 1)
        block_e = jnp.where(blk < n_used, block_e, block_e[jnp.maximum(n_used - 1, 0)])
        is_first = jnp.concatenate([jnp.ones((1,), I32), (block_e[1:] != block_e[:-1]).astype(I32)])
        dest_p = (pad_start[e_p[:, :TOP_K_EXPERTS]] + rank_p[:, :TOP_K_EXPERTS]).reshape(-1)
        dest_s = (pad_start[e_s[:, :TOP_K_EXPERTS]] + rank_s[:, :TOP_K_EXPERTS]).reshape(-1)

        xs_sorted = jnp.zeros((n_blocks * bs, d), F32)
        xs_sorted = _dispatch(dest_p, x1_p, xs_sorted)
        xs_sorted = _dispatch(dest_s, x1_s, xs_sorted)
        ys = _expert_ffn(block_e, is_first, n_used.reshape(1), xs_sorted, w_gate[l], w_up[l], w_down[l],
                         b_gate[l].reshape(N_EXPERTS, 1, -1), b_up[l].reshape(N_EXPERTS, 1, -1),
                         b_down[l].reshape(N_EXPERTS, 1, -1))
        xp_new = _combine(dest_p, ys, gate_p, x1_p, g2, b2, alpha)
        xs_new = _combine(dest_s, ys, gate_s, x1_s, g2, b2, alpha)

        heads_p = lambda a: a.reshape(batch, -1, HEAD_DIM, seq).transpose(0, 3, 1, 2)
        heads_s = lambda a: a.reshape(-1, HEAD_DIM, dec_b, n_q).transpose(2, 3, 0, 1)
        for acc, a in zip(outs_p, (heads_p(pp["fkt"]), heads_p(pp["fvt"]), pp["lft"].transpose(0, 2, 1),
                                   heads_p(pp["dkt"]), heads_p(pp["dvt"]), pp["ikt"].transpose(0, 2, 1))):
            acc.append(a)
        for acc, a in zip(outs_s, (heads_s(ps["fkt"]), heads_s(ps["fvt"]), per_seq(ps["lft"]).transpose(0, 2, 1),
                                   heads_s(ps["dkt"]), heads_s(ps["dvt"]), per_seq(ps["ikt"]).transpose(0, 2, 1))):
            acc.append(a)
        xp, xs_ = xp_new, xs_new

    stack = lambda lst: [jnp.stack(r, axis=0) for r in lst]
    return (xp.reshape(batch, seq, d), xs_.reshape(dec_b, n_q, d), *stack(outs_p), *stack(outs_s))
```

```python
import functools

import jax
import jax.numpy as jnp
from jax import lax
from jax.experimental import pallas as pl
from jax.experimental.pallas import tpu as pltpu

F32, BF16, I32 = jnp.float32, jnp.bfloat16, jnp.int32
NEG_INF = float("-inf")

HEAD_DIM = 64
H_FOX = 8
H_DSA = 8
W_FOX = H_FOX * HEAD_DIM
W_DSA = H_DSA * HEAD_DIM
H_IDX = 16
D_IDX = 64
TOPK_MAX = 256
PAGE_SIZE = 128
ROPE_THETA = 10000.0
ATTN_SCALE = HEAD_DIM ** -0.5
IDX_SCALE = D_IDX ** -0.5
IDX_W_SCALE = H_IDX ** -0.5
N_EXPERTS = 32
TOP_K_EXPERTS = 4
SWIGLU_LIMIT = 7.0
SWIGLU_ALPHA = 1.702
LN_EPS = 1e-5

LANES = 128
KEY_NEG_INF = -2139095041
INT_MIN = -2 ** 31
VMEM_LIMIT = 56 * 1024 * 1024

NT_DIMS = (((1,), (1,)), ((), ()))


def _cparams(sem):
    return pltpu.CompilerParams(dimension_semantics=sem, vmem_limit_bytes=VMEM_LIMIT)


def _split3(x):
    hi = x.astype(BF16)
    r1 = x - hi.astype(F32)
    mid = r1.astype(BF16)
    lo = (r1 - mid.astype(F32)).astype(BF16)
    return hi, mid, lo


def _lane_cumsum(x, upper):
    hi, mid, lo = _split3(x)
    return (jnp.dot(hi, upper, preferred_element_type=F32)
            + jnp.dot(mid, upper, preferred_element_type=F32)
            + jnp.dot(lo, upper, preferred_element_type=F32))


def _upper_tri(n):
    r = lax.broadcasted_iota(I32, (n, n), 0)
    c = lax.broadcasted_iota(I32, (n, n), 1)
    return (r <= c).astype(BF16)


def _sort_key(x):
    bits = pltpu.bitcast(x, I32)
    return bits ^ ((bits >> 31) & 0x7FFFFFFF)


C_FQ, C_FK, C_FV, C_DQ, C_DK, C_DV, C_IQ, C_SM = 0, 512, 1024, 1536, 2048, 2560, 3072, 4096
N_PROJ = 4224
SM_IK, SM_LF, SM_IW = 0, 64, 72


def _in_proj_kernel(x_ref, w_ref, cos_ref, sin_ref, bf_ref,
                    fqt_ref, fkb_ref, fkt_ref, fvt_ref, fvtc_ref, cb_ref, lft_ref,
                    dqt_ref, dkb_ref, dkt_ref, dvt_ref, dvtc_ref, iqt_ref, ikb_ref, ikt_ref, iwt_ref,
                    carry_ref, *, tiles_per_seq):
    @pl.when(pl.program_id(0) % tiles_per_seq == 0)
    def _():
        carry_ref[...] = jnp.zeros(carry_ref.shape, F32)

    xb = x_ref[...].astype(BF16)
    tm = xb.shape[0]
    cos = cos_ref[...]
    sin = sin_ref[...]
    lane = lax.broadcasted_iota(I32, (tm, LANES), 1)
    first_half = (lane & (HEAD_DIM - 1)) < HEAD_DIM // 2

    def mm(c0, n):
        return jnp.dot(xb, w_ref[:, c0:c0 + n], preferred_element_type=F32)

    def rope(y):
        outs = []
        for c in range(0, y.shape[1], LANES):
            yc = y[:, c:c + LANES]
            partner = jnp.where(first_half, pltpu.roll(yc, LANES - HEAD_DIM // 2, 1),
                                pltpu.roll(yc, HEAD_DIM // 2, 1))
            outs.append(yc * cos + partner * sin)
        return outs[0] if len(outs) == 1 else jnp.concatenate(outs, axis=1)

    y = mm(C_FQ, W_FOX)
    fqt_ref[...] = (y * ATTN_SCALE).T.astype(BF16)
    y = mm(C_FK, W_FOX)
    fkb_ref[...] = y.astype(BF16)
    fkt_ref[...] = y.T
    yt = mm(C_FV, W_FOX).T
    fvt_ref[...] = yt
    fvtc_ref[...] = yt.astype(BF16)
    y = rope(mm(C_DQ, W_DSA))
    dqt_ref[...] = (y * ATTN_SCALE).T.astype(BF16)
    y = rope(mm(C_DK, W_DSA))
    dkb_ref[...] = y.astype(BF16)
    dkt_ref[...] = y.T
    yt = mm(C_DV, W_DSA).T
    dvt_ref[...] = yt
    dvtc_ref[...] = yt.astype(BF16)
    y = rope(mm(C_IQ, H_IDX * D_IDX))
    iqt_ref[...] = (y * IDX_SCALE).T.astype(BF16)
    z = mm(C_SM, LANES)
    r = rope(z)
    ikb_ref[...] = r[:, SM_IK:SM_IK + D_IDX].astype(BF16)
    zb = z + bf_ref[...]
    lsg = -(jnp.maximum(-zb, 0.0) + jnp.log1p(jnp.exp(-jnp.abs(zb))))
    small_t = jnp.where(lane < SM_LF, r, jnp.where(lane < SM_IW, lsg, z * IDX_W_SCALE)).T
    ikt_ref[...] = small_t[SM_IK:SM_IK + D_IDX, :]
    lft_ref[...] = small_t[SM_LF:SM_LF + H_FOX, :]
    iwt_ref[...] = small_t[SM_IW:SM_IW + H_IDX, :]

    hi, mid, lo = _split3(jnp.where((lane >= SM_LF) & (lane < SM_IW), lsg, 0.0))
    rr = lax.broadcasted_iota(I32, (tm, tm), 0)
    cc = lax.broadcasted_iota(I32, (tm, tm), 1)
    lower = (cc <= rr).astype(BF16)
    parts = jnp.dot(lower, jnp.concatenate([hi, mid, lo], axis=1), preferred_element_type=F32)
    cs = parts[:, :LANES] + parts[:, LANES:2 * LANES] + parts[:, 2 * LANES:] + carry_ref[...]
    carry_ref[...] = cs[tm - 1:tm, :]
    for h in range(H_FOX):
        cb_ref[h] = jnp.broadcast_to(cs[:, SM_LF + h:SM_LF + h + 1], (tm, LANES))


FOX_KV_CHUNK = 256
DSA_KV_CHUNK = 512
VALUE_BISECTION_ROUNDS = 14


def _in_proj(x2d, wp, bf_row, cos_t, sin_t, n_seq, seq, tm):
    rows, d = x2d.shape
    tps = seq // tm
    ckf, ckd = min(FOX_KV_CHUNK, seq), min(DSA_KV_CHUNK, seq)
    n_tab = cos_t.shape[0] // tm
    t_spec = lambda w: pl.BlockSpec((None, w, tm), lambda i: (i // tps, 0, i % tps))
    n_spec = lambda w: pl.BlockSpec((tm, w), lambda i: (i, 0))
    c_spec = lambda w, ck: pl.BlockSpec((None, None, w, tm),
                                        lambda i: (i // tps, (i % tps) * tm // ck, 0, (i % tps) % (ck // tm)))
    t_shape = lambda w, dt: jax.ShapeDtypeStruct((n_seq, w, seq), dt)
    n_shape = lambda w, dt: jax.ShapeDtypeStruct((rows, w), dt)
    c_shape = lambda w, ck: jax.ShapeDtypeStruct((n_seq, seq // ck, w, ck), BF16)
    wi = H_IDX * D_IDX
    outs = [("fqt", t_spec(W_FOX), t_shape(W_FOX, BF16)), ("fkb", n_spec(W_FOX), n_shape(W_FOX, BF16)),
            ("fkt", t_spec(W_FOX), t_shape(W_FOX, F32)), ("fvt", t_spec(W_FOX), t_shape(W_FOX, F32)),
            ("fvtc", c_spec(W_FOX, ckf), c_shape(W_FOX, ckf)),
            ("cb", pl.BlockSpec((H_FOX, tm, LANES), lambda i: (0, i, 0)),
             jax.ShapeDtypeStruct((H_FOX, rows, LANES), F32)),
            ("lft", t_spec(H_FOX), t_shape(H_FOX, F32)),
            ("dqt", t_spec(W_DSA), t_shape(W_DSA, BF16)), ("dkb", n_spec(W_DSA), n_shape(W_DSA, BF16)),
            ("dkt", t_spec(W_DSA), t_shape(W_DSA, F32)), ("dvt", t_spec(W_DSA), t_shape(W_DSA, F32)),
            ("dvtc", c_spec(W_DSA, ckd), c_shape(W_DSA, ckd)),
            ("iqt", t_spec(wi), t_shape(wi, BF16)), ("ikb", n_spec(D_IDX), n_shape(D_IDX, BF16)),
            ("ikt", t_spec(D_IDX), t_shape(D_IDX, F32)), ("iwt", t_spec(H_IDX), t_shape(H_IDX, F32))]
    res = pl.pallas_call(
        functools.partial(_in_proj_kernel, tiles_per_seq=tps),
        grid=(rows // tm,),
        in_specs=[pl.BlockSpec((tm, d), lambda i: (i, 0)),
                  pl.BlockSpec((d, N_PROJ), lambda i: (0, 0)),
                  pl.BlockSpec((tm, LANES), lambda i: (i % n_tab, 0)),
                  pl.BlockSpec((tm, LANES), lambda i: (i % n_tab, 0)),
                  pl.BlockSpec((1, LANES), lambda i: (0, 0))],
        out_specs=[o[1] for o in outs],
        out_shape=[o[2] for o in outs],
        scratch_shapes=[pltpu.VMEM((1, LANES), F32)],
        compiler_params=_cparams(("arbitrary",)),
        name="in_proj",
    )(x2d, wp, cos_t, sin_t, bf_row)
    return {o[0]: r for o, r in zip(outs, res)}


SUBLANES = 8


def _over_sublanes(x8, reduce_fn):
    return jnp.broadcast_to(reduce_fn(x8, axis=0, keepdims=True), x8.shape)


def _by_sublane_tile(x):
    return x.reshape(x.shape[0] // SUBLANES, SUBLANES, x.shape[1])


def _flash_step(n_heads, score_fn, vt_fn, st):
    m_ref, l_ref, acc_ref, _, s_ref, p_ref, alpha_ref = st
    for h in range(n_heads):
        s_ref[h] = score_fn(h)
    for h in range(n_heads):
        s3 = _by_sublane_tile(s_ref[h])
        m_old = m_ref[h]
        m_new = jnp.maximum(m_old, _over_sublanes(jnp.max(s3, axis=0), jnp.max))
        alpha = jnp.exp(m_old - m_new)
        p3 = jnp.exp(s3 - m_new[None])
        l_ref[h] = alpha * l_ref[h] + _over_sublanes(jnp.sum(p3, axis=0), jnp.sum)
        p_ref[h] = p3.reshape(s_ref.shape[1:]).astype(BF16)
        alpha_ref[h] = alpha
        m_ref[h] = m_new
    for h in range(n_heads):
        pv = jnp.dot(vt_fn(h), p_ref[h], preferred_element_type=F32)
        acc_ref[h] = (_by_sublane_tile(acc_ref[h]) * alpha_ref[h][None]).reshape(acc_ref.shape[1:]) + pv


def _flash_init(m_ref, l_ref, acc_ref):
    m_ref[...] = jnp.full(m_ref.shape, -1e30, F32)
    l_ref[...] = jnp.zeros(l_ref.shape, F32)
    acc_ref[...] = jnp.zeros(acc_ref.shape, F32)


def _flash_finish(o_ref, l_ref, acc_ref, n_heads):
    tq = acc_ref.shape[2]
    for pair in range(n_heads // 2):
        parts = [(_by_sublane_tile(acc_ref[h]) / l_ref[h][None]).reshape(HEAD_DIM, tq)
                 for h in (2 * pair, 2 * pair + 1)]
        o_ref[:, pair * LANES:(pair + 1) * LANES] = jnp.concatenate(parts, axis=0).T.astype(BF16)


def _flash_scratch(n_heads, tk, tq):
    return [pltpu.VMEM((n_heads, SUBLANES, tq), F32), pltpu.VMEM((n_heads, SUBLANES, tq), F32),
            pltpu.VMEM((n_heads, HEAD_DIM, tq), F32), pltpu.VMEM((n_heads, LANES, tq), BF16),
            pltpu.VMEM((n_heads, tk, tq), F32), pltpu.VMEM((n_heads, tk, tq), BF16),
            pltpu.VMEM((n_heads, SUBLANES, tq), F32)]


def _stage_pair_masked_q(qt_ref, qpm_ref, n_heads):
    for h in range(n_heads):
        pair = h // 2
        rows = qt_ref[pair * LANES:(pair + 1) * LANES, :].astype(F32)
        r = lax.broadcasted_iota(I32, rows.shape, 0)
        qpm_ref[h] = jnp.where(r // HEAD_DIM == h % 2, rows, 0.0).astype(BF16)


def _fox_kernel(qt_ref, k_ref, vt_ref, cb_ref, o_ref, *st, tq):
    m_ref, l_ref, acc_ref, qpm_ref = st[:4]
    i = pl.program_id(1)
    _stage_pair_masked_q(qt_ref, qpm_ref, H_FOX)
    _flash_init(m_ref, l_ref, acc_ref)
    key_row = lax.broadcasted_iota(I32, (tq, tq), 0)
    query_col = lax.broadcasted_iota(I32, (tq, tq), 1)
    visible = key_row <= query_col

    def kv_step(j, masked):
        start = pl.multiple_of(j * tq, tq)

        def scores(h):
            pair = h // 2
            s = jnp.dot(k_ref[pl.ds(start, tq), pair * LANES:(pair + 1) * LANES], qpm_ref[h],
                        preferred_element_type=F32)
            s = s - jnp.concatenate([cb_ref[h, pl.ds(start, tq), :]] * (tq // LANES), axis=1)
            return jnp.where(visible, s, NEG_INF) if masked else s

        _flash_step(H_FOX, scores, lambda h: vt_ref[j, h * HEAD_DIM:(h + 1) * HEAD_DIM, :], st)

    def body(j, c):
        kv_step(j, False)
        return c

    lax.fori_loop(0, i, body, 0)
    kv_step(i, True)
    _flash_finish(o_ref, l_ref, acc_ref, H_FOX)


def _fox_prompt(qt, kb, vtc, cb, batch, seq):
    tq = FOX_KV_CHUNK
    nq = seq // tq
    return pl.pallas_call(
        functools.partial(_fox_kernel, tq=tq),
        grid=(batch, nq),
        in_specs=[pl.BlockSpec((None, W_FOX, tq), lambda b, i: (b, 0, i)),
                  pl.BlockSpec((seq, W_FOX), lambda b, i: (b, 0)),
                  pl.BlockSpec((None, nq, W_FOX, tq), lambda b, i: (b, 0, 0, 0)),
                  pl.BlockSpec((H_FOX, seq, LANES), lambda b, i: (0, b, 0))],
        out_specs=pl.BlockSpec((tq, W_FOX), lambda b, i: (b * nq + i, 0)),
        out_shape=jax.ShapeDtypeStruct((batch * seq, W_FOX), BF16),
        scratch_shapes=_flash_scratch(H_FOX, tq, tq),
        compiler_params=_cparams(("arbitrary", "arbitrary")),
        name="fox_prompt",
    )(qt, kb, vtc, cb)


def _search_threshold(count_ge, lo0, hi0, topk):
    def key_to_value(k):
        return pltpu.bitcast(k ^ ((k >> 31) & 0x7FFFFFFF), F32)

    def key_midpoint(lo, hi):
        return (lo >> 1) + (hi >> 1) + (lo & hi & 1)

    def guarded(lo, hi, value):
        mid = _sort_key(value)
        return jnp.where((mid > lo) & (mid < hi), mid, key_midpoint(lo, hi))

    def refine(lo, hi, n_lo, n_hi, mid):
        n_mid = count_ge(mid)
        take = n_mid >= topk
        return (jnp.where(take, mid, lo), jnp.where(take, hi, mid),
                jnp.where(take, n_mid, n_lo), jnp.where(take, n_hi, n_mid))

    def search_round(state):
        it, lo, hi, n_lo, n_hi, _ = state
        by_value = it < VALUE_BISECTION_ROUNDS
        for _ in range(2):
            mid_v = guarded(lo, hi, 0.5 * key_to_value(lo) + 0.5 * key_to_value(hi))
            lo, hi, n_lo, n_hi = refine(lo, hi, n_lo, n_hi, jnp.where(by_value, mid_v, key_midpoint(lo, hi)))
        active = (n_lo > topk) & (hi != lo + 1)
        return it + 1, lo, hi, n_lo, n_hi, jnp.max(active.astype(F32))

    def searching(state):
        return (state[0] < VALUE_BISECTION_ROUNDS + 17) & (state[5] > 0)

    n0 = count_ge(lo0)
    return lax.while_loop(searching, search_round,
                          (jnp.int32(0), lo0, hi0, n0, jnp.zeros_like(n0), jnp.max((n0 > topk).astype(F32))))[1]


def _dsa_kernel(iqt_ref, iwt_ref, ik_ref, qt_ref, k_ref, vt_ref, o_ref, key_ref, bias_ref, *st,
                tq, ck, topk):
    m_ref, l_ref, acc_ref, qpm_ref = st[:4]
    i = pl.program_id(1)
    q0 = i * tq
    n_chunks = (q0 + tq + ck - 1) // ck
    key_row = lax.broadcasted_iota(I32, (ck, tq), 0)
    query_pos = q0 + lax.broadcasted_iota(I32, (ck, tq), 1)
    iw8 = [jnp.broadcast_to(iwt_ref[h:h + 1, :], (SUBLANES, tq)) for h in range(H_IDX)]

    def score_chunk(c, _):
        start = pl.multiple_of(c * ck, ck)
        ikc = ik_ref[pl.ds(start, ck), :]
        sc = jnp.zeros((ck // SUBLANES, SUBLANES, tq), F32)
        for h in range(H_IDX):
            rel = jnp.dot(ikc, iqt_ref[h * D_IDX:(h + 1) * D_IDX, :], preferred_element_type=F32)
            sc = sc + _by_sublane_tile(jnp.maximum(rel, 0.0)) * iw8[h][None]
        sc = jnp.where(key_row + start <= query_pos, sc.reshape(ck, tq), NEG_INF)
        key_ref[c] = _sort_key(sc)
        return 0

    lax.fori_loop(0, n_chunks, score_chunk, 0)

    def count(pred):
        def body(c, acc):
            return acc + jnp.sum(pred(_by_sublane_tile(key_ref[c])).astype(F32), axis=0)
        return _over_sublanes(lax.fori_loop(0, n_chunks, body, jnp.zeros((SUBLANES, tq), F32)), jnp.sum)

    def extreme(pick, reduce_fn, fill):
        def body(c, acc):
            k = _by_sublane_tile(key_ref[c])
            return pick(acc, reduce_fn(jnp.where(k > KEY_NEG_INF, k, fill), axis=0))
        init = jnp.full((SUBLANES, tq), fill, I32)
        return _over_sublanes(lax.fori_loop(0, n_chunks, body, init), reduce_fn)

    lo0 = extreme(jnp.minimum, jnp.min, jnp.int32(2 ** 31 - 1))
    hi0 = extreme(jnp.maximum, jnp.max, jnp.int32(KEY_NEG_INF)) + 1

    thr = _search_threshold(lambda t: count(lambda k: k >= t[None]), lo0, hi0, topk)
    need = topk - count(lambda k: k > thr[None])
    ties = count(lambda k: k == thr[None])
    excess = jnp.max(jnp.where((ties > need) & (thr > KEY_NEG_INF), 1.0, 0.0)) > 0.0

    @pl.when(jnp.logical_not(excess))
    def _():
        def fill(c, _):
            k = _by_sublane_tile(key_ref[c])
            bias_ref[c] = jnp.where((k >= thr[None]) & (k > KEY_NEG_INF), 0.0, NEG_INF).reshape(ck, tq)
            return 0
        lax.fori_loop(0, n_chunks, fill, 0)

    @pl.when(excess)
    def _():
        rr = lax.broadcasted_iota(I32, (ck, ck), 0)
        cc = lax.broadcasted_iota(I32, (ck, ck), 1)
        strict_lower = (cc < rr).astype(BF16)

        def fill(c, before):
            k = _by_sublane_tile(key_ref[c])
            tie = k == thr[None]
            tie_f = tie.astype(F32)
            earlier = jnp.dot(strict_lower, tie_f.reshape(ck, tq).astype(BF16), preferred_element_type=F32)
            earlier = _by_sublane_tile(earlier) + before[None]
            sel = (k > thr[None]) | (tie & (earlier < need[None]))
            bias_ref[c] = jnp.where(sel & (k > KEY_NEG_INF), 0.0, NEG_INF).reshape(ck, tq)
            return before + _over_sublanes(jnp.sum(tie_f, axis=0), jnp.sum)
        lax.fori_loop(0, n_chunks, fill, jnp.zeros((SUBLANES, tq), F32))

    _stage_pair_masked_q(qt_ref, qpm_ref, H_DSA)
    _flash_init(m_ref, l_ref, acc_ref)

    def kv_step(c, carry):
        start = pl.multiple_of(c * ck, ck)
        def scores(h):
            pair = h // 2
            return jnp.dot(k_ref[pl.ds(start, ck), pair * LANES:(pair + 1) * LANES], qpm_ref[h],
                           preferred_element_type=F32) + bias_ref[c]

        _flash_step(H_DSA, scores, lambda h: vt_ref[c, h * HEAD_DIM:(h + 1) * HEAD_DIM, :], st)
        return carry

    lax.fori_loop(0, n_chunks, kv_step, 0)
    _flash_finish(o_ref, l_ref, acc_ref, H_DSA)


def _dsa_prompt(iqt, iwt, ikb, dqt, dkb, dvtc, batch, seq, tq):
    ck = dvtc.shape[3]
    nq = seq // tq
    topk = min(TOPK_MAX, seq // 4)
    return pl.pallas_call(
        functools.partial(_dsa_kernel, tq=tq, ck=ck, topk=topk),
        grid=(batch, nq),
        in_specs=[pl.BlockSpec((None, H_IDX * D_IDX, tq), lambda b, i: (b, 0, i)),
                  pl.BlockSpec((None, H_IDX, tq), lambda b, i: (b, 0, i)),
                  pl.BlockSpec((seq, D_IDX), lambda b, i: (b, 0)),
                  pl.BlockSpec((None, W_DSA, tq), lambda b, i: (b, 0, i)),
                  pl.BlockSpec((seq, W_DSA), lambda b, i: (b, 0)),
                  pl.BlockSpec((None, seq // ck, W_DSA, ck), lambda b, i: (b, 0, 0, 0))],
        out_specs=pl.BlockSpec((tq, W_DSA), lambda b, i: (b * nq + i, 0)),
        out_shape=jax.ShapeDtypeStruct((batch * seq, W_DSA), BF16),
        scratch_shapes=[pltpu.VMEM((seq // ck, ck, tq), I32), pltpu.VMEM((seq // ck, ck, tq), F32)]
        + _flash_scratch(H_DSA, ck, tq),
        compiler_params=_cparams(("arbitrary", "arbitrary")),
        name="dsa_prompt",
    )(iqt, iwt, ikb, dqt, dkb, dvtc)


KV_PAGES_PER_STEP = 16
IDX_PAGES_PER_STEP = 32


def _head_mask(rows, width):
    r = lax.broadcasted_iota(I32, (rows, width), 0)
    c = lax.broadcasted_iota(I32, (rows, width), 1)
    return (c // HEAD_DIM) == (r % H_FOX)


def _block_diag_q(q, n_q):
    hm = _head_mask(H_FOX, q.shape[1])
    qf = q.astype(F32)
    parts = [jnp.where(hm, jnp.broadcast_to(qf[t:t + 1, :], hm.shape), 0.0) for t in range(n_q)]
    return jnp.concatenate(parts, axis=0).astype(BF16)


def _rows_to_groups(x, n_q):
    return jnp.concatenate([jnp.broadcast_to(x[t:t + 1, :], (H_FOX, x.shape[1])) for t in range(n_q)], axis=0)


DECODE_SPLITS = 2


def _softmax_page(g, s, vt_bf16, m_ref, l_ref, acc_ref):
    m = m_ref[g]
    m_new = jnp.maximum(m, jnp.max(s, axis=1, keepdims=True))
    alpha = jnp.exp(m - m_new)
    p = jnp.exp(s - m_new)
    l_ref[g] = alpha * l_ref[g] + jnp.sum(p, axis=1, keepdims=True)
    acc_ref[g] = alpha * acc_ref[g] + lax.dot_general(p.astype(BF16), vt_bf16, NT_DIMS,
                                                      preferred_element_type=F32)
    m_ref[g] = m_new


def _decode_scratch(rows, w):
    g = DECODE_SPLITS
    return [pltpu.VMEM((g, rows, 1), F32), pltpu.VMEM((g, rows, 1), F32), pltpu.VMEM((g, rows, w), F32)]


def _stack_pages(refs):
    return jnp.concatenate([r[...].reshape(-1, PAGE_SIZE) for r in refs], axis=1).astype(BF16)


def _finish_decode(o_ref, m_ref, l_ref, acc_ref, n_q):
    m = m_ref[0]
    for g in range(1, DECODE_SPLITS):
        m = jnp.maximum(m, m_ref[g])
    l = jnp.zeros_like(m)
    acc = jnp.zeros(acc_ref.shape[1:], F32)
    for g in range(DECODE_SPLITS):
        w = jnp.exp(m_ref[g] - m)
        l = l + w * l_ref[g]
        acc = acc + w * acc_ref[g]
    o = acc / l
    o = jnp.where(_head_mask(o.shape[0], o.shape[1]), o, 0.0)
    o_ref[...] = jnp.sum(o.reshape(n_q, H_FOX, o.shape[1]), axis=1).astype(BF16)


def _sample_fox_kernel(pt_ref, q_ref, kn_ref, vn_ref, lfn_ref, *rest, n_q, pps):
    k_refs, v_refs, lf_refs = rest[:pps], rest[pps:2 * pps], rest[2 * pps:3 * pps]
    o_ref, m_ref, l_ref, acc_ref, cc_ref, qbd_ref = rest[3 * pps:]
    step = pl.program_id(1)
    rows = n_q * H_FOX

    @pl.when(step == 0)
    def _():
        m_ref[...] = jnp.full(m_ref.shape, -1e30, F32)
        l_ref[...] = jnp.zeros(l_ref.shape, F32)
        acc_ref[...] = jnp.zeros(acc_ref.shape, F32)
        cc_ref[...] = jnp.zeros(cc_ref.shape, F32)
        qbd_ref[...] = _block_diag_q(q_ref[...], n_q)

    upper = _upper_tri(PAGE_SIZE)

    within = _lane_cumsum(jnp.concatenate([r[...] for r in lf_refs], axis=0), upper)
    off = cc_ref[...]
    parts = []
    for p in range(pps):
        cp = within[p * H_FOX:(p + 1) * H_FOX] + off
        parts.append(cp)
        off = cp[:, PAGE_SIZE - 1:PAGE_SIZE]
    cc_ref[...] = off
    per = pps // DECODE_SPLITS
    for g in range(DECODE_SPLITS):
        pages = slice(g * per, (g + 1) * per)
        cbias = jnp.concatenate(parts[pages], axis=1)
        s = jnp.dot(qbd_ref[...], _stack_pages(k_refs[pages]), preferred_element_type=F32)
        s = s - jnp.concatenate([cbias] * n_q, axis=0)
        _softmax_page(g, s, _stack_pages(v_refs[pages]), m_ref, l_ref, acc_ref)

    @pl.when(step == pl.num_programs(1) - 1)
    def _():
        r = lax.broadcasted_iota(I32, (rows, PAGE_SIZE), 0)
        c = lax.broadcasted_iota(I32, (rows, PAGE_SIZE), 1)
        cn = _lane_cumsum(lfn_ref[...], upper) + cc_ref[...]
        sn = jnp.dot(qbd_ref[...], kn_ref[...].astype(BF16), preferred_element_type=F32)
        sn = sn - jnp.concatenate([cn] * n_q, axis=0)
        sn = jnp.where(c <= r // H_FOX, sn, NEG_INF)
        _softmax_page(0, sn, vn_ref[...].astype(BF16), m_ref, l_ref, acc_ref)
        _finish_decode(o_ref, m_ref, l_ref, acc_ref, n_q)


def _page_spec(block, p, pps):
    nd = len(block)
    return pl.BlockSpec((None,) + block, lambda b, s, pt: (pt[b, s * pps + p],) + (0,) * nd)


def _sample_fox(page_table, q, k_new, v_new, lft_new, pool_k, pool_v, pool_lft):
    dec_b, n_q, w = q.shape
    n_pages = page_table.shape[1]
    pps = min(KV_PAGES_PER_STEP, n_pages)
    rows = n_q * H_FOX
    seq_spec = lambda blk: pl.BlockSpec((None,) + blk, lambda b, s, pt: (b,) + (0,) * len(blk))
    kv_page = (H_FOX, HEAD_DIM, PAGE_SIZE)
    in_specs = ([seq_spec((n_q, w)), seq_spec((w, PAGE_SIZE)), seq_spec((w, PAGE_SIZE)),
                 seq_spec((H_FOX, PAGE_SIZE))]
                + [_page_spec(kv_page, p, pps) for p in range(pps)]
                + [_page_spec(kv_page, p, pps) for p in range(pps)]
                + [_page_spec((H_FOX, PAGE_SIZE), p, pps) for p in range(pps)])
    return pl.pallas_call(
        functools.partial(_sample_fox_kernel, n_q=n_q, pps=pps),
        grid_spec=pltpu.PrefetchScalarGridSpec(
            num_scalar_prefetch=1,
            grid=(dec_b, n_pages // pps),
            in_specs=in_specs,
            out_specs=pl.BlockSpec((None, n_q, w), lambda b, s, pt: (b, 0, 0)),
            scratch_shapes=_decode_scratch(rows, w) + [pltpu.VMEM((H_FOX, 1), F32),
                                                       pltpu.VMEM((rows, w), BF16)]),
        out_shape=jax.ShapeDtypeStruct((dec_b, n_q, w), BF16),
        compiler_params=_cparams(("arbitrary", "arbitrary")),
        name="sample_fox",
    )(page_table, q, k_new, v_new, lft_new, *([pool_k] * pps), *([pool_v] * pps), *([pool_lft] * pps))


def _sample_index_kernel(pt_ref, iq_ref, iw_ref, kn_ref, *rest, pps, n_q, topk):
    k_refs = rest[:pps]
    bias_ref, biasn_ref, key_ref, keyn_ref = rest[pps:]
    step = pl.program_id(1)
    n_steps = key_ref.shape[0]
    iq = iq_ref[...]
    iw = iw_ref[...]

    def scores(kt):
        rel = jnp.dot(iq, kt, preferred_element_type=F32)
        w = iw * jnp.maximum(rel, 0.0)
        by_parity = jnp.sum(w.reshape(H_IDX // 2, SUBLANES, kt.shape[1]), axis=0)
        return by_parity + pltpu.roll(by_parity, SUBLANES // 2, 0)

    key_ref[step] = _sort_key(scores(_stack_pages(k_refs)))

    @pl.when(step == n_steps - 1)
    def _():
        r = lax.broadcasted_iota(I32, (H_FOX, PAGE_SIZE), 0)
        c = lax.broadcasted_iota(I32, (H_FOX, PAGE_SIZE), 1)
        keyn_ref[...] = _sort_key(jnp.where((c <= r) & (r < n_q), scores(kn_ref[...].astype(BF16)), NEG_INF))

        def count(pred):
            past = jnp.sum(jnp.sum(pred(key_ref[...]).astype(F32), axis=0), axis=1, keepdims=True)
            return past + jnp.sum(pred(keyn_ref[...]).astype(F32), axis=1, keepdims=True)

        def extreme(reduce_fn, fill):
            past = reduce_fn(reduce_fn(jnp.where(key_ref[...] > KEY_NEG_INF, key_ref[...], fill), axis=0),
                             axis=1, keepdims=True)
            new = reduce_fn(jnp.where(keyn_ref[...] > KEY_NEG_INF, keyn_ref[...], fill), axis=1, keepdims=True)
            return reduce_fn(jnp.concatenate([past, new], axis=1), axis=1, keepdims=True)

        lo0 = extreme(jnp.min, jnp.int32(2 ** 31 - 1))
        hi0 = extreme(jnp.max, jnp.int32(KEY_NEG_INF)) + 1
        thr = _search_threshold(lambda t: count(lambda k: k >= t), lo0, hi0, topk)
        need = topk - count(lambda k: k > thr)
        ties = count(lambda k: k == thr)
        excess = jnp.max(jnp.where((ties > need) & (thr > KEY_NEG_INF), 1.0, 0.0)) > 0.0

        @pl.when(jnp.logical_not(excess))
        def _():
            k = key_ref[...]
            bias_ref[...] = jnp.where((k >= thr) & (k > KEY_NEG_INF), 0.0, NEG_INF)
            k = keyn_ref[...]
            biasn_ref[...] = jnp.where((k >= thr) & (k > KEY_NEG_INF), 0.0, NEG_INF)

        @pl.when(excess)
        def _():
            upper = _upper_tri(PAGE_SIZE)

            def tie_block(k, before):
                tie = k == thr
                incl = jnp.dot(tie.astype(BF16), upper, preferred_element_type=F32) + before
                sel = (k > thr) | (tie & (incl - tie.astype(F32) < need))
                return jnp.where(sel & (k > KEY_NEG_INF), 0.0, NEG_INF), incl[:, PAGE_SIZE - 1:PAGE_SIZE]

            def fill(c, before):
                for p in range(pps):
                    lanes = slice(p * PAGE_SIZE, (p + 1) * PAGE_SIZE)
                    b, before = tie_block(key_ref[c, :, lanes], before)
                    bias_ref[c, :, lanes] = b
                return before

            before = lax.fori_loop(0, n_steps, fill, jnp.zeros((H_FOX, 1), F32))
            biasn_ref[...] = tie_block(keyn_ref[...], before)[0]


def _sample_index(page_table, iqx, iwx, kidx_new, pool_kidx, n_q):
    dec_b = iqx.shape[0]
    n_pages = page_table.shape[1]
    pps = min(IDX_PAGES_PER_STEP, n_pages)
    n_steps = n_pages // pps
    step_w = pps * PAGE_SIZE
    topk = min(TOPK_MAX, (n_pages * PAGE_SIZE + n_q) // 4)
    seq_spec = lambda blk: pl.BlockSpec((None,) + blk, lambda b, s, pt: (b,) + (0,) * len(blk))
    iq_rows = iqx.shape[1]
    in_specs = ([seq_spec((iq_rows, D_IDX)), seq_spec((iq_rows, 1)), seq_spec((D_IDX, PAGE_SIZE))]
                + [_page_spec((D_IDX, PAGE_SIZE), p, pps) for p in range(pps)])
    return pl.pallas_call(
        functools.partial(_sample_index_kernel, pps=pps, n_q=n_q, topk=topk),
        grid_spec=pltpu.PrefetchScalarGridSpec(
            num_scalar_prefetch=1,
            grid=(dec_b, n_steps),
            in_specs=in_specs,
            out_specs=[pl.BlockSpec((None, n_steps, H_FOX, step_w), lambda b, s, pt: (b, 0, 0, 0)),
                       pl.BlockSpec((None, H_FOX, PAGE_SIZE), lambda b, s, pt: (b, 0, 0))],
            scratch_shapes=[pltpu.VMEM((n_steps, H_FOX, step_w), I32), pltpu.VMEM((H_FOX, PAGE_SIZE), I32)]),
        out_shape=[jax.ShapeDtypeStruct((dec_b, n_steps, H_FOX, step_w), F32),
                   jax.ShapeDtypeStruct((dec_b, H_FOX, PAGE_SIZE), F32)],
        compiler_params=_cparams(("arbitrary", "arbitrary")),
        name="sample_index",
    )(page_table, iqx, iwx, kidx_new, *([pool_kidx] * pps))


def _sample_dsa_kernel(pt_ref, q_ref, kn_ref, vn_ref, bias_ref, biasn_ref, *rest, n_q, pps):
    k_refs, v_refs = rest[:pps], rest[pps:2 * pps]
    o_ref, m_ref, l_ref, acc_ref, qbd_ref = rest[2 * pps:]
    step = pl.program_id(1)

    @pl.when(step == 0)
    def _():
        m_ref[...] = jnp.full(m_ref.shape, -1e30, F32)
        l_ref[...] = jnp.zeros(l_ref.shape, F32)
        acc_ref[...] = jnp.zeros(acc_ref.shape, F32)
        qbd_ref[...] = _block_diag_q(q_ref[...], n_q)

    def attend(g, kt, vt, bias):
        s = jnp.dot(qbd_ref[...], kt, preferred_element_type=F32) + _rows_to_groups(bias, n_q)
        _softmax_page(g, s, vt, m_ref, l_ref, acc_ref)

    per = pps // DECODE_SPLITS
    for g in range(DECODE_SPLITS):
        pages = slice(g * per, (g + 1) * per)
        attend(g, _stack_pages(k_refs[pages]), _stack_pages(v_refs[pages]),
               bias_ref[0, :, g * per * PAGE_SIZE:(g + 1) * per * PAGE_SIZE])

    @pl.when(step == pl.num_programs(1) - 1)
    def _():
        attend(0, kn_ref[...].astype(BF16), vn_ref[...].astype(BF16), biasn_ref[...])
        _finish_decode(o_ref, m_ref, l_ref, acc_ref, n_q)


def _sample_dsa(page_table, q, k_new, v_new, bias, bias_new, pool_k, pool_v):
    dec_b, n_q, w = q.shape
    n_pages = page_table.shape[1]
    pps = min(KV_PAGES_PER_STEP, n_pages)
    rows = n_q * H_DSA
    kv_page = (H_DSA, HEAD_DIM, PAGE_SIZE)
    per_bias_chunk = bias.shape[3] // (pps * PAGE_SIZE)
    seq_spec = lambda blk: pl.BlockSpec((None,) + blk, lambda b, s, pt: (b,) + (0,) * len(blk))
    in_specs = ([seq_spec((n_q, w)), seq_spec((w, PAGE_SIZE)), seq_spec((w, PAGE_SIZE)),
                 pl.BlockSpec((None, 1, H_FOX, pps * PAGE_SIZE),
                              lambda b, s, pt: (b, s // per_bias_chunk, 0, s % per_bias_chunk)),
                 seq_spec((H_FOX, PAGE_SIZE))]
                + [_page_spec(kv_page, p, pps) for p in range(pps)]
                + [_page_spec(kv_page, p, pps) for p in range(pps)])
    return pl.pallas_call(
        functools.partial(_sample_dsa_kernel, n_q=n_q, pps=pps),
        grid_spec=pltpu.PrefetchScalarGridSpec(
            num_scalar_prefetch=1,
            grid=(dec_b, n_pages // pps),
            in_specs=in_specs,
            out_specs=pl.BlockSpec((None, n_q, w), lambda b, s, pt: (b, 0, 0)),
            scratch_shapes=_decode_scratch(rows, w) + [pltpu.VMEM((rows, w), BF16)]),
        out_shape=jax.ShapeDtypeStruct((dec_b, n_q, w), BF16),
        compiler_params=_cparams(("arbitrary", "arbitrary")),
        name="sample_dsa",
    )(page_table, q, k_new, v_new, bias, bias_new, *([pool_k] * pps), *([pool_v] * pps))


def _layer_norm(r, g, b):
    mu = jnp.mean(r, axis=1, keepdims=True)
    d = r - mu
    var = jnp.mean(d * d, axis=1, keepdims=True)
    return d * lax.rsqrt(var + LN_EPS) * g + b


def _tail_kernel(of_ref, od_ref, x_ref, wo_ref, g_ref, b_ref, wr_ref, br_ref, cin_ref,
                 x1_ref, e_ref, gate_ref, rank_ref, cnt_ref, carry_ref, *, alpha):
    i = pl.program_id(0)
    tm = x_ref.shape[0]

    @pl.when(i == 0)
    def _():
        carry_ref[...] = cin_ref[...]

    r = (alpha * x_ref[...]
         + jnp.dot(of_ref[...], wo_ref[0:W_FOX, :], preferred_element_type=F32)
         + jnp.dot(od_ref[...], wo_ref[W_FOX:W_FOX + W_DSA, :], preferred_element_type=F32))
    x1 = _layer_norm(r, g_ref[...], b_ref[...])
    x1_ref[...] = x1

    logits = jnp.dot(x1.astype(BF16), wr_ref[...], preferred_element_type=F32) + br_ref[...]
    lane = lax.broadcasted_iota(I32, (tm, LANES), 1)
    lane_f = lane.astype(F32)
    vals, idxs, hots = [], [], []
    lg = logits
    for _ in range(TOP_K_EXPERTS):
        m = jnp.max(lg, axis=1, keepdims=True)
        idx = jnp.min(jnp.where(lg == m, lane_f, float(LANES)), axis=1, keepdims=True)
        hot = lane_f == idx
        vals.append(m)
        idxs.append(idx)
        hots.append(hot)
        lg = jnp.where(hot, NEG_INF, lg)
    exps = [jnp.exp(v - vals[0]) for v in vals]
    den = exps[0]
    for e in exps[1:]:
        den = den + e
    sel = jnp.zeros((tm, LANES), F32)
    for hot in hots:
        sel = sel + hot.astype(F32)
    rr = lax.broadcasted_iota(I32, (tm, tm), 0)
    cc = lax.broadcasted_iota(I32, (tm, tm), 1)
    strict_lower = (cc < rr).astype(BF16)
    before = jnp.dot(strict_lower, sel.astype(BF16), preferred_element_type=F32) + carry_ref[...]
    e_out = jnp.zeros((tm, LANES), F32)
    g_out = jnp.zeros((tm, LANES), F32)
    r_out = jnp.zeros((tm, LANES), F32)
    for j in range(TOP_K_EXPERTS):
        rank = jnp.sum(jnp.where(hots[j], before, 0.0), axis=1, keepdims=True)
        e_out = jnp.where(lane == j, idxs[j], e_out)
        g_out = jnp.where(lane == j, exps[j] / den, g_out)
        r_out = jnp.where(lane == j, rank, r_out)
    e_ref[...] = e_out.astype(I32)
    gate_ref[...] = g_out
    rank_ref[...] = r_out.astype(I32)
    carry_ref[...] = carry_ref[...] + jnp.sum(sel, axis=0, keepdims=True)
    cnt_ref[...] = carry_ref[...]


def _tail(of, od, x2d, wo, g1, b1, wr, br, cin, alpha, tm):
    rows, d = x2d.shape
    row_spec = lambda w: pl.BlockSpec((tm, w), lambda i: (i, 0))
    full = lambda a: pl.BlockSpec(a.shape, lambda i: (0,) * a.ndim)
    return pl.pallas_call(
        functools.partial(_tail_kernel, alpha=alpha),
        grid=(rows // tm,),
        in_specs=[row_spec(W_FOX), row_spec(W_DSA), row_spec(d), full(wo), full(g1), full(b1),
                  full(wr), full(br), full(cin)],
        out_specs=[row_spec(d), row_spec(LANES), row_spec(LANES), row_spec(LANES),
                   pl.BlockSpec((1, LANES), lambda i: (0, 0))],
        out_shape=[jax.ShapeDtypeStruct((rows, d), F32), jax.ShapeDtypeStruct((rows, LANES), I32),
                   jax.ShapeDtypeStruct((rows, LANES), F32), jax.ShapeDtypeStruct((rows, LANES), I32),
                   jax.ShapeDtypeStruct((1, LANES), F32)],
        scratch_shapes=[pltpu.VMEM((1, LANES), F32)],
        compiler_params=_cparams(("arbitrary",)),
        name="tail_router",
    )(of, od, x2d, wo, g1, b1, wr, br, cin)


EXPERT_BLOCK = 256
DISPATCH_TOKENS = 128
COMBINE_UNROLL = 4


def _dispatch_kernel(dest_ref, x_ref, xs_in, xs_out, sem, *, td):
    del xs_in

    def row_copy(src_row, dst_row):
        return pltpu.make_async_copy(x_ref.at[pl.ds(src_row, 1), :], xs_out.at[pl.ds(dst_row, 1), :], sem)

    def issue(tt, c):
        for u in range(COMBINE_UNROLL):
            t = tt * COMBINE_UNROLL + u
            for j in range(TOP_K_EXPERTS):
                row_copy(t, dest_ref[t * TOP_K_EXPERTS + j]).start()
        return c

    lax.fori_loop(0, td // COMBINE_UNROLL, issue, 0)

    def drain(tt, c):
        for _ in range(COMBINE_UNROLL * TOP_K_EXPERTS):
            row_copy(0, 0).wait()
        return c

    lax.fori_loop(0, td // COMBINE_UNROLL, drain, 0)


def _dispatch(dest_flat, x1, xs):
    rows, d = x1.shape
    td = DISPATCH_TOKENS
    return pl.pallas_call(
        functools.partial(_dispatch_kernel, td=td),
        grid=(rows // td,),
        in_specs=[pl.BlockSpec((td * TOP_K_EXPERTS,), lambda i: (i,), memory_space=pltpu.SMEM),
                  pl.BlockSpec((td, d), lambda i: (i, 0)), pl.BlockSpec(memory_space=pl.ANY)],
        out_specs=pl.BlockSpec(memory_space=pl.ANY),
        out_shape=jax.ShapeDtypeStruct(xs.shape, xs.dtype),
        scratch_shapes=[pltpu.SemaphoreType.DMA(())],
        input_output_aliases={2: 0},
        compiler_params=_cparams(("arbitrary",)),
        name="moe_dispatch",
    )(dest_flat, x1, xs)


def _ffn_kernel(be_ref, first_ref, nused_ref, xs_ref, wg_ref, wu_ref, wd_ref, bg_ref, bu_ref, bd_ref,
                ys_ref, wgb_ref, wub_ref, wdb_ref):
    b = pl.program_id(0)

    @pl.when(first_ref[b] == 1)
    def _():
        wgb_ref[...] = wg_ref[...].astype(BF16)
        wub_ref[...] = wu_ref[...].astype(BF16)
        wdb_ref[...] = wd_ref[...].astype(BF16)

    @pl.when(b < nused_ref[0])
    def _():
        xb = xs_ref[...].astype(BF16)
        g = jnp.dot(xb, wgb_ref[...], preferred_element_type=F32) + bg_ref[...]
        u = jnp.dot(xb, wub_ref[...], preferred_element_type=F32) + bu_ref[...]
        g = jnp.minimum(g, SWIGLU_LIMIT)
        u = jnp.clip(u, -SWIGLU_LIMIT, SWIGLU_LIMIT)
        h = g * (1.0 / (1.0 + jnp.exp(-SWIGLU_ALPHA * g))) * (u + 1.0)
        ys_ref[...] = jnp.dot(h.astype(BF16), wdb_ref[...], preferred_element_type=F32) + bd_ref[...]

    @pl.when(b >= nused_ref[0])
    def _():
        ys_ref[...] = jnp.zeros(ys_ref.shape, F32)


def _expert_ffn(block_e, is_first, n_used, xs, wg, wu, wd, bg, bu, bd):
    p, d = xs.shape
    bs = EXPERT_BLOCK
    de = wg.shape[2]
    w_spec = lambda shape: pl.BlockSpec((None,) + shape, lambda b, be, fi, nu: (be[b], 0, 0))
    return pl.pallas_call(
        _ffn_kernel,
        grid_spec=pltpu.PrefetchScalarGridSpec(
            num_scalar_prefetch=3,
            grid=(p // bs,),
            in_specs=[pl.BlockSpec((bs, d), lambda b, be, fi, nu: (b, 0)),
                      w_spec((d, de)), w_spec((d, de)), w_spec((de, d)),
                      w_spec((1, de)), w_spec((1, de)), w_spec((1, d))],
            out_specs=pl.BlockSpec((bs, d), lambda b, be, fi, nu: (b, 0)),
            scratch_shapes=[pltpu.VMEM((d, de), BF16), pltpu.VMEM((d, de), BF16), pltpu.VMEM((de, d), BF16)]),
        out_shape=jax.ShapeDtypeStruct((p, d), F32),
        compiler_params=_cparams(("arbitrary",)),
        name="moe_ffn",
    )(block_e, is_first, n_used, xs, wg, wu, wd, bg, bu, bd)


def _combine_kernel(dest_ref, dest_next_ref, ys_hbm, gate_ref, x1_ref, g_ref, b_ref, o_ref, buf_ref, sem,
                    *, tc, alpha):
    i = pl.program_id(0)
    slot = i % 2

    def row_copy(src_row, sl, j, t):
        return pltpu.make_async_copy(ys_hbm.at[pl.ds(src_row, 1), :], buf_ref.at[sl, j, pl.ds(t, 1), :],
                                     sem.at[sl])

    def issue(d_ref, sl):
        def body(tt, c):
            for u in range(COMBINE_UNROLL):
                t = tt * COMBINE_UNROLL + u
                for j in range(TOP_K_EXPERTS):
                    row_copy(d_ref[t * TOP_K_EXPERTS + j], sl, j, t).start()
            return c
        lax.fori_loop(0, tc // COMBINE_UNROLL, body, 0)

    @pl.when(i == 0)
    def _():
        issue(dest_ref, slot)

    @pl.when(i + 1 < pl.num_programs(0))
    def _():
        issue(dest_next_ref, 1 - slot)

    def drain(tt, c):
        for _ in range(COMBINE_UNROLL * TOP_K_EXPERTS):
            row_copy(0, slot, 0, 0).wait()
        return c

    lax.fori_loop(0, tc // COMBINE_UNROLL, drain, 0)

    gates = gate_ref[...]
    moe = gates[:, 0:1] * buf_ref[slot, 0]
    for j in range(1, TOP_K_EXPERTS):
        moe = moe + gates[:, j:j + 1] * buf_ref[slot, j]
    o_ref[...] = _layer_norm(alpha * x1_ref[...] + moe, g_ref[...], b_ref[...])


def _combine(dest_flat, ys, gates, x1, g2, b2, alpha):
    rows, d = x1.shape
    tc = DISPATCH_TOKENS
    last = rows // tc - 1
    return pl.pallas_call(
        functools.partial(_combine_kernel, tc=tc, alpha=alpha),
        grid=(rows // tc,),
        in_specs=[pl.BlockSpec((tc * TOP_K_EXPERTS,), lambda i: (i,), memory_space=pltpu.SMEM),
                  pl.BlockSpec((tc * TOP_K_EXPERTS,), lambda i: (jnp.minimum(i + 1, last),),
                               memory_space=pltpu.SMEM),
                  pl.BlockSpec(memory_space=pl.ANY),
                  pl.BlockSpec((tc, LANES), lambda i: (i, 0)),
                  pl.BlockSpec((tc, d), lambda i: (i, 0)),
                  pl.BlockSpec((1, d), lambda i: (0, 0)),
                  pl.BlockSpec((1, d), lambda i: (0, 0))],
        out_specs=pl.BlockSpec((tc, d), lambda i: (i, 0)),
        out_shape=jax.ShapeDtypeStruct((rows, d), F32),
        scratch_shapes=[pltpu.VMEM((2, TOP_K_EXPERTS, tc, d), F32), pltpu.SemaphoreType.DMA((2,))],
        compiler_params=_cparams(("arbitrary",)),
        name="moe_combine",
    )(dest_flat, dest_flat, ys, gates, x1, g2, b2)


def _prep_w_in(w, b_f):
    d = w.shape[0]
    sizes = (W_FOX, W_FOX, W_FOX, H_FOX, W_DSA, W_DSA, W_DSA, H_IDX * D_IDX, D_IDX, H_IDX)
    parts, acc = [], 0
    for s in sizes:
        parts.append(w[:, acc:acc + s])
        acc += s
    fq, fk, fv, fl, dq, dk, dv, iq, ik, iw = parts
    small = jnp.concatenate([ik, fl, iw, jnp.zeros((d, LANES - D_IDX - H_FOX - H_IDX), w.dtype)], axis=1)
    wp = jnp.concatenate([fq, fk, fv, dq, dk, dv, iq, small], axis=1).astype(BF16)
    bf_row = jnp.zeros((1, LANES), F32).at[0, SM_LF:SM_LF + H_FOX].set(b_f.astype(F32))
    return wp, bf_row


def _rope_tables(pos):
    half = HEAD_DIM // 2
    inv = ROPE_THETA ** (-jnp.arange(half, dtype=F32) / half)
    ang = pos.astype(F32)[:, None] * inv[None, :]
    cos, sin = jnp.cos(ang), jnp.sin(ang)
    cos_t = jnp.concatenate([cos, cos, cos, cos], axis=1)
    sin_t = jnp.concatenate([-sin, sin, -sin, sin], axis=1)
    return cos_t, sin_t


def _pad_rows(a, n):
    return jnp.pad(a, ((0, 0), (0, n - a.shape[1]), (0, 0)))


def kernel(x_prompt, x_sample, cache_fox_k, cache_fox_v, cache_fox_logf, cache_dsa_k, cache_dsa_v,
           cache_dsa_kidx, page_table, w_in, b_f, w_o, ln1_g, ln1_b, w_router, b_router, w_gate, b_gate,
           w_up, b_up, w_down, b_down, ln2_g, ln2_b):
    batch, seq, d = x_prompt.shape
    dec_b, n_q, _ = x_sample.shape
    depth = w_in.shape[0]
    n_pool = cache_fox_k.shape[1]
    n_pages = page_table.shape[1]
    past = n_pages * PAGE_SIZE
    alpha = (2.0 * depth) ** 0.25
    rows_p, rows_s = batch * seq, dec_b * n_q

    cos_p, sin_p = _rope_tables(jnp.arange(seq))
    cos_s, sin_s = _rope_tables(past + (jnp.arange(rows_s) % n_q))

    xp = x_prompt.reshape(rows_p, d)
    xs_ = x_sample.reshape(rows_s, d)
    outs_p = [[] for _ in range(6)]
    outs_s = [[] for _ in range(6)]
    for l in range(depth):
        wp, bf_row = _prep_w_in(w_in[l], b_f[l])
        wo = w_o[l].astype(BF16)
        wr = jnp.pad(w_router[l], ((0, 0), (0, LANES - N_EXPERTS))).astype(BF16)
        br = jnp.full((1, LANES), NEG_INF, F32).at[0, :N_EXPERTS].set(b_router[l].astype(F32))
        g1, b1 = ln1_g[l].reshape(1, d), ln1_b[l].reshape(1, d)
        g2, b2 = ln2_g[l].reshape(1, d), ln2_b[l].reshape(1, d)

        pp = _in_proj(xp, wp, bf_row, cos_p, sin_p, batch, seq, tm=256)
        o_fox = _fox_prompt(pp["fqt"], pp["fkb"], pp["fvtc"], pp["cb"], batch, seq)
        o_dsa = _dsa_prompt(pp["iqt"], pp["iwt"], pp["ikb"], pp["dqt"], pp["dkb"], pp["dvtc"], batch, seq,
                            tq=256)

        ps = _in_proj(xs_, wp, bf_row, cos_s, sin_s, 1, rows_s, tm=rows_s)
        ps = {k: v[0] for k, v in ps.items() if v.ndim == 3}
        pool_fk = cache_fox_k[l].transpose(0, 2, 3, 1)
        pool_fv = cache_fox_v[l].transpose(0, 2, 3, 1)
        pool_lft = jnp.swapaxes(cache_fox_logf[l], 1, 2)
        pool_dk = cache_dsa_k[l].transpose(0, 2, 3, 1)
        pool_dv = cache_dsa_v[l].transpose(0, 2, 3, 1)
        pool_ki = jnp.swapaxes(cache_dsa_kidx[l], 1, 2)
        per_seq = lambda a: a.reshape(a.shape[0], dec_b, n_q).transpose(1, 0, 2)
        new_t = lambda a: jnp.pad(per_seq(a), ((0, 0), (0, 0), (0, PAGE_SIZE - n_q)))
        rows_q = lambda a: a.T.reshape(dec_b, n_q, a.shape[0])
        so_fox = _sample_fox(page_table, rows_q(ps["fqt"]), new_t(ps["fkt"]), new_t(ps["fvt"]),
                             new_t(ps["lft"]), pool_fk, pool_fv, pool_lft)
        half = SUBLANES // 2
        assert n_q <= half
        iqx = ps["iqt"].reshape(H_IDX, D_IDX, dec_b, n_q).transpose(2, 0, 3, 1)
        iqx = jnp.pad(iqx, ((0, 0), (0, 0), (0, half - n_q), (0, 0))).reshape(dec_b, H_IDX * half, D_IDX)
        iwx = jnp.pad(per_seq(ps["iwt"]), ((0, 0), (0, 0), (0, half - n_q)))
        iwx = iwx.reshape(dec_b, H_IDX * half, 1)
        sel_bias, sel_bias_new = _sample_index(page_table, iqx, iwx, new_t(ps["ikt"]), pool_ki, n_q)
        so_dsa = _sample_dsa(page_table, rows_q(ps["dqt"]), new_t(ps["dkt"]), new_t(ps["dvt"]), sel_bias,
                             sel_bias_new, pool_dk, pool_dv)

        cin = jnp.zeros((1, LANES), F32)
        x1_p, e_p, gate_p, rank_p, cnt_p = _tail(o_fox, o_dsa, xp, wo, g1, b1, wr, br, cin, alpha, tm=256)
        x1_s, e_s, gate_s, rank_s, cnt = _tail(so_fox.reshape(rows_s, W_FOX), so_dsa.reshape(rows_s, W_DSA),
                                               xs_, wo, g1, b1, wr, br, cnt_p, alpha, tm=rows_s)

        bs = EXPERT_BLOCK
        counts = cnt[0, :N_EXPERTS].astype(I32)
        padded = (counts + bs - 1) // bs * bs
        pad_end = jnp.cumsum(padded)
        pad_start = pad_end - padded
        n_assign = (rows_p + rows_s) * TOP_K_EXPERTS
        n_blocks = -(-(n_assign + N_EXPERTS * (bs - 1)) // bs)
        blk = jnp.arange(n_blocks, dtype=I32)
        n_used = (pad_end[-1] // bs).astype(I32)
        block_e = jnp.sum((pad_end[None, :] <= (blk * bs)[:, None]).astype(I32), axis=1)
        block_e = jnp.minimum(block_e, N_EXPERTS - 1)
        block_e = jnp.where(blk < n_used, block_e, block_e[jnp.maximum(n_used - 1, 0)])
        is_first = jnp.concatenate([jnp.ones((1,), I32), (block_e[1:] != block_e[:-1]).astype(I32)])
        dest_p = (pad_start[e_p[:, :TOP_K_EXPERTS]] + rank_p[:, :TOP_K_EXPERTS]).reshape(-1)
        dest_s = (pad_start[e_s[:, :TOP_K_EXPERTS]] + rank_s[:, :TOP_K_EXPERTS]).reshape(-1)

        xs_sorted = jnp.zeros((n_blocks * bs, d), F32)
        xs_sorted = _dispatch(dest_p, x1_p, xs_sorted)
        xs_sorted = _dispatch(dest_s, x1_s, xs_sorted)
        ys = _expert_ffn(block_e, is_first, n_used.reshape(1), xs_sorted, w_gate[l], w_up[l], w_down[l],
                         b_gate[l].reshape(N_EXPERTS, 1, -1), b_up[l].reshape(N_EXPERTS, 1, -1),
                         b_down[l].reshape(N_EXPERTS, 1, -1))
        xp_new = _combine(dest_p, ys, gate_p, x1_p, g2, b2, alpha)
        xs_new = _combine(dest_s, ys, gate_s, x1_s, g2, b2, alpha)

        heads_p = lambda a: a.reshape(batch, -1, HEAD_DIM, seq).transpose(0, 3, 1, 2)
        heads_s = lambda a: a.reshape(-1, HEAD_DIM, dec_b, n_q).transpose(2, 3, 0, 1)
        for acc, a in zip(outs_p, (heads_p(pp["fkt"]), heads_p(pp["fvt"]), pp["lft"].transpose(0, 2, 1),
                                   heads_p(pp["dkt"]), heads_p(pp["dvt"]), pp["ikt"].transpose(0, 2, 1))):
            acc.append(a)
        for acc, a in zip(outs_s, (heads_s(ps["fkt"]), heads_s(ps["fvt"]), per_seq(ps["lft"]).transpose(0, 2, 1),
                                   heads_s(ps["dkt"]), heads_s(ps["dvt"]), per_seq(ps["ikt"]).transpose(0, 2, 1))):
            acc.append(a)
        xp, xs_ = xp_new, xs_new

    stack = lambda lst: [jnp.stack(r, axis=0) for r in lst]
    return (xp.reshape(batch, seq, d), xs_.reshape(dec_b, n_q, d), *stack(outs_p), *stack(outs_s))
```

```python
import functools

import jax
import jax.numpy as jnp
from jax import lax
from jax.experimental import pallas as pl
from jax.experimental.pallas import tpu as pltpu

F32, BF16, I32 = jnp.float32, jnp.bfloat16, jnp.int32
NEG_INF = float("-inf")

HEAD_DIM = 64
H_FOX = 8
H_DSA = 8
W_FOX = H_FOX * HEAD_DIM
W_DSA = H_DSA * HEAD_DIM
H_IDX = 16
D_IDX = 64
TOPK_MAX = 256
PAGE_SIZE = 128
ROPE_THETA = 10000.0
ATTN_SCALE = HEAD_DIM ** -0.5
IDX_SCALE = D_IDX ** -0.5
IDX_W_SCALE = H_IDX ** -0.5
N_EXPERTS = 32
TOP_K_EXPERTS = 4
SWIGLU_LIMIT = 7.0
SWIGLU_ALPHA = 1.702
LN_EPS = 1e-5

LANES = 128
KEY_NEG_INF = -2139095041
INT_MIN = -2 ** 31
VMEM_LIMIT = 56 * 1024 * 1024

NT_DIMS = (((1,), (1,)), ((), ()))


def _cparams(sem):
    return pltpu.CompilerParams(dimension_semantics=sem, vmem_limit_bytes=VMEM_LIMIT)


def _split3(x):
    hi = x.astype(BF16)
    r1 = x - hi.astype(F32)
    mid = r1.astype(BF16)
    lo = (r1 - mid.astype(F32)).astype(BF16)
    return hi, mid, lo


def _lane_cumsum(x, upper):
    hi, mid, lo = _split3(x)
    return (jnp.dot(hi, upper, preferred_element_type=F32)
            + jnp.dot(mid, upper, preferred_element_type=F32)
            + jnp.dot(lo, upper, preferred_element_type=F32))


def _upper_tri(n):
    r = lax.broadcasted_iota(I32, (n, n), 0)
    c = lax.broadcasted_iota(I32, (n, n), 1)
    return (r <= c).astype(BF16)


def _sort_key(x):
    bits = pltpu.bitcast(x, I32)
    return bits ^ ((bits >> 31) & 0x7FFFFFFF)


C_FQ, C_FK, C_FV, C_DQ, C_DK, C_DV, C_IQ, C_SM = 0, 512, 1024, 1536, 2048, 2560, 3072, 4096
N_PROJ = 4224
SM_IK, SM_LF, SM_IW = 0, 64, 72


def _in_proj_kernel(x_ref, w_ref, cos_ref, sin_ref, bf_ref,
                    fqt_ref, fkb_ref, fkt_ref, fvt_ref, fvtc_ref, cb_ref, lft_ref,
                    dqt_ref, dkb_ref, dkt_ref, dvt_ref, dvtc_ref, iqt_ref, ikb_ref, ikt_ref, iwt_ref,
                    carry_ref, *, tiles_per_seq):
    @pl.when(pl.program_id(0) % tiles_per_seq == 0)
    def _():
        carry_ref[...] = jnp.zeros(carry_ref.shape, F32)

    xb = x_ref[...].astype(BF16)
    tm = xb.shape[0]
    cos = cos_ref[...]
    sin = sin_ref[...]
    lane = lax.broadcasted_iota(I32, (tm, LANES), 1)
    first_half = (lane & (HEAD_DIM - 1)) < HEAD_DIM // 2

    def mm(c0, n):
        return jnp.dot(xb, w_ref[:, c0:c0 + n], preferred_element_type=F32)

    def rope(y):
        outs = []
        for c in range(0, y.shape[1], LANES):
            yc = y[:, c:c + LANES]
            partner = jnp.where(first_half, pltpu.roll(yc, LANES - HEAD_DIM // 2, 1),
                                pltpu.roll(yc, HEAD_DIM // 2, 1))
            outs.append(yc * cos + partner * sin)
        return outs[0] if len(outs) == 1 else jnp.concatenate(outs, axis=1)

    y = mm(C_FQ, W_FOX)
    fqt_ref[...] = (y * ATTN_SCALE).T.astype(BF16)
    y = mm(C_FK, W_FOX)
    fkb_ref[...] = y.astype(BF16)
    fkt_ref[...] = y.T
    yt = mm(C_FV, W_FOX).T
    fvt_ref[...] = yt
    fvtc_ref[...] = yt.astype(BF16)
    y = rope(mm(C_DQ, W_DSA))
    dqt_ref[...] = (y * ATTN_SCALE).T.astype(BF16)
    y = rope(mm(C_DK, W_DSA))
    dkb_ref[...] = y.astype(BF16)
    dkt_ref[...] = y.T
    yt = mm(C_DV, W_DSA).T
    dvt_ref[...] = yt
    dvtc_ref[...] = yt.astype(BF16)
    y = rope(mm(C_IQ, H_IDX * D_IDX))
    iqt_ref[...] = (y * IDX_SCALE).T.astype(BF16)
    z = mm(C_SM, LANES)
    r = rope(z)
    ikb_ref[...] = r[:, SM_IK:SM_IK + D_IDX].astype(BF16)
    zb = z + bf_ref[...]
    lsg = -(jnp.maximum(-zb, 0.0) + jnp.log1p(jnp.exp(-jnp.abs(zb))))
    small_t = jnp.where(lane < SM_LF, r, jnp.where(lane < SM_IW, lsg, z * IDX_W_SCALE)).T
    ikt_ref[...] = small_t[SM_IK:SM_IK + D_IDX, :]
    lft_ref[...] = small_t[SM_LF:SM_LF + H_FOX, :]
    iwt_ref[...] = small_t[SM_IW:SM_IW + H_IDX, :]

    hi, mid, lo = _split3(jnp.where((lane >= SM_LF) & (lane < SM_IW), lsg, 0.0))
    rr = lax.broadcasted_iota(I32, (tm, tm), 0)
    cc = lax.broadcasted_iota(I32, (tm, tm), 1)
    lower = (cc <= rr).astype(BF16)
    parts = jnp.dot(lower, jnp.concatenate([hi, mid, lo], axis=1), preferred_element_type=F32)
    cs = parts[:, :LANES] + parts[:, LANES:2 * LANES] + parts[:, 2 * LANES:] + carry_ref[...]
    carry_ref[...] = cs[tm - 1:tm, :]
    for h in range(H_FOX):
        cb_ref[h] = jnp.broadcast_to(cs[:, SM_LF + h:SM_LF + h + 1], (tm, LANES))


FOX_KV_CHUNK = 256
DSA_KV_CHUNK = 512
VALUE_BISECTION_ROUNDS = 14


def _in_proj(x2d, wp, bf_row, cos_t, sin_t, n_seq, seq, tm):
    rows, d = x2d.shape
    tps = seq // tm
    ckf, ckd = min(FOX_KV_CHUNK, seq), min(DSA_KV_CHUNK, seq)
    n_tab = cos_t.shape[0] // tm
    t_spec = lambda w: pl.BlockSpec((None, w, tm), lambda i: (i // tps, 0, i % tps))
    n_spec = lambda w: pl.BlockSpec((tm, w), lambda i: (i, 0))
    c_spec = lambda w, ck: pl.BlockSpec((None, None, w, tm),
                                        lambda i: (i // tps, (i % tps) * tm // ck, 0, (i % tps) % (ck // tm)))
    t_shape = lambda w, dt: jax.ShapeDtypeStruct((n_seq, w, seq), dt)
    n_shape = lambda w, dt: jax.ShapeDtypeStruct((rows, w), dt)
    c_shape = lambda w, ck: jax.ShapeDtypeStruct((n_seq, seq // ck, w, ck), BF16)
    wi = H_IDX * D_IDX
    outs = [("fqt", t_spec(W_FOX), t_shape(W_FOX, BF16)), ("fkb", n_spec(W_FOX), n_shape(W_FOX, BF16)),
            ("fkt", t_spec(W_FOX), t_shape(W_FOX, F32)), ("fvt", t_spec(W_FOX), t_shape(W_FOX, F32)),
            ("fvtc", c_spec(W_FOX, ckf), c_shape(W_FOX, ckf)),
            ("cb", pl.BlockSpec((H_FOX, tm, LANES), lambda i: (0, i, 0)),
             jax.ShapeDtypeStruct((H_FOX, rows, LANES), F32)),
            ("lft", t_spec(H_FOX), t_shape(H_FOX, F32)),
            ("dqt", t_spec(W_DSA), t_shape(W_DSA, BF16)), ("dkb", n_spec(W_DSA), n_shape(W_DSA, BF16)),
            ("dkt", t_spec(W_DSA), t_shape(W_DSA, F32)), ("dvt", t_spec(W_DSA), t_shape(W_DSA, F32)),
            ("dvtc", c_spec(W_DSA, ckd), c_shape(W_DSA, ckd)),
            ("iqt", t_spec(wi), t_shape(wi, BF16)), ("ikb", n_spec(D_IDX), n_shape(D_IDX, BF16)),
            ("ikt", t_spec(D_IDX), t_shape(D_IDX, F32)), ("iwt", t_spec(H_IDX), t_shape(H_IDX, F32))]
    res = pl.pallas_call(
        functools.partial(_in_proj_kernel, tiles_per_seq=tps),
        grid=(rows // tm,),
        in_specs=[pl.BlockSpec((tm, d), lambda i: (i, 0)),
                  pl.BlockSpec((d, N_PROJ), lambda i: (0, 0)),
                  pl.BlockSpec((tm, LANES), lambda i: (i % n_tab, 0)),
                  pl.BlockSpec((tm, LANES), lambda i: (i % n_tab, 0)),
                  pl.BlockSpec((1, LANES), lambda i: (0, 0))],
        out_specs=[o[1] for o in outs],
        out_shape=[o[2] for o in outs],
        scratch_shapes=[pltpu.VMEM((1, LANES), F32)],
        compiler_params=_cparams(("arbitrary",)),
        name="in_proj",
    )(x2d, wp, cos_t, sin_t, bf_row)
    return {o[0]: r for o, r in zip(outs, res)}


SUBLANES = 8


def _over_sublanes(x8, reduce_fn):
    return jnp.broadcast_to(reduce_fn(x8, axis=0, keepdims=True), x8.shape)


def _by_sublane_tile(x):
    return x.reshape(x.shape[0] // SUBLANES, SUBLANES, x.shape[1])


def _flash_step(n_heads, score_fn, vt_fn, st):
    m_ref, l_ref, acc_ref, _, s_ref, p_ref, alpha_ref = st
    for h in range(n_heads):
        s_ref[h] = score_fn(h)
    for h in range(n_heads):
        s3 = _by_sublane_tile(s_ref[h])
        m_old = m_ref[h]
        m_new = jnp.maximum(m_old, _over_sublanes(jnp.max(s3, axis=0), jnp.max))
        alpha = jnp.exp(m_old - m_new)
        p3 = jnp.exp(s3 - m_new[None])
        l_ref[h] = alpha * l_ref[h] + _over_sublanes(jnp.sum(p3, axis=0), jnp.sum)
        p_ref[h] = p3.reshape(s_ref.shape[1:]).astype(BF16)
        alpha_ref[h] = alpha
        m_ref[h] = m_new
    for h in range(n_heads):
        pv = jnp.dot(vt_fn(h), p_ref[h], preferred_element_type=F32)
        acc_ref[h] = (_by_sublane_tile(acc_ref[h]) * alpha_ref[h][None]).reshape(acc_ref.shape[1:]) + pv


def _flash_init(m_ref, l_ref, acc_ref):
    m_ref[...] = jnp.full(m_ref.shape, -1e30, F32)
    l_ref[...] = jnp.zeros(l_ref.shape, F32)
    acc_ref[...] = jnp.zeros(acc_ref.shape, F32)


def _flash_finish(o_ref, l_ref, acc_ref, n_heads):
    tq = acc_ref.shape[2]
    for pair in range(n_heads // 2):
        parts = [(_by_sublane_tile(acc_ref[h]) / l_ref[h][None]).reshape(HEAD_DIM, tq)
                 for h in (2 * pair, 2 * pair + 1)]
        o_ref[:, pair * LANES:(pair + 1) * LANES] = jnp.concatenate(parts, axis=0).T.astype(BF16)


def _flash_scratch(n_heads, tk, tq):
    return [pltpu.VMEM((n_heads, SUBLANES, tq), F32), pltpu.VMEM((n_heads, SUBLANES, tq), F32),
            pltpu.VMEM((n_heads, HEAD_DIM, tq), F32), pltpu.VMEM((n_heads, LANES, tq), BF16),
            pltpu.VMEM((n_heads, tk, tq), F32), pltpu.VMEM((n_heads, tk, tq), BF16),
            pltpu.VMEM((n_heads, SUBLANES, tq), F32)]


def _stage_pair_masked_q(qt_ref, qpm_ref, n_heads):
    for h in range(n_heads):
        pair = h // 2
        rows = qt_ref[pair * LANES:(pair + 1) * LANES, :].astype(F32)
        r = lax.broadcasted_iota(I32, rows.shape, 0)
        qpm_ref[h] = jnp.where(r // HEAD_DIM == h % 2, rows, 0.0).astype(BF16)


def _fox_kernel(qt_ref, k_ref, vt_ref, cb_ref, o_ref, *st, tq):
    m_ref, l_ref, acc_ref, qpm_ref = st[:4]
    i = pl.program_id(1)
    _stage_pair_masked_q(qt_ref, qpm_ref, H_FOX)
    _flash_init(m_ref, l_ref, acc_ref)
    key_row = lax.broadcasted_iota(I32, (tq, tq), 0)
    query_col = lax.broadcasted_iota(I32, (tq, tq), 1)
    visible = key_row <= query_col

    def kv_step(j, masked):
        start = pl.multiple_of(j * tq, tq)

        def scores(h):
            pair = h // 2
            s = jnp.dot(k_ref[pl.ds(start, tq), pair * LANES:(pair + 1) * LANES], qpm_ref[h],
                        preferred_element_type=F32)
            s = s - jnp.concatenate([cb_ref[h, pl.ds(start, tq), :]] * (tq // LANES), axis=1)
            return jnp.where(visible, s, NEG_INF) if masked else s

        _flash_step(H_FOX, scores, lambda h: vt_ref[j, h * HEAD_DIM:(h + 1) * HEAD_DIM, :], st)

    def body(j, c):
        kv_step(j, False)
        return c

    lax.fori_loop(0, i, body, 0)
    kv_step(i, True)
    _flash_finish(o_ref, l_ref, acc_ref, H_FOX)


def _fox_prompt(qt, kb, vtc, cb, batch, seq):
    tq = FOX_KV_CHUNK
    nq = seq // tq
    return pl.pallas_call(
        functools.partial(_fox_kernel, tq=tq),
        grid=(batch, nq),
        in_specs=[pl.BlockSpec((None, W_FOX, tq), lambda b, i: (b, 0, i)),
                  pl.BlockSpec((seq, W_FOX), lambda b, i: (b, 0)),
                  pl.BlockSpec((None, nq, W_FOX, tq), lambda b, i: (b, 0, 0, 0)),
                  pl.BlockSpec((H_FOX, seq, LANES), lambda b, i: (0, b, 0))],
        out_specs=pl.BlockSpec((tq, W_FOX), lambda b, i: (b * nq + i, 0)),
        out_shape=jax.ShapeDtypeStruct((batch * seq, W_FOX), BF16),
        scratch_shapes=_flash_scratch(H_FOX, tq, tq),
        compiler_params=_cparams(("arbitrary", "arbitrary")),
        name="fox_prompt",
    )(qt, kb, vtc, cb)


def _search_threshold(count_ge, lo0, hi0, topk):
    def key_to_value(k):
        return pltpu.bitcast(k ^ ((k >> 31) & 0x7FFFFFFF), F32)

    def key_midpoint(lo, hi):
        return (lo >> 1) + (hi >> 1) + (lo & hi & 1)

    def guarded(lo, hi, value):
        mid = _sort_key(value)
        return jnp.where((mid > lo) & (mid < hi), mid, key_midpoint(lo, hi))

    def refine(lo, hi, n_lo, n_hi, mid):
        n_mid = count_ge(mid)
        take = n_mid >= topk
        return (jnp.where(take, mid, lo), jnp.where(take, hi, mid),
                jnp.where(take, n_mid, n_lo), jnp.where(take, n_hi, n_mid))

    def search_round(state):
        it, lo, hi, n_lo, n_hi, _ = state
        by_value = it < VALUE_BISECTION_ROUNDS
        for _ in range(2):
            mid_v = guarded(lo, hi, 0.5 * key_to_value(lo) + 0.5 * key_to_value(hi))
            lo, hi, n_lo, n_hi = refine(lo, hi, n_lo, n_hi, jnp.where(by_value, mid_v, key_midpoint(lo, hi)))
        active = (n_lo > topk) & (hi != lo + 1)
        return it + 1, lo, hi, n_lo, n_hi, jnp.max(active.astype(F32))

    def searching(state):
        return (state[0] < VALUE_BISECTION_ROUNDS + 17) & (state[5] > 0)

    n0 = count_ge(lo0)
    return lax.while_loop(searching, search_round,
                          (jnp.int32(0), lo0, hi0, n0, jnp.zeros_like(n0), jnp.max((n0 > topk).astype(F32))))[1]


def _dsa_kernel(iqt_ref, iwt_ref, ik_ref, qt_ref, k_ref, vt_ref, o_ref, key_ref, bias_ref, *st,
                tq, ck, topk):
    m_ref, l_ref, acc_ref, qpm_ref = st[:4]
    i = pl.program_id(1)
    q0 = i * tq
    n_chunks = (q0 + tq + ck - 1) // ck
    key_row = lax.broadcasted_iota(I32, (ck, tq), 0)
    query_pos = q0 + lax.broadcasted_iota(I32, (ck, tq), 1)
    iw8 = [jnp.broadcast_to(iwt_ref[h:h + 1, :], (SUBLANES, tq)) for h in range(H_IDX)]

    def score_chunk(c, _):
        start = pl.multiple_of(c * ck, ck)
        ikc = ik_ref[pl.ds(start, ck), :]
        sc = jnp.zeros((ck // SUBLANES, SUBLANES, tq), F32)
        for h in range(H_IDX):
            rel = jnp.dot(ikc, iqt_ref[h * D_IDX:(h + 1) * D_IDX, :], preferred_element_type=F32)
            sc = sc + _by_sublane_tile(jnp.maximum(rel, 0.0)) * iw8[h][None]
        sc = jnp.where(key_row + start <= query_pos, sc.reshape(ck, tq), NEG_INF)
        key_ref[c] = _sort_key(sc)
        return 0

    lax.fori_loop(0, n_chunks, score_chunk, 0)

    def count(pred):
        def body(c, acc):
            return acc + jnp.sum(pred(_by_sublane_tile(key_ref[c])).astype(F32), axis=0)
        return _over_sublanes(lax.fori_loop(0, n_chunks, body, jnp.zeros((SUBLANES, tq), F32)), jnp.sum)

    def extreme(pick, reduce_fn, fill):
        def body(c, acc):
            k = _by_sublane_tile(key_ref[c])
            return pick(acc, reduce_fn(jnp.where(k > KEY_NEG_INF, k, fill), axis=0))
        init = jnp.full((SUBLANES, tq), fill, I32)
        return _over_sublanes(lax.fori_loop(0, n_chunks, body, init), reduce_fn)

    lo0 = extreme(jnp.minimum, jnp.min, jnp.int32(2 ** 31 - 1))
    hi0 = extreme(jnp.maximum, jnp.max, jnp.int32(KEY_NEG_INF)) + 1

    thr = _search_threshold(lambda t: count(lambda k: k >= t[None]), lo0, hi0, topk)
    need = topk - count(lambda k: k > thr[None])
    ties = count(lambda k: k == thr[None])
    excess = jnp.max(jnp.where((ties > need) & (thr > KEY_NEG_INF), 1.0, 0.0)) > 0.0

    @pl.when(jnp.logical_not(excess))
    def _():
        def fill(c, _):
            k = _by_sublane_tile(key_ref[c])
            bias_ref[c] = jnp.where((k >= thr[None]) & (k > KEY_NEG_INF), 0.0, NEG_INF).reshape(ck, tq)
            return 0
        lax.fori_loop(0, n_chunks, fill, 0)

    @pl.when(excess)
    def _():
        rr = lax.broadcasted_iota(I32, (ck, ck), 0)
        cc = lax.broadcasted_iota(I32, (ck, ck), 1)
        strict_lower = (cc < rr).astype(BF16)

        def fill(c, before):
            k = _by_sublane_tile(key_ref[c])
            tie = k == thr[None]
            tie_f = tie.astype(F32)
            earlier = jnp.dot(strict_lower, tie_f.reshape(ck, tq).astype(BF16), preferred_element_type=F32)
            earlier = _by_sublane_tile(earlier) + before[None]
            sel = (k > thr[None]) | (tie & (earlier < need[None]))
            bias_ref[c] = jnp.where(sel & (k > KEY_NEG_INF), 0.0, NEG_INF).reshape(ck, tq)
            return before + _over_sublanes(jnp.sum(tie_f, axis=0), jnp.sum)
        lax.fori_loop(0, n_chunks, fill, jnp.zeros((SUBLANES, tq), F32))

    _stage_pair_masked_q(qt_ref, qpm_ref, H_DSA)
    _flash_init(m_ref, l_ref, acc_ref)

    def kv_step(c, carry):
        start = pl.multiple_of(c * ck, ck)
        def scores(h):
            pair = h // 2
            return jnp.dot(k_ref[pl.ds(start, ck), pair * LANES:(pair + 1) * LANES], qpm_ref[h],
                           preferred_element_type=F32) + bias_ref[c]

        _flash_step(H_DSA, scores, lambda h: vt_ref[c, h * HEAD_DIM:(h + 1) * HEAD_DIM, :], st)
        return carry

    lax.fori_loop(0, n_chunks, kv_step, 0)
    _flash_finish(o_ref, l_ref, acc_ref, H_DSA)


def _dsa_prompt(iqt, iwt, ikb, dqt, dkb, dvtc, batch, seq, tq):
    ck = dvtc.shape[3]
    nq = seq // tq
    topk = min(TOPK_MAX, seq // 4)
    return pl.pallas_call(
        functools.partial(_dsa_kernel, tq=tq, ck=ck, topk=topk),
        grid=(batch, nq),
        in_specs=[pl.BlockSpec((None, H_IDX * D_IDX, tq), lambda b, i: (b, 0, i)),
                  pl.BlockSpec((None, H_IDX, tq), lambda b, i: (b, 0, i)),
                  pl.BlockSpec((seq, D_IDX), lambda b, i: (b, 0)),
                  pl.BlockSpec((None, W_DSA, tq), lambda b, i: (b, 0, i)),
                  pl.BlockSpec((seq, W_DSA), lambda b, i: (b, 0)),
                  pl.BlockSpec((None, seq // ck, W_DSA, ck), lambda b, i: (b, 0, 0, 0))],
        out_specs=pl.BlockSpec((tq, W_DSA), lambda b, i: (b * nq + i, 0)),
        out_shape=jax.ShapeDtypeStruct((batch * seq, W_DSA), BF16),
        scratch_shapes=[pltpu.VMEM((seq // ck, ck, tq), I32), pltpu.VMEM((seq // ck, ck, tq), F32)]
        + _flash_scratch(H_DSA, ck, tq),
        compiler_params=_cparams(("arbitrary", "arbitrary")),
        name="dsa_prompt",
    )(iqt, iwt, ikb, dqt, dkb, dvtc)


KV_PAGES_PER_STEP = 16
IDX_PAGES_PER_STEP = 32


def _head_mask(rows, width):
    r = lax.broadcasted_iota(I32, (rows, width), 0)
    c = lax.broadcasted_iota(I32, (rows, width), 1)
    return (c // HEAD_DIM) == (r % H_FOX)


def _block_diag_q(q, n_q):
    hm = _head_mask(H_FOX, q.shape[1])
    qf = q.astype(F32)
    parts = [jnp.where(hm, jnp.broadcast_to(qf[t:t + 1, :], hm.shape), 0.0) for t in range(n_q)]
    return jnp.concatenate(parts, axis=0).astype(BF16)


def _rows_to_groups(x, n_q):
    return jnp.concatenate([jnp.broadcast_to(x[t:t + 1, :], (H_FOX, x.shape[1])) for t in range(n_q)], axis=0)


DECODE_SPLITS = 2


def _softmax_page(g, s, vt_bf16, m_ref, l_ref, acc_ref):
    m = m_ref[g]
    m_new = jnp.maximum(m, jnp.max(s, axis=1, keepdims=True))
    alpha = jnp.exp(m - m_new)
    p = jnp.exp(s - m_new)
    l_ref[g] = alpha * l_ref[g] + jnp.sum(p, axis=1, keepdims=True)
    acc_ref[g] = alpha * acc_ref[g] + lax.dot_general(p.astype(BF16), vt_bf16, NT_DIMS,
                                                      preferred_element_type=F32)
    m_ref[g] = m_new


def _decode_scratch(rows, w):
    g = DECODE_SPLITS
    return [pltpu.VMEM((g, rows, 1), F32), pltpu.VMEM((g, rows, 1), F32), pltpu.VMEM((g, rows, w), F32)]


def _stack_pages(refs):
    return jnp.concatenate([r[...].reshape(-1, PAGE_SIZE) for r in refs], axis=1).astype(BF16)


def _finish_decode(o_ref, m_ref, l_ref, acc_ref, n_q):
    m = m_ref[0]
    for g in range(1, DECODE_SPLITS):
        m = jnp.maximum(m, m_ref[g])
    l = jnp.zeros_like(m)
    acc = jnp.zeros(acc_ref.shape[1:], F32)
    for g in range(DECODE_SPLITS):
        w = jnp.exp(m_ref[g] - m)
        l = l + w * l_ref[g]
        acc = acc + w * acc_ref[g]
    o = acc / l
    o = jnp.where(_head_mask(o.shape[0], o.shape[1]), o, 0.0)
    o_ref[...] = jnp.sum(o.reshape(n_q, H_FOX, o.shape[1]), axis=1).astype(BF16)


def _sample_fox_kernel(pt_ref, q_ref, kn_ref, vn_ref, lfn_ref, *rest, n_q, pps):
    k_refs, v_refs, lf_refs = rest[:pps], rest[pps:2 * pps], rest[2 * pps:3 * pps]
    o_ref, m_ref, l_ref, acc_ref, cc_ref, qbd_ref = rest[3 * pps:]
    step = pl.program_id(1)
    rows = n_q * H_FOX

    @pl.when(step == 0)
    def _():
        m_ref[...] = jnp.full(m_ref.shape, -1e30, F32)
        l_ref[...] = jnp.zeros(l_ref.shape, F32)
        acc_ref[...] = jnp.zeros(acc_ref.shape, F32)
        cc_ref[...] = jnp.zeros(cc_ref.shape, F32)
        qbd_ref[...] = _block_diag_q(q_ref[...], n_q)

    upper = _upper_tri(PAGE_SIZE)

    within = _lane_cumsum(jnp.concatenate([r[...] for r in lf_refs], axis=0), upper)
    off = cc_ref[...]
    parts = []
    for p in range(pps):
        cp = within[p * H_FOX:(p + 1) * H_FOX] + off
        parts.append(cp)
        off = cp[:, PAGE_SIZE - 1:PAGE_SIZE]
    cc_ref[...] = off
    per = pps // DECODE_SPLITS
    for g in range(DECODE_SPLITS):
        pages = slice(g * per, (g + 1) * per)
        cbias = jnp.concatenate(parts[pages], axis=1)
        s = jnp.dot(qbd_ref[...], _stack_pages(k_refs[pages]), preferred_element_type=F32)
        s = s - jnp.concatenate([cbias] * n_q, axis=0)
        _softmax_page(g, s, _stack_pages(v_refs[pages]), m_ref, l_ref, acc_ref)

    @pl.when(step == pl.num_programs(1) - 1)
    def _():
        r = lax.broadcasted_iota(I32, (rows, PAGE_SIZE), 0)
        c = lax.broadcasted_iota(I32, (rows, PAGE_SIZE), 1)
        cn = _lane_cumsum(lfn_ref[...], upper) + cc_ref[...]
        sn = jnp.dot(qbd_ref[...], kn_ref[...].astype(BF16), preferred_element_type=F32)
        sn = sn - jnp.concatenate([cn] * n_q, axis=0)
        sn = jnp.where(c <= r // H_FOX, sn, NEG_INF)
        _softmax_page(0, sn, vn_ref[...].astype(BF16), m_ref, l_ref, acc_ref)
        _finish_decode(o_ref, m_ref, l_ref, acc_ref, n_q)


def _page_spec(block, p, pps):
    nd = len(block)
    return pl.BlockSpec((None,) + block, lambda b, s, pt: (pt[b, s * pps + p],) + (0,) * nd)


def _sample_fox(page_table, q, k_new, v_new, lft_new, pool_k, pool_v, pool_lft):
    dec_b, n_q, w = q.shape
    n_pages = page_table.shape[1]
    pps = min(KV_PAGES_PER_STEP, n_pages)
    rows = n_q * H_FOX
    seq_spec = lambda blk: pl.BlockSpec((None,) + blk, lambda b, s, pt: (b,) + (0,) * len(blk))
    kv_page = (H_FOX, HEAD_DIM, PAGE_SIZE)
    in_specs = ([seq_spec((n_q, w)), seq_spec((w, PAGE_SIZE)), seq_spec((w, PAGE_SIZE)),
                 seq_spec((H_FOX, PAGE_SIZE))]
                + [_page_spec(kv_page, p, pps) for p in range(pps)]
                + [_page_spec(kv_page, p, pps) for p in range(pps)]
                + [_page_spec((H_FOX, PAGE_SIZE), p, pps) for p in range(pps)])
    return pl.pallas_call(
        functools.partial(_sample_fox_kernel, n_q=n_q, pps=pps),
        grid_spec=pltpu.PrefetchScalarGridSpec(
            num_scalar_prefetch=1,
            grid=(dec_b, n_pages // pps),
            in_specs=in_specs,
            out_specs=pl.BlockSpec((None, n_q, w), lambda b, s, pt: (b, 0, 0)),
            scratch_shapes=_decode_scratch(rows, w) + [pltpu.VMEM((H_FOX, 1), F32),
                                                       pltpu.VMEM((rows, w), BF16)]),
        out_shape=jax.ShapeDtypeStruct((dec_b, n_q, w), BF16),
        compiler_params=_cparams(("arbitrary", "arbitrary")),
        name="sample_fox",
    )(page_table, q, k_new, v_new, lft_new, *([pool_k] * pps), *([pool_v] * pps), *([pool_lft] * pps))


def _sample_index_kernel(pt_ref, iq_ref, iw_ref, kn_ref, *rest, pps, n_q, topk):
    k_refs = rest[:pps]
    bias_ref, biasn_ref, key_ref, keyn_ref = rest[pps:]
    step = pl.program_id(1)
    n_steps = key_ref.shape[0]
    iq = iq_ref[...]
    iw = iw_ref[...]

    def scores(kt):
        rel = jnp.dot(iq, kt, preferred_element_type=F32)
        w = iw * jnp.maximum(rel, 0.0)
        by_parity = jnp.sum(w.reshape(H_IDX // 2, SUBLANES, kt.shape[1]), axis=0)
        return by_parity + pltpu.roll(by_parity, SUBLANES // 2, 0)

    key_ref[step] = _sort_key(scores(_stack_pages(k_refs)))

    @pl.when(step == n_steps - 1)
    def _():
        r = lax.broadcasted_iota(I32, (H_FOX, PAGE_SIZE), 0)
        c = lax.broadcasted_iota(I32, (H_FOX, PAGE_SIZE), 1)
        keyn_ref[...] = _sort_key(jnp.where((c <= r) & (r < n_q), scores(kn_ref[...].astype(BF16)), NEG_INF))

        def count(pred):
            past = jnp.sum(jnp.sum(pred(key_ref[...]).astype(F32), axis=0), axis=1, keepdims=True)
            return past + jnp.sum(pred(keyn_ref[...]).astype(F32), axis=1, keepdims=True)

        def extreme(reduce_fn, fill):
            past = reduce_fn(reduce_fn(jnp.where(key_ref[...] > KEY_NEG_INF, key_ref[...], fill), axis=0),
                             axis=1, keepdims=True)
            new = reduce_fn(jnp.where(keyn_ref[...] > KEY_NEG_INF, keyn_ref[...], fill), axis=1, keepdims=True)
            return reduce_fn(jnp.concatenate([past, new], axis=1), axis=1, keepdims=True)

        lo0 = extreme(jnp.min, jnp.int32(2 ** 31 - 1))
        hi0 = extreme(jnp.max, jnp.int32(KEY_NEG_INF)) + 1
        thr = _search_threshold(lambda t: count(lambda k: k >= t), lo0, hi0, topk)
        need = topk - count(lambda k: k > thr)
        ties = count(lambda k: k == thr)
        excess = jnp.max(jnp.where((ties > need) & (thr > KEY_NEG_INF), 1.0, 0.0)) > 0.0

        @pl.when(jnp.logical_not(excess))
        def _():
            k = key_ref[...]
            bias_ref[...] = jnp.where((k >= thr) & (k > KEY_NEG_INF), 0.0, NEG_INF)
            k = keyn_ref[...]
            biasn_ref[...] = jnp.where((k >= thr) & (k > KEY_NEG_INF), 0.0, NEG_INF)

        @pl.when(excess)
        def _():
            upper = _upper_tri(PAGE_SIZE)

            def tie_block(k, before):
                tie = k == thr
                incl = jnp.dot(tie.astype(BF16), upper, preferred_element_type=F32) + before
                sel = (k > thr) | (tie & (incl - tie.astype(F32) < need))
                return jnp.where(sel & (k > KEY_NEG_INF), 0.0, NEG_INF), incl[:, PAGE_SIZE - 1:PAGE_SIZE]

            def fill(c, before):
                for p in range(pps):
                    lanes = slice(p * PAGE_SIZE, (p + 1) * PAGE_SIZE)
                    b, before = tie_block(key_ref[c, :, lanes], before)
                    bias_ref[c, :, lanes] = b
                return before

            before = lax.fori_loop(0, n_steps, fill, jnp.zeros((H_FOX, 1), F32))
            biasn_ref[...] = tie_block(keyn_ref[...], before)[0]


def _sample_index(page_table, iqx, iwx, kidx_new, pool_kidx, n_q):
    dec_b = iqx.shape[0]
    n_pages = page_table.shape[1]
    pps = min(IDX_PAGES_PER_STEP, n_pages)
    n_steps = n_pages // pps
    step_w = pps * PAGE_SIZE
    topk = min(TOPK_MAX, (n_pages * PAGE_SIZE + n_q) // 4)
    seq_spec = lambda blk: pl.BlockSpec((None,) + blk, lambda b, s, pt: (b,) + (0,) * len(blk))
    iq_rows = iqx.shape[1]
    in_specs = ([seq_spec((iq_rows, D_IDX)), seq_spec((iq_rows, 1)), seq_spec((D_IDX, PAGE_SIZE))]
                + [_page_spec((D_IDX, PAGE_SIZE), p, pps) for p in range(pps)])
    return pl.pallas_call(
        functools.partial(_sample_index_kernel, pps=pps, n_q=n_q, topk=topk),
        grid_spec=pltpu.PrefetchScalarGridSpec(
            num_scalar_prefetch=1,
            grid=(dec_b, n_steps),
            in_specs=in_specs,
            out_specs=[pl.BlockSpec((None, n_steps, H_FOX, step_w), lambda b, s, pt: (b, 0, 0, 0)),
                       pl.BlockSpec((None, H_FOX, PAGE_SIZE), lambda b, s, pt: (b, 0, 0))],
            scratch_shapes=[pltpu.VMEM((n_steps, H_FOX, step_w), I32), pltpu.VMEM((H_FOX, PAGE_SIZE), I32)]),
        out_shape=[jax.ShapeDtypeStruct((dec_b, n_steps, H_FOX, step_w), F32),
                   jax.ShapeDtypeStruct((dec_b, H_FOX, PAGE_SIZE), F32)],
        compiler_params=_cparams(("arbitrary", "arbitrary")),
        name="sample_index",
    )(page_table, iqx, iwx, kidx_new, *([pool_kidx] * pps))


def _sample_dsa_kernel(pt_ref, q_ref, kn_ref, vn_ref, bias_ref, biasn_ref, *rest, n_q, pps):
    k_refs, v_refs = rest[:pps], rest[pps:2 * pps]
    o_ref, m_ref, l_ref, acc_ref, qbd_ref = rest[2 * pps:]
    step = pl.program_id(1)

    @pl.when(step == 0)
    def _():
        m_ref[...] = jnp.full(m_ref.shape, -1e30, F32)
        l_ref[...] = jnp.zeros(l_ref.shape, F32)
        acc_ref[...] = jnp.zeros(acc_ref.shape, F32)
        qbd_ref[...] = _block_diag_q(q_ref[...], n_q)

    def attend(g, kt, vt, bias):
        s = jnp.dot(qbd_ref[...], kt, preferred_element_type=F32) + _rows_to_groups(bias, n_q)
        _softmax_page(g, s, vt, m_ref, l_ref, acc_ref)

    per = pps // DECODE_SPLITS
    for g in range(DECODE_SPLITS):
        pages = slice(g * per, (g + 1) * per)
        attend(g, _stack_pages(k_refs[pages]), _stack_pages(v_refs[pages]),
               bias_ref[0, :, g * per * PAGE_SIZE:(g + 1) * per * PAGE_SIZE])

    @pl.when(step == pl.num_programs(1) - 1)
    def _():
        attend(0, kn_ref[...].astype(BF16), vn_ref[...].astype(BF16), biasn_ref[...])
        _finish_decode(o_ref, m_ref, l_ref, acc_ref, n_q)


def _sample_dsa(page_table, q, k_new, v_new, bias, bias_new, pool_k, pool_v):
    dec_b, n_q, w = q.shape
    n_pages = page_table.shape[1]
    pps = min(KV_PAGES_PER_STEP, n_pages)
    rows = n_q * H_DSA
    kv_page = (H_DSA, HEAD_DIM, PAGE_SIZE)
    per_bias_chunk = bias.shape[3] // (pps * PAGE_SIZE)
    seq_spec = lambda blk: pl.BlockSpec((None,) + blk, lambda b, s, pt: (b,) + (0,) * len(blk))
    in_specs = ([seq_spec((n_q, w)), seq_spec((w, PAGE_SIZE)), seq_spec((w, PAGE_SIZE)),
                 pl.BlockSpec((None, 1, H_FOX, pps * PAGE_SIZE),
                              lambda b, s, pt: (b, s // per_bias_chunk, 0, s % per_bias_chunk)),
                 seq_spec((H_FOX, PAGE_SIZE))]
                + [_page_spec(kv_page, p, pps) for p in range(pps)]
                + [_page_spec(kv_page, p, pps) for p in range(pps)])
    return pl.pallas_call(
        functools.partial(_sample_dsa_kernel, n_q=n_q, pps=pps),
        grid_spec=pltpu.PrefetchScalarGridSpec(
            num_scalar_prefetch=1,
            grid=(dec_b, n_pages // pps),
            in_specs=in_specs,
            out_specs=pl.BlockSpec((None, n_q, w), lambda b, s, pt: (b, 0, 0)),
            scratch_shapes=_decode_scratch(rows, w) + [pltpu.VMEM((rows, w), BF16)]),
        out_shape=jax.ShapeDtypeStruct((dec_b, n_q, w), BF16),
        compiler_params=_cparams(("arbitrary", "arbitrary")),
        name="sample_dsa",
    )(page_table, q, k_new, v_new, bias, bias_new, *([pool_k] * pps), *([pool_v] * pps))


ROUTE_LANES = 8


def _layer_norm(r, g, b):
    mu = jnp.mean(r, axis=1, keepdims=True)
    d = r - mu
    var = jnp.mean(d * d, axis=1, keepdims=True)
    return d * lax.rsqrt(var + LN_EPS) * g + b


def _tail_kernel(of_ref, od_ref, x_ref, wo_ref, g_ref, b_ref, wr_ref, br_ref, cin_ref,
                 x1_ref, e_ref, gate_ref, rank_ref, cnt_ref, carry_ref, *, alpha):
    i = pl.program_id(0)
    tm = x_ref.shape[0]

    @pl.when(i == 0)
    def _():
        carry_ref[...] = cin_ref[...]

    r = (alpha * x_ref[...]
         + jnp.dot(of_ref[...], wo_ref[0:W_FOX, :], preferred_element_type=F32)
         + jnp.dot(od_ref[...], wo_ref[W_FOX:W_FOX + W_DSA, :], preferred_element_type=F32))
    x1 = _layer_norm(r, g_ref[...], b_ref[...])
    x1_ref[...] = x1

    logits = jnp.dot(x1.astype(BF16), wr_ref[...], preferred_element_type=F32) + br_ref[...]
    lane = lax.broadcasted_iota(I32, (tm, LANES), 1)
    lane_f = lane.astype(F32)
    vals, idxs, hots = [], [], []
    lg = logits
    for _ in range(TOP_K_EXPERTS):
        m = jnp.max(lg, axis=1, keepdims=True)
        idx = jnp.min(jnp.where(lg == m, lane_f, float(LANES)), axis=1, keepdims=True)
        hot = lane_f == idx
        vals.append(m)
        idxs.append(idx)
        hots.append(hot)
        lg = jnp.where(hot, NEG_INF, lg)
    exps = [jnp.exp(v - vals[0]) for v in vals]
    den = exps[0]
    for e in exps[1:]:
        den = den + e
    sel = jnp.zeros((tm, LANES), F32)
    for hot in hots:
        sel = sel + hot.astype(F32)
    rr = lax.broadcasted_iota(I32, (tm, tm), 0)
    cc = lax.broadcasted_iota(I32, (tm, tm), 1)
    strict_lower = (cc < rr).astype(BF16)
    before = jnp.dot(strict_lower, sel.astype(BF16), preferred_element_type=F32) + carry_ref[...]
    e_out = jnp.zeros((tm, LANES), F32)
    g_out = jnp.zeros((tm, LANES), F32)
    r_out = jnp.zeros((tm, LANES), F32)
    for j in range(TOP_K_EXPERTS):
        rank = jnp.sum(jnp.where(hots[j], before, 0.0), axis=1, keepdims=True)
        e_out = jnp.where(lane == j, idxs[j], e_out)
        g_out = jnp.where(lane == j, exps[j] / den, g_out)
        r_out = jnp.where(lane == j, rank, r_out)
    e_ref[...] = e_out[:, :ROUTE_LANES].astype(I32)
    gate_ref[...] = g_out[:, :ROUTE_LANES]
    rank_ref[...] = r_out[:, :ROUTE_LANES].astype(I32)
    carry_ref[...] = carry_ref[...] + jnp.sum(sel, axis=0, keepdims=True)
    cnt_ref[...] = carry_ref[...]


def _tail(of, od, x2d, wo, g1, b1, wr, br, cin, alpha, tm):
    rows, d = x2d.shape
    row_spec = lambda w: pl.BlockSpec((tm, w), lambda i: (i, 0))
    full = lambda a: pl.BlockSpec(a.shape, lambda i: (0,) * a.ndim)
    return pl.pallas_call(
        functools.partial(_tail_kernel, alpha=alpha),
        grid=(rows // tm,),
        in_specs=[row_spec(W_FOX), row_spec(W_DSA), row_spec(d), full(wo), full(g1), full(b1),
                  full(wr), full(br), full(cin)],
        out_specs=[row_spec(d), row_spec(ROUTE_LANES), row_spec(ROUTE_LANES), row_spec(ROUTE_LANES),
                   pl.BlockSpec((1, LANES), lambda i: (0, 0))],
        out_shape=[jax.ShapeDtypeStruct((rows, d), F32), jax.ShapeDtypeStruct((rows, ROUTE_LANES), I32),
                   jax.ShapeDtypeStruct((rows, ROUTE_LANES), F32), jax.ShapeDtypeStruct((rows, ROUTE_LANES), I32),
                   jax.ShapeDtypeStruct((1, LANES), F32)],
        scratch_shapes=[pltpu.VMEM((1, LANES), F32)],
        compiler_params=_cparams(("arbitrary",)),
        name="tail_router",
    )(of, od, x2d, wo, g1, b1, wr, br, cin)


EXPERT_BLOCK = 256
DISPATCH_TOKENS = 128
COMBINE_UNROLL = 4


def _dispatch_kernel(dest_ref, x_ref, xs_in, xs_out, sem, *, td):
    del xs_in

    def row_copy(src_row, dst_row):
        return pltpu.make_async_copy(x_ref.at[pl.ds(src_row, 1), :], xs_out.at[pl.ds(dst_row, 1), :], sem)

    def issue(tt, c):
        for u in range(COMBINE_UNROLL):
            t = tt * COMBINE_UNROLL + u
            for j in range(TOP_K_EXPERTS):
                row_copy(t, dest_ref[t * TOP_K_EXPERTS + j]).start()
        return c

    lax.fori_loop(0, td // COMBINE_UNROLL, issue, 0)

    def drain(tt, c):
        for _ in range(COMBINE_UNROLL * TOP_K_EXPERTS):
            row_copy(0, 0).wait()
        return c

    lax.fori_loop(0, td // COMBINE_UNROLL, drain, 0)


def _dispatch(dest_flat, x1, xs):
    rows, d = x1.shape
    td = DISPATCH_TOKENS
    return pl.pallas_call(
        functools.partial(_dispatch_kernel, td=td),
        grid=(rows // td,),
        in_specs=[pl.BlockSpec((td * TOP_K_EXPERTS,), lambda i: (i,), memory_space=pltpu.SMEM),
                  pl.BlockSpec((td, d), lambda i: (i, 0)), pl.BlockSpec(memory_space=pl.ANY)],
        out_specs=pl.BlockSpec(memory_space=pl.ANY),
        out_shape=jax.ShapeDtypeStruct(xs.shape, xs.dtype),
        scratch_shapes=[pltpu.SemaphoreType.DMA(())],
        input_output_aliases={2: 0},
        compiler_params=_cparams(("arbitrary",)),
        name="moe_dispatch",
    )(dest_flat, x1, xs)


def _ffn_kernel(be_ref, first_ref, nused_ref, nexte_ref, slot_ref, xs_ref, wg_hbm, wu_hbm, wd_hbm,
                bg_ref, bu_ref, bd_ref, ys_ref, stage_ref, wgb_ref, wub_ref, wdb_ref, sem):
    b = pl.program_id(0)

    def weight_copies(e, slot):
        return [pltpu.make_async_copy(w.at[e], stage_ref.at[slot, k], sem.at[slot, k])
                for k, w in enumerate((wg_hbm, wu_hbm, wd_hbm))]

    @pl.when(b == 0)
    def _():
        for c in weight_copies(be_ref[0], 0):
            c.start()

    @pl.when(first_ref[b] == 1)
    def _():
        slot = slot_ref[b]
        for c in weight_copies(be_ref[b], slot):
            c.wait()

        @pl.when(nexte_ref[b] >= 0)
        def _():
            for c in weight_copies(nexte_ref[b], 1 - slot):
                c.start()

        wgb_ref[...] = stage_ref[slot, 0].astype(BF16)
        wub_ref[...] = stage_ref[slot, 1].astype(BF16)
        wdb_ref[...] = stage_ref[slot, 2].astype(BF16)

    @pl.when(b < nused_ref[0])
    def _():
        xb = xs_ref[...].astype(BF16)
        g = jnp.dot(xb, wgb_ref[...], preferred_element_type=F32) + bg_ref[...]
        u = jnp.dot(xb, wub_ref[...], preferred_element_type=F32) + bu_ref[...]
        g = jnp.minimum(g, SWIGLU_LIMIT)
        u = jnp.clip(u, -SWIGLU_LIMIT, SWIGLU_LIMIT)
        h = g * (1.0 / (1.0 + jnp.exp(-SWIGLU_ALPHA * g))) * (u + 1.0)
        ys_ref[...] = jnp.dot(h.astype(BF16), wdb_ref[...], preferred_element_type=F32) + bd_ref[...]

    @pl.when(b >= nused_ref[0])
    def _():
        ys_ref[...] = jnp.zeros(ys_ref.shape, F32)


def _expert_ffn(block_e, is_first, n_used, next_e, slot, xs, wg, wu, wd, bg, bu, bd):
    p, d = xs.shape
    bs = EXPERT_BLOCK
    de = wg.shape[2]
    assert d == de
    b_spec = lambda w: pl.BlockSpec((None, 1, w), lambda b, be, fi, nu, ne, sl: (be[b], 0, 0))
    hbm = pl.BlockSpec(memory_space=pl.ANY)
    return pl.pallas_call(
        _ffn_kernel,
        grid_spec=pltpu.PrefetchScalarGridSpec(
            num_scalar_prefetch=5,
            grid=(p // bs,),
            in_specs=[pl.BlockSpec((bs, d), lambda b, be, fi, nu, ne, sl: (b, 0)),
                      hbm, hbm, hbm, b_spec(de), b_spec(de), b_spec(d)],
            out_specs=pl.BlockSpec((bs, d), lambda b, be, fi, nu, ne, sl: (b, 0)),
            scratch_shapes=[pltpu.VMEM((2, 3, d, de), F32),
                            pltpu.VMEM((d, de), BF16), pltpu.VMEM((d, de), BF16), pltpu.VMEM((de, d), BF16),
                            pltpu.SemaphoreType.DMA((2, 3))]),
        out_shape=jax.ShapeDtypeStruct((p, d), F32),
        compiler_params=_cparams(("arbitrary",)),
        name="moe_ffn",
    )(block_e, is_first, n_used, next_e, slot, xs, wg, wu, wd, bg, bu, bd)


def _combine_kernel(dest_ref, dest_next_ref, ys_hbm, gate_ref, x1_ref, g_ref, b_ref, o_ref, buf_ref, sem,
                    *, tc, alpha):
    i = pl.program_id(0)
    slot = i % 2

    def row_copy(src_row, sl, j, t):
        return pltpu.make_async_copy(ys_hbm.at[pl.ds(src_row, 1), :], buf_ref.at[sl, j, pl.ds(t, 1), :],
                                     sem.at[sl])

    def issue(d_ref, sl):
        def body(tt, c):
            for u in range(COMBINE_UNROLL):
                t = tt * COMBINE_UNROLL + u
                for j in range(TOP_K_EXPERTS):
                    row_copy(d_ref[t * TOP_K_EXPERTS + j], sl, j, t).start()
            return c
        lax.fori_loop(0, tc // COMBINE_UNROLL, body, 0)

    @pl.when(i == 0)
    def _():
        issue(dest_ref, slot)

    @pl.when(i + 1 < pl.num_programs(0))
    def _():
        issue(dest_next_ref, 1 - slot)

    def drain(tt, c):
        for _ in range(COMBINE_UNROLL * TOP_K_EXPERTS):
            row_copy(0, slot, 0, 0).wait()
        return c

    lax.fori_loop(0, tc // COMBINE_UNROLL, drain, 0)

    gates = gate_ref[...]
    moe = gates[:, 0:1] * buf_ref[slot, 0]
    for j in range(1, TOP_K_EXPERTS):
        moe = moe + gates[:, j:j + 1] * buf_ref[slot, j]
    o_ref[...] = _layer_norm(alpha * x1_ref[...] + moe, g_ref[...], b_ref[...])


def _combine(dest_flat, ys, gates, x1, g2, b2, alpha):
    rows, d = x1.shape
    tc = DISPATCH_TOKENS
    last = rows // tc - 1
    return pl.pallas_call(
        functools.partial(_combine_kernel, tc=tc, alpha=alpha),
        grid=(rows // tc,),
        in_specs=[pl.BlockSpec((tc * TOP_K_EXPERTS,), lambda i: (i,), memory_space=pltpu.SMEM),
                  pl.BlockSpec((tc * TOP_K_EXPERTS,), lambda i: (jnp.minimum(i + 1, last),),
                               memory_space=pltpu.SMEM),
                  pl.BlockSpec(memory_space=pl.ANY),
                  pl.BlockSpec((tc, ROUTE_LANES), lambda i: (i, 0)),
                  pl.BlockSpec((tc, d), lambda i: (i, 0)),
                  pl.BlockSpec((1, d), lambda i: (0, 0)),
                  pl.BlockSpec((1, d), lambda i: (0, 0))],
        out_specs=pl.BlockSpec((tc, d), lambda i: (i, 0)),
        out_shape=jax.ShapeDtypeStruct((rows, d), F32),
        scratch_shapes=[pltpu.VMEM((2, TOP_K_EXPERTS, tc, d), F32), pltpu.SemaphoreType.DMA((2,))],
        compiler_params=_cparams(("arbitrary",)),
        name="moe_combine",
    )(dest_flat, dest_flat, ys, gates, x1, g2, b2)


def _prep_w_in(w, b_f):
    d = w.shape[0]
    sizes = (W_FOX, W_FOX, W_FOX, H_FOX, W_DSA, W_DSA, W_DSA, H_IDX * D_IDX, D_IDX, H_IDX)
    parts, acc = [], 0
    for s in sizes:
        parts.append(w[:, acc:acc + s])
        acc += s
    fq, fk, fv, fl, dq, dk, dv, iq, ik, iw = parts
    small = jnp.concatenate([ik, fl, iw, jnp.zeros((d, LANES - D_IDX - H_FOX - H_IDX), w.dtype)], axis=1)
    wp = jnp.concatenate([fq, fk, fv, dq, dk, dv, iq, small], axis=1).astype(BF16)
    bf_row = jnp.zeros((1, LANES), F32).at[0, SM_LF:SM_LF + H_FOX].set(b_f.astype(F32))
    return wp, bf_row


def _rope_tables(pos):
    half = HEAD_DIM // 2
    inv = ROPE_THETA ** (-jnp.arange(half, dtype=F32) / half)
    ang = pos.astype(F32)[:, None] * inv[None, :]
    cos, sin = jnp.cos(ang), jnp.sin(ang)
    cos_t = jnp.concatenate([cos, cos, cos, cos], axis=1)
    sin_t = jnp.concatenate([-sin, sin, -sin, sin], axis=1)
    return cos_t, sin_t


def _pad_rows(a, n):
    return jnp.pad(a, ((0, 0), (0, n - a.shape[1]), (0, 0)))


def kernel(x_prompt, x_sample, cache_fox_k, cache_fox_v, cache_fox_logf, cache_dsa_k, cache_dsa_v,
           cache_dsa_kidx, page_table, w_in, b_f, w_o, ln1_g, ln1_b, w_router, b_router, w_gate, b_gate,
           w_up, b_up, w_down, b_down, ln2_g, ln2_b):
    batch, seq, d = x_prompt.shape
    dec_b, n_q, _ = x_sample.shape
    depth = w_in.shape[0]
    n_pool = cache_fox_k.shape[1]
    n_pages = page_table.shape[1]
    past = n_pages * PAGE_SIZE
    alpha = (2.0 * depth) ** 0.25
    rows_p, rows_s = batch * seq, dec_b * n_q

    cos_p, sin_p = _rope_tables(jnp.arange(seq))
    cos_s, sin_s = _rope_tables(past + (jnp.arange(rows_s) % n_q))

    xp = x_prompt.reshape(rows_p, d)
    xs_ = x_sample.reshape(rows_s, d)
    outs_p = [[] for _ in range(6)]
    outs_s = [[] for _ in range(6)]
    for l in range(depth):
        wp, bf_row = _prep_w_in(w_in[l], b_f[l])
        wo = w_o[l].astype(BF16)
        wr = jnp.pad(w_router[l], ((0, 0), (0, LANES - N_EXPERTS))).astype(BF16)
        br = jnp.full((1, LANES), NEG_INF, F32).at[0, :N_EXPERTS].set(b_router[l].astype(F32))
        g1, b1 = ln1_g[l].reshape(1, d), ln1_b[l].reshape(1, d)
        g2, b2 = ln2_g[l].reshape(1, d), ln2_b[l].reshape(1, d)

        pp = _in_proj(xp, wp, bf_row, cos_p, sin_p, batch, seq, tm=256)
        o_fox = _fox_prompt(pp["fqt"], pp["fkb"], pp["fvtc"], pp["cb"], batch, seq)
        o_dsa = _dsa_prompt(pp["iqt"], pp["iwt"], pp["ikb"], pp["dqt"], pp["dkb"], pp["dvtc"], batch, seq,
                            tq=256)

        ps = _in_proj(xs_, wp, bf_row, cos_s, sin_s, 1, rows_s, tm=rows_s)
        ps = {k: v[0] for k, v in ps.items() if v.ndim == 3}
        pool_fk = cache_fox_k[l].transpose(0, 2, 3, 1)
        pool_fv = cache_fox_v[l].transpose(0, 2, 3, 1)
        pool_lft = jnp.swapaxes(cache_fox_logf[l], 1, 2)
        pool_dk = cache_dsa_k[l].transpose(0, 2, 3, 1)
        pool_dv = cache_dsa_v[l].transpose(0, 2, 3, 1)
        pool_ki = jnp.swapaxes(cache_dsa_kidx[l], 1, 2)
        per_seq = lambda a: a.reshape(a.shape[0], dec_b, n_q).transpose(1, 0, 2)
        new_t = lambda a: jnp.pad(per_seq(a), ((0, 0), (0, 0), (0, PAGE_SIZE - n_q)))
        rows_q = lambda a: a.T.reshape(dec_b, n_q, a.shape[0])
        so_fox = _sample_fox(page_table, rows_q(ps["fqt"]), new_t(ps["fkt"]), new_t(ps["fvt"]),
                             new_t(ps["lft"]), pool_fk, pool_fv, pool_lft)
        half = SUBLANES // 2
        assert n_q <= half
        iqx = ps["iqt"].reshape(H_IDX, D_IDX, dec_b, n_q).transpose(2, 0, 3, 1)
        iqx = jnp.pad(iqx, ((0, 0), (0, 0), (0, half - n_q), (0, 0))).reshape(dec_b, H_IDX * half, D_IDX)
        iwx = jnp.pad(per_seq(ps["iwt"]), ((0, 0), (0, 0), (0, half - n_q)))
        iwx = iwx.reshape(dec_b, H_IDX * half, 1)
        sel_bias, sel_bias_new = _sample_index(page_table, iqx, iwx, new_t(ps["ikt"]), pool_ki, n_q)
        so_dsa = _sample_dsa(page_table, rows_q(ps["dqt"]), new_t(ps["dkt"]), new_t(ps["dvt"]), sel_bias,
                             sel_bias_new, pool_dk, pool_dv)

        cin = jnp.zeros((1, LANES), F32)
        x1_p, e_p, gate_p, rank_p, cnt_p = _tail(o_fox, o_dsa, xp, wo, g1, b1, wr, br, cin, alpha, tm=256)
        x1_s, e_s, gate_s, rank_s, cnt = _tail(so_fox.reshape(rows_s, W_FOX), so_dsa.reshape(rows_s, W_DSA),
                                               xs_, wo, g1, b1, wr, br, cnt_p, alpha, tm=rows_s)

        bs = EXPERT_BLOCK
        counts = cnt[0, :N_EXPERTS].astype(I32)
        padded = (counts + bs - 1) // bs * bs
        pad_end = jnp.cumsum(padded)
        pad_start = pad_end - padded
        n_assign = (rows_p + rows_s) * TOP_K_EXPERTS
        n_blocks = -(-(n_assign + N_EXPERTS * (bs - 1)) // bs)
        blk = jnp.arange(n_blocks, dtype=I32)
        n_used = (pad_end[-1] // bs).astype(I32)
        block_e = jnp.sum((pad_end[None, :] <= (blk * bs)[:, None]).astype(I32), axis=1)
        block_e = jnp.minimum(block_e, N_EXPERTS - 1)
        block_e = jnp.where(blk < n_used, block_e, block_e[jnp.maximum(n_used - 1, 0)])
        is_first = jnp.concatenate([jnp.ones((1,), I32), (block_e[1:] != block_e[:-1]).astype(I32)])
        change_at = jnp.where(is_first == 1, blk, n_blocks)
        next_change = jnp.concatenate([lax.cummin(change_at[::-1])[::-1][1:], jnp.full((1,), n_blocks, I32)])
        next_e = jnp.where(next_change < n_blocks, block_e[jnp.minimum(next_change, n_blocks - 1)], -1).astype(I32)
        slot = ((jnp.cumsum(is_first) - 1) % 2).astype(I32)
        dest_p = (pad_start[e_p[:, :TOP_K_EXPERTS]] + rank_p[:, :TOP_K_EXPERTS]).reshape(-1)
        dest_s = (pad_start[e_s[:, :TOP_K_EXPERTS]] + rank_s[:, :TOP_K_EXPERTS]).reshape(-1)

        xs_sorted = jnp.zeros((n_blocks * bs, d), F32)
        xs_sorted = _dispatch(dest_p, x1_p, xs_sorted)
        xs_sorted = _dispatch(dest_s, x1_s, xs_sorted)
        ys = _expert_ffn(block_e, is_first, n_used.reshape(1), next_e, slot, xs_sorted, w_gate[l], w_up[l], w_down[l],
                         b_gate[l].reshape(N_EXPERTS, 1, -1), b_up[l].reshape(N_EXPERTS, 1, -1),
                         b_down[l].reshape(N_EXPERTS, 1, -1))
        xp_new = _combine(dest_p, ys, gate_p, x1_p, g2, b2, alpha)
        xs_new = _combine(dest_s, ys, gate_s, x1_s, g2, b2, alpha)

        heads_p = lambda a: a.reshape(batch, -1, HEAD_DIM, seq).transpose(0, 3, 1, 2)
        heads_s = lambda a: a.reshape(-1, HEAD_DIM, dec_b, n_q).transpose(2, 3, 0, 1)
        for acc, a in zip(outs_p, (heads_p(pp["fkt"]), heads_p(pp["fvt"]), pp["lft"].transpose(0, 2, 1),
                                   heads_p(pp["dkt"]), heads_p(pp["dvt"]), pp["ikt"].transpose(0, 2, 1))):
            acc.append(a)
        for acc, a in zip(outs_s, (heads_s(ps["fkt"]), heads_s(ps["fvt"]), per_seq(ps["lft"]).transpose(0, 2, 1),
                                   heads_s(ps["dkt"]), heads_s(ps["dvt"]), per_seq(ps["ikt"]).transpose(0, 2, 1))):
            acc.append(a)
        xp, xs_ = xp_new, xs_new

    stack = lambda lst: [jnp.stack(r, axis=0) for r in lst]
    return (xp.reshape(batch, seq, d), xs_.reshape(dec_b, n_q, d), *stack(outs_p), *stack(outs_s))
```

```python
import functools

import jax
import jax.numpy as jnp
from jax import lax
from jax.experimental import pallas as pl
from jax.experimental.pallas import tpu as pltpu

F32, BF16, I32 = jnp.float32, jnp.bfloat16, jnp.int32
NEG_INF = float("-inf")

HEAD_DIM = 64
H_FOX = 8
H_DSA = 8
W_FOX = H_FOX * HEAD_DIM
W_DSA = H_DSA * HEAD_DIM
H_IDX = 16
D_IDX = 64
TOPK_MAX = 256
PAGE_SIZE = 128
ROPE_THETA = 10000.0
ATTN_SCALE = HEAD_DIM ** -0.5
IDX_SCALE = D_IDX ** -0.5
IDX_W_SCALE = H_IDX ** -0.5
N_EXPERTS = 32
TOP_K_EXPERTS = 4
SWIGLU_LIMIT = 7.0
SWIGLU_ALPHA = 1.702
LN_EPS = 1e-5

LANES = 128
KEY_NEG_INF = -2139095041
VMEM_LIMIT = 56 * 1024 * 1024

NT_DIMS = (((1,), (1,)), ((), ()))


def _cparams(sem):
    return pltpu.CompilerParams(dimension_semantics=sem, vmem_limit_bytes=VMEM_LIMIT)


def _split3(x):
    hi = x.astype(BF16)
    r1 = x - hi.astype(F32)
    mid = r1.astype(BF16)
    lo = (r1 - mid.astype(F32)).astype(BF16)
    return hi, mid, lo


def _lane_cumsum(x, upper):
    hi, mid, lo = _split3(x)
    return (jnp.dot(hi, upper, preferred_element_type=F32)
            + jnp.dot(mid, upper, preferred_element_type=F32)
            + jnp.dot(lo, upper, preferred_element_type=F32))


def _upper_tri(n):
    r = lax.broadcasted_iota(I32, (n, n), 0)
    c = lax.broadcasted_iota(I32, (n, n), 1)
    return (r <= c).astype(BF16)


def _sort_key(x):
    bits = pltpu.bitcast(x, I32)
    return bits ^ ((bits >> 31) & 0x7FFFFFFF)


C_FQ, C_FK, C_FV, C_DQ, C_DK, C_DV, C_IQ, C_SM = 0, 512, 1024, 1536, 2048, 2560, 3072, 4096
N_PROJ = 4224
SM_IK, SM_LF, SM_IW = 0, 64, 72


def _in_proj_kernel(x_ref, w_ref, cos_ref, sin_ref, bf_ref,
                    fqt_ref, fkb_ref, fkt_ref, fvt_ref, fvtc_ref, cb_ref, lft_ref,
                    dqt_ref, dkb_ref, dkt_ref, dvt_ref, dvtc_ref, iqt_ref, ikb_ref, ikt_ref, iwt_ref,
                    carry_ref, *, tiles_per_seq):
    @pl.when(pl.program_id(0) % tiles_per_seq == 0)
    def _():
        carry_ref[...] = jnp.zeros(carry_ref.shape, F32)

    xb = x_ref[...].astype(BF16)
    tm = xb.shape[0]
    cos = cos_ref[...]
    sin = sin_ref[...]
    lane = lax.broadcasted_iota(I32, (tm, LANES), 1)
    first_half = (lane & (HEAD_DIM - 1)) < HEAD_DIM // 2

    def mm(c0, n):
        return jnp.dot(xb, w_ref[:, c0:c0 + n], preferred_element_type=F32)

    def rope(y):
        outs = []
        for c in range(0, y.shape[1], LANES):
            yc = y[:, c:c + LANES]
            partner = jnp.where(first_half, pltpu.roll(yc, LANES - HEAD_DIM // 2, 1),
                                pltpu.roll(yc, HEAD_DIM // 2, 1))
            outs.append(yc * cos + partner * sin)
        return outs[0] if len(outs) == 1 else jnp.concatenate(outs, axis=1)

    y = mm(C_FQ, W_FOX)
    fqt_ref[...] = (y * ATTN_SCALE).T.astype(BF16)
    y = mm(C_FK, W_FOX)
    fkb_ref[...] = y.astype(BF16)
    fkt_ref[...] = y.T
    yt = mm(C_FV, W_FOX).T
    fvt_ref[...] = yt
    fvtc_ref[...] = yt.astype(BF16)
    y = rope(mm(C_DQ, W_DSA))
    dqt_ref[...] = (y * ATTN_SCALE).T.astype(BF16)
    y = rope(mm(C_DK, W_DSA))
    dkb_ref[...] = y.astype(BF16)
    dkt_ref[...] = y.T
    yt = mm(C_DV, W_DSA).T
    dvt_ref[...] = yt
    dvtc_ref[...] = yt.astype(BF16)
    y = rope(mm(C_IQ, H_IDX * D_IDX))
    iqt_ref[...] = (y * IDX_SCALE).T.astype(BF16)
    z = mm(C_SM, LANES)
    r = rope(z)
    ikb_ref[...] = r[:, SM_IK:SM_IK + D_IDX].astype(BF16)
    zb = z + bf_ref[...]
    lsg = -(jnp.maximum(-zb, 0.0) + jnp.log1p(jnp.exp(-jnp.abs(zb))))
    small_t = jnp.where(lane < SM_LF, r, jnp.where(lane < SM_IW, lsg, z * IDX_W_SCALE)).T
    ikt_ref[...] = small_t[SM_IK:SM_IK + D_IDX, :]
    lft_ref[...] = small_t[SM_LF:SM_LF + H_FOX, :]
    iwt_ref[...] = small_t[SM_IW:SM_IW + H_IDX, :]

    hi, mid, lo = _split3(jnp.where((lane >= SM_LF) & (lane < SM_IW), lsg, 0.0))
    rr = lax.broadcasted_iota(I32, (tm, tm), 0)
    cc = lax.broadcasted_iota(I32, (tm, tm), 1)
    lower = (cc <= rr).astype(BF16)
    parts = jnp.dot(lower, jnp.concatenate([hi, mid, lo], axis=1), preferred_element_type=F32)
    cs = parts[:, :LANES] + parts[:, LANES:2 * LANES] + parts[:, 2 * LANES:] + carry_ref[...]
    carry_ref[...] = cs[tm - 1:tm, :]
    for h in range(H_FOX):
        cb_ref[h] = jnp.broadcast_to(cs[:, SM_LF + h:SM_LF + h + 1], (tm, LANES))


FOX_KV_CHUNK = 256
DSA_KV_CHUNK = 512
VALUE_BISECTION_ROUNDS = 14


def _in_proj(x2d, wp, bf_row, cos_t, sin_t, n_seq, seq, tm):
    rows, d = x2d.shape
    tps = seq // tm
    ckf, ckd = min(FOX_KV_CHUNK, seq), min(DSA_KV_CHUNK, seq)
    n_tab = cos_t.shape[0] // tm
    t_spec = lambda w: pl.BlockSpec((None, w, tm), lambda i: (i // tps, 0, i % tps))
    n_spec = lambda w: pl.BlockSpec((tm, w), lambda i: (i, 0))
    c_spec = lambda w, ck: pl.BlockSpec((None, None, w, tm),
                                        lambda i: (i // tps, (i % tps) * tm // ck, 0, (i % tps) % (ck // tm)))
    t_shape = lambda w, dt: jax.ShapeDtypeStruct((n_seq, w, seq), dt)
    n_shape = lambda w, dt: jax.ShapeDtypeStruct((rows, w), dt)
    c_shape = lambda w, ck: jax.ShapeDtypeStruct((n_seq, seq // ck, w, ck), BF16)
    wi = H_IDX * D_IDX
    outs = [("fqt", t_spec(W_FOX), t_shape(W_FOX, BF16)), ("fkb", n_spec(W_FOX), n_shape(W_FOX, BF16)),
            ("fkt", t_spec(W_FOX), t_shape(W_FOX, F32)), ("fvt", t_spec(W_FOX), t_shape(W_FOX, F32)),
            ("fvtc", c_spec(W_FOX, ckf), c_shape(W_FOX, ckf)),
            ("cb", pl.BlockSpec((H_FOX, tm, LANES), lambda i: (0, i, 0)),
             jax.ShapeDtypeStruct((H_FOX, rows, LANES), F32)),
            ("lft", t_spec(H_FOX), t_shape(H_FOX, F32)),
            ("dqt", t_spec(W_DSA), t_shape(W_DSA, BF16)), ("dkb", n_spec(W_DSA), n_shape(W_DSA, BF16)),
            ("dkt", t_spec(W_DSA), t_shape(W_DSA, F32)), ("dvt", t_spec(W_DSA), t_shape(W_DSA, F32)),
            ("dvtc", c_spec(W_DSA, ckd), c_shape(W_DSA, ckd)),
            ("iqt", t_spec(wi), t_shape(wi, BF16)), ("ikb", n_spec(D_IDX), n_shape(D_IDX, BF16)),
            ("ikt", t_spec(D_IDX), t_shape(D_IDX, F32)), ("iwt", t_spec(H_IDX), t_shape(H_IDX, F32))]
    res = pl.pallas_call(
        functools.partial(_in_proj_kernel, tiles_per_seq=tps),
        grid=(rows // tm,),
        in_specs=[pl.BlockSpec((tm, d), lambda i: (i, 0)),
                  pl.BlockSpec((d, N_PROJ), lambda i: (0, 0)),
                  pl.BlockSpec((tm, LANES), lambda i: (i % n_tab, 0)),
                  pl.BlockSpec((tm, LANES), lambda i: (i % n_tab, 0)),
                  pl.BlockSpec((1, LANES), lambda i: (0, 0))],
        out_specs=[o[1] for o in outs],
        out_shape=[o[2] for o in outs],
        scratch_shapes=[pltpu.VMEM((1, LANES), F32)],
        compiler_params=_cparams(("arbitrary",)),
        name="in_proj",
    )(x2d, wp, cos_t, sin_t, bf_row)
    return {o[0]: r for o, r in zip(outs, res)}


SUBLANES = 8


def _over_sublanes(x8, reduce_fn):
    return jnp.broadcast_to(reduce_fn(x8, axis=0, keepdims=True), x8.shape)


def _by_sublane_tile(x):
    return x.reshape(x.shape[0] // SUBLANES, SUBLANES, x.shape[1])


def _flash_step(n_heads, score_fn, vt_fn, st):
    m_ref, l_ref, acc_ref, _, s_ref, p_ref, alpha_ref = st
    for h in range(n_heads):
        s_ref[h] = score_fn(h)
    for h in range(n_heads):
        s3 = _by_sublane_tile(s_ref[h])
        m_old = m_ref[h]
        m_new = jnp.maximum(m_old, _over_sublanes(jnp.max(s3, axis=0), jnp.max))
        alpha = jnp.exp(m_old - m_new)
        p3 = jnp.exp(s3 - m_new[None])
        l_ref[h] = alpha * l_ref[h] + _over_sublanes(jnp.sum(p3, axis=0), jnp.sum)
        p_ref[h] = p3.reshape(s_ref.shape[1:]).astype(BF16)
        alpha_ref[h] = alpha
        m_ref[h] = m_new
    for h in range(n_heads):
        pv = jnp.dot(vt_fn(h), p_ref[h], preferred_element_type=F32)
        acc_ref[h] = (_by_sublane_tile(acc_ref[h]) * alpha_ref[h][None]).reshape(acc_ref.shape[1:]) + pv


def _flash_init(m_ref, l_ref, acc_ref):
    m_ref[...] = jnp.full(m_ref.shape, -1e30, F32)
    l_ref[...] = jnp.zeros(l_ref.shape, F32)
    acc_ref[...] = jnp.zeros(acc_ref.shape, F32)


def _flash_finish(o_ref, l_ref, acc_ref, n_heads):
    tq = acc_ref.shape[2]
    for pair in range(n_heads // 2):
        parts = [(_by_sublane_tile(acc_ref[h]) / l_ref[h][None]).reshape(HEAD_DIM, tq)
                 for h in (2 * pair, 2 * pair + 1)]
        o_ref[:, pair * LANES:(pair + 1) * LANES] = jnp.concatenate(parts, axis=0).T.astype(BF16)


def _flash_scratch(n_heads, tk, tq):
    return [pltpu.VMEM((n_heads, SUBLANES, tq), F32), pltpu.VMEM((n_heads, SUBLANES, tq), F32),
            pltpu.VMEM((n_heads, HEAD_DIM, tq), F32), pltpu.VMEM((n_heads, LANES, tq), BF16),
            pltpu.VMEM((n_heads, tk, tq), F32), pltpu.VMEM((n_heads, tk, tq), BF16),
            pltpu.VMEM((n_heads, SUBLANES, tq), F32)]


def _stage_pair_masked_q(qt_ref, qpm_ref, n_heads):
    for h in range(n_heads):
        pair = h // 2
        rows = qt_ref[pair * LANES:(pair + 1) * LANES, :].astype(F32)
        r = lax.broadcasted_iota(I32, rows.shape, 0)
        qpm_ref[h] = jnp.where(r // HEAD_DIM == h % 2, rows, 0.0).astype(BF16)


def _fox_kernel(qt_ref, k_ref, vt_ref, cb_ref, o_ref, *st, tq):
    m_ref, l_ref, acc_ref, qpm_ref = st[:4]
    i = pl.program_id(1)
    _stage_pair_masked_q(qt_ref, qpm_ref, H_FOX)
    _flash_init(m_ref, l_ref, acc_ref)
    key_row = lax.broadcasted_iota(I32, (tq, tq), 0)
    query_col = lax.broadcasted_iota(I32, (tq, tq), 1)
    visible = key_row <= query_col

    def kv_step(j, masked):
        start = pl.multiple_of(j * tq, tq)

        def scores(h):
            pair = h // 2
            s = jnp.dot(k_ref[pl.ds(start, tq), pair * LANES:(pair + 1) * LANES], qpm_ref[h],
                        preferred_element_type=F32)
            s = s - jnp.concatenate([cb_ref[h, pl.ds(start, tq), :]] * (tq // LANES), axis=1)
            return jnp.where(visible, s, NEG_INF) if masked else s

        _flash_step(H_FOX, scores, lambda h: vt_ref[j, h * HEAD_DIM:(h + 1) * HEAD_DIM, :], st)

    def body(j, c):
        kv_step(j, False)
        return c

    lax.fori_loop(0, i, body, 0)
    kv_step(i, True)
    _flash_finish(o_ref, l_ref, acc_ref, H_FOX)


def _fox_prompt(qt, kb, vtc, cb, batch, seq):
    tq = FOX_KV_CHUNK
    nq = seq // tq
    return pl.pallas_call(
        functools.partial(_fox_kernel, tq=tq),
        grid=(batch, nq),
        in_specs=[pl.BlockSpec((None, W_FOX, tq), lambda b, i: (b, 0, i)),
                  pl.BlockSpec((seq, W_FOX), lambda b, i: (b, 0)),
                  pl.BlockSpec((None, nq, W_FOX, tq), lambda b, i: (b, 0, 0, 0)),
                  pl.BlockSpec((H_FOX, seq, LANES), lambda b, i: (0, b, 0))],
        out_specs=pl.BlockSpec((tq, W_FOX), lambda b, i: (b * nq + i, 0)),
        out_shape=jax.ShapeDtypeStruct((batch * seq, W_FOX), BF16),
        scratch_shapes=_flash_scratch(H_FOX, tq, tq),
        compiler_params=_cparams(("arbitrary", "arbitrary")),
        name="fox_prompt",
    )(qt, kb, vtc, cb)


def _search_threshold(count_ge, lo0, hi0, topk):
    def key_to_value(k):
        return pltpu.bitcast(k ^ ((k >> 31) & 0x7FFFFFFF), F32)

    def key_midpoint(lo, hi):
        return (lo >> 1) + (hi >> 1) + (lo & hi & 1)

    def guarded(lo, hi, value):
        mid = _sort_key(value)
        return jnp.where((mid > lo) & (mid < hi), mid, key_midpoint(lo, hi))

    def refine(lo, hi, n_lo, n_hi, mid):
        n_mid = count_ge(mid)
        take = n_mid >= topk
        return (jnp.where(take, mid, lo), jnp.where(take, hi, mid),
                jnp.where(take, n_mid, n_lo), jnp.where(take, n_hi, n_mid))

    def search_round(state):
        it, lo, hi, n_lo, n_hi, _ = state
        by_value = it < VALUE_BISECTION_ROUNDS
        for _ in range(2):
            mid_v = guarded(lo, hi, 0.5 * key_to_value(lo) + 0.5 * key_to_value(hi))
            lo, hi, n_lo, n_hi = refine(lo, hi, n_lo, n_hi, jnp.where(by_value, mid_v, key_midpoint(lo, hi)))
        active = (n_lo > topk) & (hi != lo + 1)
        return it + 1, lo, hi, n_lo, n_hi, jnp.max(active.astype(F32))

    def searching(state):
        return (state[0] < VALUE_BISECTION_ROUNDS + 17) & (state[5] > 0)

    n0 = count_ge(lo0)
    return lax.while_loop(searching, search_round,
                          (jnp.int32(0), lo0, hi0, n0, jnp.zeros_like(n0), jnp.max((n0 > topk).astype(F32))))[1]


def _dsa_kernel(iqt_ref, iwt_ref, ik_ref, qt_ref, k_ref, vt_ref, o_ref, key_ref, bias_ref, *st,
                tq, ck, topk):
    m_ref, l_ref, acc_ref, qpm_ref = st[:4]
    i = pl.program_id(1)
    q0 = i * tq
    n_chunks = (q0 + tq + ck - 1) // ck
    key_row = lax.broadcasted_iota(I32, (ck, tq), 0)
    query_pos = q0 + lax.broadcasted_iota(I32, (ck, tq), 1)
    iw8 = [jnp.broadcast_to(iwt_ref[h:h + 1, :], (SUBLANES, tq)) for h in range(H_IDX)]

    def score_chunk(c, _):
        start = pl.multiple_of(c * ck, ck)
        ikc = ik_ref[pl.ds(start, ck), :]
        sc = jnp.zeros((ck // SUBLANES, SUBLANES, tq), F32)
        for h in range(H_IDX):
            rel = jnp.dot(ikc, iqt_ref[h * D_IDX:(h + 1) * D_IDX, :], preferred_element_type=F32)
            sc = sc + _by_sublane_tile(jnp.maximum(rel, 0.0)) * iw8[h][None]
        sc = jnp.where(key_row + start <= query_pos, sc.reshape(ck, tq), NEG_INF)
        key_ref[c] = _sort_key(sc)
        return 0

    lax.fori_loop(0, n_chunks, score_chunk, 0)

    def count(pred):
        def body(c, acc):
            return acc + jnp.sum(pred(_by_sublane_tile(key_ref[c])).astype(F32), axis=0)
        return _over_sublanes(lax.fori_loop(0, n_chunks, body, jnp.zeros((SUBLANES, tq), F32)), jnp.sum)

    def extreme(pick, reduce_fn, fill):
        def body(c, acc):
            k = _by_sublane_tile(key_ref[c])
            return pick(acc, reduce_fn(jnp.where(k > KEY_NEG_INF, k, fill), axis=0))
        init = jnp.full((SUBLANES, tq), fill, I32)
        return _over_sublanes(lax.fori_loop(0, n_chunks, body, init), reduce_fn)

    lo0 = extreme(jnp.minimum, jnp.min, jnp.int32(2 ** 31 - 1))
    hi0 = extreme(jnp.maximum, jnp.max, jnp.int32(KEY_NEG_INF)) + 1

    thr = _search_threshold(lambda t: count(lambda k: k >= t[None]), lo0, hi0, topk)
    need = topk - count(lambda k: k > thr[None])
    ties = count(lambda k: k == thr[None])
    excess = jnp.max(jnp.where((ties > need) & (thr > KEY_NEG_INF), 1.0, 0.0)) > 0.0

    @pl.when(jnp.logical_not(excess))
    def _():
        def fill(c, _):
            k = _by_sublane_tile(key_ref[c])
            bias_ref[c] = jnp.where((k >= thr[None]) & (k > KEY_NEG_INF), 0.0, NEG_INF).reshape(ck, tq)
            return 0
        lax.fori_loop(0, n_chunks, fill, 0)

    @pl.when(excess)
    def _():
        rr = lax.broadcasted_iota(I32, (ck, ck), 0)
        cc = lax.broadcasted_iota(I32, (ck, ck), 1)
        strict_lower = (cc < rr).astype(BF16)

        def fill(c, before):
            k = _by_sublane_tile(key_ref[c])
            tie = k == thr[None]
            tie_f = tie.astype(F32)
            earlier = jnp.dot(strict_lower, tie_f.reshape(ck, tq).astype(BF16), preferred_element_type=F32)
            earlier = _by_sublane_tile(earlier) + before[None]
            sel = (k > thr[None]) | (tie & (earlier < need[None]))
            bias_ref[c] = jnp.where(sel & (k > KEY_NEG_INF), 0.0, NEG_INF).reshape(ck, tq)
            return before + _over_sublanes(jnp.sum(tie_f, axis=0), jnp.sum)
        lax.fori_loop(0, n_chunks, fill, jnp.zeros((SUBLANES, tq), F32))

    _stage_pair_masked_q(qt_ref, qpm_ref, H_DSA)
    _flash_init(m_ref, l_ref, acc_ref)

    def kv_step(c, carry):
        start = pl.multiple_of(c * ck, ck)
        def scores(h):
            pair = h // 2
            return jnp.dot(k_ref[pl.ds(start, ck), pair * LANES:(pair + 1) * LANES], qpm_ref[h],
                           preferred_element_type=F32) + bias_ref[c]

        _flash_step(H_DSA, scores, lambda h: vt_ref[c, h * HEAD_DIM:(h + 1) * HEAD_DIM, :], st)
        return carry

    lax.fori_loop(0, n_chunks, kv_step, 0)
    _flash_finish(o_ref, l_ref, acc_ref, H_DSA)


def _dsa_prompt(iqt, iwt, ikb, dqt, dkb, dvtc, batch, seq, tq):
    ck = dvtc.shape[3]
    nq = seq // tq
    topk = min(TOPK_MAX, seq // 4)
    return pl.pallas_call(
        functools.partial(_dsa_kernel, tq=tq, ck=ck, topk=topk),
        grid=(batch, nq),
        in_specs=[pl.BlockSpec((None, H_IDX * D_IDX, tq), lambda b, i: (b, 0, i)),
                  pl.BlockSpec((None, H_IDX, tq), lambda b, i: (b, 0, i)),
                  pl.BlockSpec((seq, D_IDX), lambda b, i: (b, 0)),
                  pl.BlockSpec((None, W_DSA, tq), lambda b, i: (b, 0, i)),
                  pl.BlockSpec((seq, W_DSA), lambda b, i: (b, 0)),
                  pl.BlockSpec((None, seq // ck, W_DSA, ck), lambda b, i: (b, 0, 0, 0))],
        out_specs=pl.BlockSpec((tq, W_DSA), lambda b, i: (b * nq + i, 0)),
        out_shape=jax.ShapeDtypeStruct((batch * seq, W_DSA), BF16),
        scratch_shapes=[pltpu.VMEM((seq // ck, ck, tq), I32), pltpu.VMEM((seq // ck, ck, tq), F32)]
        + _flash_scratch(H_DSA, ck, tq),
        compiler_params=_cparams(("arbitrary", "arbitrary")),
        name="dsa_prompt",
    )(iqt, iwt, ikb, dqt, dkb, dvtc)


KV_PAGES_PER_STEP = 16
IDX_PAGES_PER_STEP = 32


def _head_mask(rows, width):
    r = lax.broadcasted_iota(I32, (rows, width), 0)
    c = lax.broadcasted_iota(I32, (rows, width), 1)
    return (c // HEAD_DIM) == (r % H_FOX)


def _block_diag_q(q, n_q):
    hm = _head_mask(H_FOX, q.shape[1])
    qf = q.astype(F32)
    parts = [jnp.where(hm, jnp.broadcast_to(qf[t:t + 1, :], hm.shape), 0.0) for t in range(n_q)]
    return jnp.concatenate(parts, axis=0).astype(BF16)


def _rows_to_groups(x, n_q):
    return jnp.concatenate([jnp.broadcast_to(x[t:t + 1, :], (H_FOX, x.shape[1])) for t in range(n_q)], axis=0)


DECODE_SPLITS = 2


def _softmax_page(g, s, vt_bf16, m_ref, l_ref, acc_ref):
    m = m_ref[g]
    m_new = jnp.maximum(m, jnp.max(s, axis=1, keepdims=True))
    alpha = jnp.exp(m - m_new)
    p = jnp.exp(s - m_new)
    l_ref[g] = alpha * l_ref[g] + jnp.sum(p, axis=1, keepdims=True)
    acc_ref[g] = alpha * acc_ref[g] + lax.dot_general(p.astype(BF16), vt_bf16, NT_DIMS,
                                                      preferred_element_type=F32)
    m_ref[g] = m_new


def _decode_scratch(rows, w):
    g = DECODE_SPLITS
    return [pltpu.VMEM((g, rows, 1), F32), pltpu.VMEM((g, rows, 1), F32), pltpu.VMEM((g, rows, w), F32)]


def _stack_pages(refs):
    return jnp.concatenate([r[...].reshape(-1, PAGE_SIZE) for r in refs], axis=1).astype(BF16)


def _finish_decode(o_ref, m_ref, l_ref, acc_ref, n_q):
    m = m_ref[0]
    for g in range(1, DECODE_SPLITS):
        m = jnp.maximum(m, m_ref[g])
    l = jnp.zeros_like(m)
    acc = jnp.zeros(acc_ref.shape[1:], F32)
    for g in range(DECODE_SPLITS):
        w = jnp.exp(m_ref[g] - m)
        l = l + w * l_ref[g]
        acc = acc + w * acc_ref[g]
    o = acc / l
    o = jnp.where(_head_mask(o.shape[0], o.shape[1]), o, 0.0)
    o_ref[...] = jnp.sum(o.reshape(n_q, H_FOX, o.shape[1]), axis=1).astype(BF16)


def _sample_fox_kernel(pt_ref, q_ref, kn_ref, vn_ref, lfn_ref, *rest, n_q, pps):
    k_refs, v_refs, lf_refs = rest[:pps], rest[pps:2 * pps], rest[2 * pps:3 * pps]
    o_ref, m_ref, l_ref, acc_ref, cc_ref, qbd_ref = rest[3 * pps:]
    step = pl.program_id(1)
    rows = n_q * H_FOX

    @pl.when(step == 0)
    def _():
        m_ref[...] = jnp.full(m_ref.shape, -1e30, F32)
        l_ref[...] = jnp.zeros(l_ref.shape, F32)
        acc_ref[...] = jnp.zeros(acc_ref.shape, F32)
        cc_ref[...] = jnp.zeros(cc_ref.shape, F32)
        qbd_ref[...] = _block_diag_q(q_ref[...], n_q)

    upper = _upper_tri(PAGE_SIZE)

    within = _lane_cumsum(jnp.concatenate([r[...] for r in lf_refs], axis=0), upper)
    off = cc_ref[...]
    parts = []
    for p in range(pps):
        cp = within[p * H_FOX:(p + 1) * H_FOX] + off
        parts.append(cp)
        off = cp[:, PAGE_SIZE - 1:PAGE_SIZE]
    cc_ref[...] = off
    per = pps // DECODE_SPLITS
    for g in range(DECODE_SPLITS):
        pages = slice(g * per, (g + 1) * per)
        cbias = jnp.concatenate(parts[pages], axis=1)
        s = jnp.dot(qbd_ref[...], _stack_pages(k_refs[pages]), preferred_element_type=F32)
        s = s - jnp.concatenate([cbias] * n_q, axis=0)
        _softmax_page(g, s, _stack_pages(v_refs[pages]), m_ref, l_ref, acc_ref)

    @pl.when(step == pl.num_programs(1) - 1)
    def _():
        r = lax.broadcasted_iota(I32, (rows, PAGE_SIZE), 0)
        c = lax.broadcasted_iota(I32, (rows, PAGE_SIZE), 1)
        cn = _lane_cumsum(lfn_ref[...], upper) + cc_ref[...]
        sn = jnp.dot(qbd_ref[...], kn_ref[...].astype(BF16), preferred_element_type=F32)
        sn = sn - jnp.concatenate([cn] * n_q, axis=0)
        sn = jnp.where(c <= r // H_FOX, sn, NEG_INF)
        _softmax_page(0, sn, vn_ref[...].astype(BF16), m_ref, l_ref, acc_ref)
        _finish_decode(o_ref, m_ref, l_ref, acc_ref, n_q)


def _page_spec(block, p, pps):
    nd = len(block)
    return pl.BlockSpec((None,) + block, lambda b, s, pt: (pt[b, s * pps + p],) + (0,) * nd)


def _sample_fox(page_table, q, k_new, v_new, lft_new, pool_k, pool_v, pool_lft):
    dec_b, n_q, w = q.shape
    n_pages = page_table.shape[1]
    pps = min(KV_PAGES_PER_STEP, n_pages)
    rows = n_q * H_FOX
    seq_spec = lambda blk: pl.BlockSpec((None,) + blk, lambda b, s, pt: (b,) + (0,) * len(blk))
    kv_page = (H_FOX, HEAD_DIM, PAGE_SIZE)
    in_specs = ([seq_spec((n_q, w)), seq_spec((w, PAGE_SIZE)), seq_spec((w, PAGE_SIZE)),
                 seq_spec((H_FOX, PAGE_SIZE))]
                + [_page_spec(kv_page, p, pps) for p in range(pps)]
                + [_page_spec(kv_page, p, pps) for p in range(pps)]
                + [_page_spec((H_FOX, PAGE_SIZE), p, pps) for p in range(pps)])
    return pl.pallas_call(
        functools.partial(_sample_fox_kernel, n_q=n_q, pps=pps),
        grid_spec=pltpu.PrefetchScalarGridSpec(
            num_scalar_prefetch=1,
            grid=(dec_b, n_pages // pps),
            in_specs=in_specs,
            out_specs=pl.BlockSpec((None, n_q, w), lambda b, s, pt: (b, 0, 0)),
            scratch_shapes=_decode_scratch(rows, w) + [pltpu.VMEM((H_FOX, 1), F32),
                                                       pltpu.VMEM((rows, w), BF16)]),
        out_shape=jax.ShapeDtypeStruct((dec_b, n_q, w), BF16),
        compiler_params=_cparams(("arbitrary", "arbitrary")),
        name="sample_fox",
    )(page_table, q, k_new, v_new, lft_new, *([pool_k] * pps), *([pool_v] * pps), *([pool_lft] * pps))


def _sample_index_kernel(pt_ref, iq_ref, iw_ref, kn_ref, *rest, pps, n_q, topk):
    k_refs = rest[:pps]
    bias_ref, biasn_ref, key_ref, keyn_ref = rest[pps:]
    step = pl.program_id(1)
    n_steps = key_ref.shape[0]
    iq = iq_ref[...]
    iw = iw_ref[...]

    def scores(kt):
        rel = jnp.dot(iq, kt, preferred_element_type=F32)
        w = iw * jnp.maximum(rel, 0.0)
        by_parity = jnp.sum(w.reshape(H_IDX // 2, SUBLANES, kt.shape[1]), axis=0)
        return by_parity + pltpu.roll(by_parity, SUBLANES // 2, 0)

    key_ref[step] = _sort_key(scores(_stack_pages(k_refs)))

    @pl.when(step == n_steps - 1)
    def _():
        r = lax.broadcasted_iota(I32, (H_FOX, PAGE_SIZE), 0)
        c = lax.broadcasted_iota(I32, (H_FOX, PAGE_SIZE), 1)
        keyn_ref[...] = _sort_key(jnp.where((c <= r) & (r < n_q), scores(kn_ref[...].astype(BF16)), NEG_INF))

        def count(pred):
            past = jnp.sum(jnp.sum(pred(key_ref[...]).astype(F32), axis=0), axis=1, keepdims=True)
            return past + jnp.sum(pred(keyn_ref[...]).astype(F32), axis=1, keepdims=True)

        def extreme(reduce_fn, fill):
            past = reduce_fn(reduce_fn(jnp.where(key_ref[...] > KEY_NEG_INF, key_ref[...], fill), axis=0),
                             axis=1, keepdims=True)
            new = reduce_fn(jnp.where(keyn_ref[...] > KEY_NEG_INF, keyn_ref[...], fill), axis=1, keepdims=True)
            return reduce_fn(jnp.concatenate([past, new], axis=1), axis=1, keepdims=True)

        lo0 = extreme(jnp.min, jnp.int32(2 ** 31 - 1))
        hi0 = extreme(jnp.max, jnp.int32(KEY_NEG_INF)) + 1
        thr = _search_threshold(lambda t: count(lambda k: k >= t), lo0, hi0, topk)
        need = topk - count(lambda k: k > thr)
        ties = count(lambda k: k == thr)
        excess = jnp.max(jnp.where((ties > need) & (thr > KEY_NEG_INF), 1.0, 0.0)) > 0.0

        @pl.when(jnp.logical_not(excess))
        def _():
            k = key_ref[...]
            bias_ref[...] = jnp.where((k >= thr) & (k > KEY_NEG_INF), 0.0, NEG_INF)
            k = keyn_ref[...]
            biasn_ref[...] = jnp.where((k >= thr) & (k > KEY_NEG_INF), 0.0, NEG_INF)

        @pl.when(excess)
        def _():
            upper = _upper_tri(PAGE_SIZE)

            def tie_block(k, before):
                tie = k == thr
                incl = jnp.dot(tie.astype(BF16), upper, preferred_element_type=F32) + before
                sel = (k > thr) | (tie & (incl - tie.astype(F32) < need))
                return jnp.where(sel & (k > KEY_NEG_INF), 0.0, NEG_INF), incl[:, PAGE_SIZE - 1:PAGE_SIZE]

            def fill(c, before):
                for p in range(pps):
                    lanes = slice(p * PAGE_SIZE, (p + 1) * PAGE_SIZE)
                    b, before = tie_block(key_ref[c, :, lanes], before)
                    bias_ref[c, :, lanes] = b
                return before

            before = lax.fori_loop(0, n_steps, fill, jnp.zeros((H_FOX, 1), F32))
            biasn_ref[...] = tie_block(keyn_ref[...], before)[0]


def _sample_index(page_table, iqx, iwx, kidx_new, pool_kidx, n_q):
    dec_b = iqx.shape[0]
    n_pages = page_table.shape[1]
    pps = min(IDX_PAGES_PER_STEP, n_pages)
    n_steps = n_pages // pps
    step_w = pps * PAGE_SIZE
    topk = min(TOPK_MAX, (n_pages * PAGE_SIZE + n_q) // 4)
    seq_spec = lambda blk: pl.BlockSpec((None,) + blk, lambda b, s, pt: (b,) + (0,) * len(blk))
    iq_rows = iqx.shape[1]
    in_specs = ([seq_spec((iq_rows, D_IDX)), seq_spec((iq_rows, 1)), seq_spec((D_IDX, PAGE_SIZE))]
                + [_page_spec((D_IDX, PAGE_SIZE), p, pps) for p in range(pps)])
    return pl.pallas_call(
        functools.partial(_sample_index_kernel, pps=pps, n_q=n_q, topk=topk),
        grid_spec=pltpu.PrefetchScalarGridSpec(
            num_scalar_prefetch=1,
            grid=(dec_b, n_steps),
            in_specs=in_specs,
            out_specs=[pl.BlockSpec((None, n_steps, H_FOX, step_w), lambda b, s, pt: (b, 0, 0, 0)),
                       pl.BlockSpec((None, H_FOX, PAGE_SIZE), lambda b, s, pt: (b, 0, 0))],
            scratch_shapes=[pltpu.VMEM((n_steps, H_FOX, step_w), I32), pltpu.VMEM((H_FOX, PAGE_SIZE), I32)]),
        out_shape=[jax.ShapeDtypeStruct((dec_b, n_steps, H_FOX, step_w), F32),
                   jax.ShapeDtypeStruct((dec_b, H_FOX, PAGE_SIZE), F32)],
        compiler_params=_cparams(("arbitrary", "arbitrary")),
        name="sample_index",
    )(page_table, iqx, iwx, kidx_new, *([pool_kidx] * pps))


def _sample_dsa_kernel(pt_ref, q_ref, kn_ref, vn_ref, bias_ref, biasn_ref, *rest, n_q, pps):
    k_refs, v_refs = rest[:pps], rest[pps:2 * pps]
    o_ref, m_ref, l_ref, acc_ref, qbd_ref = rest[2 * pps:]
    step = pl.program_id(1)

    @pl.when(step == 0)
    def _():
        m_ref[...] = jnp.full(m_ref.shape, -1e30, F32)
        l_ref[...] = jnp.zeros(l_ref.shape, F32)
        acc_ref[...] = jnp.zeros(acc_ref.shape, F32)
        qbd_ref[...] = _block_diag_q(q_ref[...], n_q)

    def attend(g, kt, vt, bias):
        s = jnp.dot(qbd_ref[...], kt, preferred_element_type=F32) + _rows_to_groups(bias, n_q)
        _softmax_page(g, s, vt, m_ref, l_ref, acc_ref)

    per = pps // DECODE_SPLITS
    for g in range(DECODE_SPLITS):
        pages = slice(g * per, (g + 1) * per)
        attend(g, _stack_pages(k_refs[pages]), _stack_pages(v_refs[pages]),
               bias_ref[0, :, g * per * PAGE_SIZE:(g + 1) * per * PAGE_SIZE])

    @pl.when(step == pl.num_programs(1) - 1)
    def _():
        attend(0, kn_ref[...].astype(BF16), vn_ref[...].astype(BF16), biasn_ref[...])
        _finish_decode(o_ref, m_ref, l_ref, acc_ref, n_q)


def _sample_dsa(page_table, q, k_new, v_new, bias, bias_new, pool_k, pool_v):
    dec_b, n_q, w = q.shape
    n_pages = page_table.shape[1]
    pps = min(KV_PAGES_PER_STEP, n_pages)
    rows = n_q * H_DSA
    kv_page = (H_DSA, HEAD_DIM, PAGE_SIZE)
    per_bias_chunk = bias.shape[3] // (pps * PAGE_SIZE)
    seq_spec = lambda blk: pl.BlockSpec((None,) + blk, lambda b, s, pt: (b,) + (0,) * len(blk))
    in_specs = ([seq_spec((n_q, w)), seq_spec((w, PAGE_SIZE)), seq_spec((w, PAGE_SIZE)),
                 pl.BlockSpec((None, 1, H_FOX, pps * PAGE_SIZE),
                              lambda b, s, pt: (b, s // per_bias_chunk, 0, s % per_bias_chunk)),
                 seq_spec((H_FOX, PAGE_SIZE))]
                + [_page_spec(kv_page, p, pps) for p in range(pps)]
                + [_page_spec(kv_page, p, pps) for p in range(pps)])
    return pl.pallas_call(
        functools.partial(_sample_dsa_kernel, n_q=n_q, pps=pps),
        grid_spec=pltpu.PrefetchScalarGridSpec(
            num_scalar_prefetch=1,
            grid=(dec_b, n_pages // pps),
            in_specs=in_specs,
            out_specs=pl.BlockSpec((None, n_q, w), lambda b, s, pt: (b, 0, 0)),
            scratch_shapes=_decode_scratch(rows, w) + [pltpu.VMEM((rows, w), BF16)]),
        out_shape=jax.ShapeDtypeStruct((dec_b, n_q, w), BF16),
        compiler_params=_cparams(("arbitrary", "arbitrary")),
        name="sample_dsa",
    )(page_table, q, k_new, v_new, bias, bias_new, *([pool_k] * pps), *([pool_v] * pps))


ROUTE_LANES = 8


def _layer_norm(r, g, b):
    mu = jnp.mean(r, axis=1, keepdims=True)
    d = r - mu
    var = jnp.mean(d * d, axis=1, keepdims=True)
    return d * lax.rsqrt(var + LN_EPS) * g + b


def _tail_kernel(of_ref, od_ref, x_ref, wo_ref, g_ref, b_ref, wr_ref, br_ref, cin_ref,
                 x1_ref, e_ref, gate_ref, rank_ref, cnt_ref, carry_ref, *, alpha):
    i = pl.program_id(0)
    tm = x_ref.shape[0]

    @pl.when(i == 0)
    def _():
        carry_ref[...] = cin_ref[...]

    r = (alpha * x_ref[...]
         + jnp.dot(of_ref[...], wo_ref[0:W_FOX, :], preferred_element_type=F32)
         + jnp.dot(od_ref[...], wo_ref[W_FOX:W_FOX + W_DSA, :], preferred_element_type=F32))
    x1 = _layer_norm(r, g_ref[...], b_ref[...])
    x1_ref[...] = x1

    logits = jnp.dot(x1.astype(BF16), wr_ref[...], preferred_element_type=F32) + br_ref[...]
    lane = lax.broadcasted_iota(I32, (tm, LANES), 1)
    lane_f = lane.astype(F32)
    vals, idxs, hots = [], [], []
    lg = logits
    for _ in range(TOP_K_EXPERTS):
        m = jnp.max(lg, axis=1, keepdims=True)
        idx = jnp.min(jnp.where(lg == m, lane_f, float(LANES)), axis=1, keepdims=True)
        hot = lane_f == idx
        vals.append(m)
        idxs.append(idx)
        hots.append(hot)
        lg = jnp.where(hot, NEG_INF, lg)
    exps = [jnp.exp(v - vals[0]) for v in vals]
    den = exps[0]
    for e in exps[1:]:
        den = den + e
    sel = jnp.zeros((tm, LANES), F32)
    for hot in hots:
        sel = sel + hot.astype(F32)
    rr = lax.broadcasted_iota(I32, (tm, tm), 0)
    cc = lax.broadcasted_iota(I32, (tm, tm), 1)
    strict_lower = (cc < rr).astype(BF16)
    before = jnp.dot(strict_lower, sel.astype(BF16), preferred_element_type=F32) + carry_ref[...]
    e_out = jnp.zeros((tm, LANES), F32)
    g_out = jnp.zeros((tm, LANES), F32)
    r_out = jnp.zeros((tm, LANES), F32)
    for j in range(TOP_K_EXPERTS):
        rank = jnp.sum(jnp.where(hots[j], before, 0.0), axis=1, keepdims=True)
        e_out = jnp.where(lane == j, idxs[j], e_out)
        g_out = jnp.where(lane == j, exps[j] / den, g_out)
        r_out = jnp.where(lane == j, rank, r_out)
    e_ref[...] = e_out[:, :ROUTE_LANES].astype(I32)
    gate_ref[...] = g_out[:, :ROUTE_LANES]
    rank_ref[...] = r_out[:, :ROUTE_LANES].astype(I32)
    carry_ref[...] = carry_ref[...] + jnp.sum(sel, axis=0, keepdims=True)
    cnt_ref[...] = carry_ref[...]


def _tail(of, od, x2d, wo, g1, b1, wr, br, cin, alpha, tm):
    rows, d = x2d.shape
    row_spec = lambda w: pl.BlockSpec((tm, w), lambda i: (i, 0))
    full = lambda a: pl.BlockSpec(a.shape, lambda i: (0,) * a.ndim)
    return pl.pallas_call(
        functools.partial(_tail_kernel, alpha=alpha),
        grid=(rows // tm,),
        in_specs=[row_spec(W_FOX), row_spec(W_DSA), row_spec(d), full(wo), full(g1), full(b1),
                  full(wr), full(br), full(cin)],
        out_specs=[row_spec(d), row_spec(ROUTE_LANES), row_spec(ROUTE_LANES), row_spec(ROUTE_LANES),
                   pl.BlockSpec((1, LANES), lambda i: (0, 0))],
        out_shape=[jax.ShapeDtypeStruct((rows, d), F32), jax.ShapeDtypeStruct((rows, ROUTE_LANES), I32),
                   jax.ShapeDtypeStruct((rows, ROUTE_LANES), F32), jax.ShapeDtypeStruct((rows, ROUTE_LANES), I32),
                   jax.ShapeDtypeStruct((1, LANES), F32)],
        scratch_shapes=[pltpu.VMEM((1, LANES), F32)],
        compiler_params=_cparams(("arbitrary",)),
        name="tail_router",
    )(of, od, x2d, wo, g1, b1, wr, br, cin)


EXPERT_BLOCK = 256
DISPATCH_TOKENS = 128
COMBINE_UNROLL = 4


def _dispatch_kernel(dest_ref, x_ref, xs_in, xs_out, sem, *, td):
    del xs_in

    def row_copy(src_row, dst_row):
        return pltpu.make_async_copy(x_ref.at[pl.ds(src_row, 1), :], xs_out.at[pl.ds(dst_row, 1), :], sem)

    def issue(tt, c):
        for u in range(COMBINE_UNROLL):
            t = tt * COMBINE_UNROLL + u
            for j in range(TOP_K_EXPERTS):
                row_copy(t, dest_ref[t * TOP_K_EXPERTS + j]).start()
        return c

    lax.fori_loop(0, td // COMBINE_UNROLL, issue, 0)

    def drain(tt, c):
        for _ in range(COMBINE_UNROLL * TOP_K_EXPERTS):
            row_copy(0, 0).wait()
        return c

    lax.fori_loop(0, td // COMBINE_UNROLL, drain, 0)


def _dispatch(dest_flat, x1, xs):
    rows, d = x1.shape
    td = DISPATCH_TOKENS
    return pl.pallas_call(
        functools.partial(_dispatch_kernel, td=td),
        grid=(rows // td,),
        in_specs=[pl.BlockSpec((td * TOP_K_EXPERTS,), lambda i: (i,), memory_space=pltpu.SMEM),
                  pl.BlockSpec((td, d), lambda i: (i, 0)), pl.BlockSpec(memory_space=pl.ANY)],
        out_specs=pl.BlockSpec(memory_space=pl.ANY),
        out_shape=jax.ShapeDtypeStruct(xs.shape, xs.dtype),
        scratch_shapes=[pltpu.SemaphoreType.DMA(())],
        input_output_aliases={2: 0},
        compiler_params=_cparams(("arbitrary",)),
        name="moe_dispatch",
    )(dest_flat, x1, xs)


def _ffn_kernel(be_ref, first_ref, nused_ref, nexte_ref, slot_ref, xs_ref, wg_hbm, wu_hbm, wd_hbm,
                bg_ref, bu_ref, bd_ref, ys_ref, stage_ref, wgb_ref, wub_ref, wdb_ref, sem):
    b = pl.program_id(0)

    def weight_copies(e, slot):
        return [pltpu.make_async_copy(w.at[e], stage_ref.at[slot, k], sem.at[slot, k])
                for k, w in enumerate((wg_hbm, wu_hbm, wd_hbm))]

    @pl.when(b == 0)
    def _():
        for c in weight_copies(be_ref[0], 0):
            c.start()

    @pl.when(first_ref[b] == 1)
    def _():
        slot = slot_ref[b]
        for c in weight_copies(be_ref[b], slot):
            c.wait()

        @pl.when(nexte_ref[b] >= 0)
        def _():
            for c in weight_copies(nexte_ref[b], 1 - slot):
                c.start()

        wgb_ref[...] = stage_ref[slot, 0].astype(BF16)
        wub_ref[...] = stage_ref[slot, 1].astype(BF16)
        wdb_ref[...] = stage_ref[slot, 2].astype(BF16)

    @pl.when(b < nused_ref[0])
    def _():
        xb = xs_ref[...].astype(BF16)
        g = jnp.dot(xb, wgb_ref[...], preferred_element_type=F32) + bg_ref[...]
        u = jnp.dot(xb, wub_ref[...], preferred_element_type=F32) + bu_ref[...]
        g = jnp.minimum(g, SWIGLU_LIMIT)
        u = jnp.clip(u, -SWIGLU_LIMIT, SWIGLU_LIMIT)
        h = g * (1.0 / (1.0 + jnp.exp(-SWIGLU_ALPHA * g))) * (u + 1.0)
        ys_ref[...] = jnp.dot(h.astype(BF16), wdb_ref[...], preferred_element_type=F32) + bd_ref[...]

    @pl.when(b >= nused_ref[0])
    def _():
        ys_ref[...] = jnp.zeros(ys_ref.shape, F32)


def _expert_ffn(block_e, is_first, n_used, next_e, slot, xs, wg, wu, wd, bg, bu, bd):
    p, d = xs.shape
    bs = EXPERT_BLOCK
    de = wg.shape[2]
    assert d == de
    b_spec = lambda w: pl.BlockSpec((None, 1, w), lambda b, be, fi, nu, ne, sl: (be[b], 0, 0))
    hbm = pl.BlockSpec(memory_space=pl.ANY)
    return pl.pallas_call(
        _ffn_kernel,
        grid_spec=pltpu.PrefetchScalarGridSpec(
            num_scalar_prefetch=5,
            grid=(p // bs,),
            in_specs=[pl.BlockSpec((bs, d), lambda b, be, fi, nu, ne, sl: (b, 0)),
                      hbm, hbm, hbm, b_spec(de), b_spec(de), b_spec(d)],
            out_specs=pl.BlockSpec((bs, d), lambda b, be, fi, nu, ne, sl: (b, 0)),
            scratch_shapes=[pltpu.VMEM((2, 3, d, de), F32),
                            pltpu.VMEM((d, de), BF16), pltpu.VMEM((d, de), BF16), pltpu.VMEM((de, d), BF16),
                            pltpu.SemaphoreType.DMA((2, 3))]),
        out_shape=jax.ShapeDtypeStruct((p, d), F32),
        compiler_params=_cparams(("arbitrary",)),
        name="moe_ffn",
    )(block_e, is_first, n_used, next_e, slot, xs, wg, wu, wd, bg, bu, bd)


def _combine_kernel(dest_ref, dest_next_ref, ys_hbm, gate_ref, x1_ref, g_ref, b_ref, o_ref, buf_ref, sem,
                    *, tc, alpha):
    i = pl.program_id(0)
    slot = i % 2

    def row_copy(src_row, sl, j, t):
        return pltpu.make_async_copy(ys_hbm.at[pl.ds(src_row, 1), :], buf_ref.at[sl, j, pl.ds(t, 1), :],
                                     sem.at[sl])

    def issue(d_ref, sl):
        def body(tt, c):
            for u in range(COMBINE_UNROLL):
                t = tt * COMBINE_UNROLL + u
                for j in range(TOP_K_EXPERTS):
                    row_copy(d_ref[t * TOP_K_EXPERTS + j], sl, j, t).start()
            return c
        lax.fori_loop(0, tc // COMBINE_UNROLL, body, 0)

    @pl.when(i == 0)
    def _():
        issue(dest_ref, slot)

    @pl.when(i + 1 < pl.num_programs(0))
    def _():
        issue(dest_next_ref, 1 - slot)

    def drain(tt, c):
        for _ in range(COMBINE_UNROLL * TOP_K_EXPERTS):
            row_copy(0, slot, 0, 0).wait()
        return c

    lax.fori_loop(0, tc // COMBINE_UNROLL, drain, 0)

    gates = gate_ref[...]
    moe = gates[:, 0:1] * buf_ref[slot, 0]
    for j in range(1, TOP_K_EXPERTS):
        moe = moe + gates[:, j:j + 1] * buf_ref[slot, j]
    o_ref[...] = _layer_norm(alpha * x1_ref[...] + moe, g_ref[...], b_ref[...])


def _combine(dest_flat, ys, gates, x1, g2, b2, alpha):
    rows, d = x1.shape
    tc = DISPATCH_TOKENS
    last = rows // tc - 1
    return pl.pallas_call(
        functools.partial(_combine_kernel, tc=tc, alpha=alpha),
        grid=(rows // tc,),
        in_specs=[pl.BlockSpec((tc * TOP_K_EXPERTS,), lambda i: (i,), memory_space=pltpu.SMEM),
                  pl.BlockSpec((tc * TOP_K_EXPERTS,), lambda i: (jnp.minimum(i + 1, last),),
                               memory_space=pltpu.SMEM),
                  pl.BlockSpec(memory_space=pl.ANY),
                  pl.BlockSpec((tc, ROUTE_LANES), lambda i: (i, 0)),
                  pl.BlockSpec((tc, d), lambda i: (i, 0)),
                  pl.BlockSpec((1, d), lambda i: (0, 0)),
                  pl.BlockSpec((1, d), lambda i: (0, 0))],
        out_specs=pl.BlockSpec((tc, d), lambda i: (i, 0)),
        out_shape=jax.ShapeDtypeStruct((rows, d), F32),
        scratch_shapes=[pltpu.VMEM((2, TOP_K_EXPERTS, tc, d), F32), pltpu.SemaphoreType.DMA((2,))],
        compiler_params=_cparams(("arbitrary",)),
        name="moe_combine",
    )(dest_flat, dest_flat, ys, gates, x1, g2, b2)


def _prep_w_in(w, b_f):
    d = w.shape[0]
    sizes = (W_FOX, W_FOX, W_FOX, H_FOX, W_DSA, W_DSA, W_DSA, H_IDX * D_IDX, D_IDX, H_IDX)
    parts, acc = [], 0
    for s in sizes:
        parts.append(w[:, acc:acc + s])
        acc += s
    fq, fk, fv, fl, dq, dk, dv, iq, ik, iw = parts
    small = jnp.concatenate([ik, fl, iw, jnp.zeros((d, LANES - D_IDX - H_FOX - H_IDX), w.dtype)], axis=1)
    wp = jnp.concatenate([fq, fk, fv, dq, dk, dv, iq, small], axis=1).astype(BF16)
    bf_row = jnp.zeros((1, LANES), F32).at[0, SM_LF:SM_LF + H_FOX].set(b_f.astype(F32))
    return wp, bf_row


def _rope_tables(pos):
    half = HEAD_DIM // 2
    inv = ROPE_THETA ** (-jnp.arange(half, dtype=F32) / half)
    ang = pos.astype(F32)[:, None] * inv[None, :]
    cos, sin = jnp.cos(ang), jnp.sin(ang)
    cos_t = jnp.concatenate([cos, cos, cos, cos], axis=1)
    sin_t = jnp.concatenate([-sin, sin, -sin, sin], axis=1)
    return cos_t, sin_t


def kernel(x_prompt, x_sample, cache_fox_k, cache_fox_v, cache_fox_logf, cache_dsa_k, cache_dsa_v,
           cache_dsa_kidx, page_table, w_in, b_f, w_o, ln1_g, ln1_b, w_router, b_router, w_gate, b_gate,
           w_up, b_up, w_down, b_down, ln2_g, ln2_b):
    batch, seq, d = x_prompt.shape
    dec_b, n_q, _ = x_sample.shape
    depth = w_in.shape[0]
    n_pool = cache_fox_k.shape[1]
    n_pages = page_table.shape[1]
    past = n_pages * PAGE_SIZE
    alpha = (2.0 * depth) ** 0.25
    rows_p, rows_s = batch * seq, dec_b * n_q

    cos_p, sin_p = _rope_tables(jnp.arange(seq))
    cos_s, sin_s = _rope_tables(past + (jnp.arange(rows_s) % n_q))

    xp = x_prompt.reshape(rows_p, d)
    xs_ = x_sample.reshape(rows_s, d)
    outs_p = [[] for _ in range(6)]
    outs_s = [[] for _ in range(6)]
    for l in range(depth):
        wp, bf_row = _prep_w_in(w_in[l], b_f[l])
        wo = w_o[l].astype(BF16)
        wr = jnp.pad(w_router[l], ((0, 0), (0, LANES - N_EXPERTS))).astype(BF16)
        br = jnp.full((1, LANES), NEG_INF, F32).at[0, :N_EXPERTS].set(b_router[l].astype(F32))
        g1, b1 = ln1_g[l].reshape(1, d), ln1_b[l].reshape(1, d)
        g2, b2 = ln2_g[l].reshape(1, d), ln2_b[l].reshape(1, d)

        pp = _in_proj(xp, wp, bf_row, cos_p, sin_p, batch, seq, tm=256)
        o_fox = _fox_prompt(pp["fqt"], pp["fkb"], pp["fvtc"], pp["cb"], batch, seq)
        o_dsa = _dsa_prompt(pp["iqt"], pp["iwt"], pp["ikb"], pp["dqt"], pp["dkb"], pp["dvtc"], batch, seq,
                            tq=256)

        ps = _in_proj(xs_, wp, bf_row, cos_s, sin_s, 1, rows_s, tm=rows_s)
        ps = {k: v[0] for k, v in ps.items() if v.ndim == 3}
        pool_fk = cache_fox_k[l].transpose(0, 2, 3, 1)
        pool_fv = cache_fox_v[l].transpose(0, 2, 3, 1)
        pool_lft = jnp.swapaxes(cache_fox_logf[l], 1, 2)
        pool_dk = cache_dsa_k[l].transpose(0, 2, 3, 1)
        pool_dv = cache_dsa_v[l].transpose(0, 2, 3, 1)
        pool_ki = jnp.swapaxes(cache_dsa_kidx[l], 1, 2)
        per_seq = lambda a: a.reshape(a.shape[0], dec_b, n_q).transpose(1, 0, 2)
        new_t = lambda a: jnp.pad(per_seq(a), ((0, 0), (0, 0), (0, PAGE_SIZE - n_q)))
        rows_q = lambda a: a.T.reshape(dec_b, n_q, a.shape[0])
        so_fox = _sample_fox(page_table, rows_q(ps["fqt"]), new_t(ps["fkt"]), new_t(ps["fvt"]),
                             new_t(ps["lft"]), pool_fk, pool_fv, pool_lft)
        half = SUBLANES // 2
        assert n_q <= half
        iqx = ps["iqt"].reshape(H_IDX, D_IDX, dec_b, n_q).transpose(2, 0, 3, 1)
        iqx = jnp.pad(iqx, ((0, 0), (0, 0), (0, half - n_q), (0, 0))).reshape(dec_b, H_IDX * half, D_IDX)
        iwx = jnp.pad(per_seq(ps["iwt"]), ((0, 0), (0, 0), (0, half - n_q)))
        iwx = iwx.reshape(dec_b, H_IDX * half, 1)
        sel_bias, sel_bias_new = _sample_index(page_table, iqx, iwx, new_t(ps["ikt"]), pool_ki, n_q)
        so_dsa = _sample_dsa(page_table, rows_q(ps["dqt"]), new_t(ps["dkt"]), new_t(ps["dvt"]), sel_bias,
                             sel_bias_new, pool_dk, pool_dv)

        cin = jnp.zeros((1, LANES), F32)
        x1_p, e_p, gate_p, rank_p, cnt_p = _tail(o_fox, o_dsa, xp, wo, g1, b1, wr, br, cin, alpha, tm=256)
        x1_s, e_s, gate_s, rank_s, cnt = _tail(so_fox.reshape(rows_s, W_FOX), so_dsa.reshape(rows_s, W_DSA),
                                               xs_, wo, g1, b1, wr, br, cnt_p, alpha, tm=rows_s)

        bs = EXPERT_BLOCK
        counts = cnt[0, :N_EXPERTS].astype(I32)
        padded = (counts + bs - 1) // bs * bs
        pad_end = jnp.cumsum(padded)
        pad_start = pad_end - padded
        n_assign = (rows_p + rows_s) * TOP_K_EXPERTS
        n_blocks = -(-(n_assign + N_EXPERTS * (bs - 1)) // bs)
        blk = jnp.arange(n_blocks, dtype=I32)
        n_used = (pad_end[-1] // bs).astype(I32)
        block_e = jnp.sum((pad_end[None, :] <= (blk * bs)[:, None]).astype(I32), axis=1)
        block_e = jnp.minimum(block_e, N_EXPERTS - 1)
        block_e = jnp.where(blk < n_used, block_e, block_e[jnp.maximum(n_used - 1, 0)])
        is_first = jnp.concatenate([jnp.ones((1,), I32), (block_e[1:] != block_e[:-1]).astype(I32)])
        change_at = jnp.where(is_first == 1, blk, n_blocks)
        next_change = jnp.concatenate([lax.cummin(change_at[::-1])[::-1][1:], jnp.full((1,), n_blocks, I32)])
        next_e = jnp.where(next_change < n_blocks, block_e[jnp.minimum(next_change, n_blocks - 1)], -1).astype(I32)
        slot = ((jnp.cumsum(is_first) - 1) % 2).astype(I32)
        def slots(e, rank):
            flat = lambda a: a[:, :TOP_K_EXPERTS].reshape(-1, LANES)
            return (pad_start[flat(e)] + flat(rank)).reshape(-1)

        dest_p, dest_s = slots(e_p, rank_p), slots(e_s, rank_s)

        xs_sorted = jnp.zeros((n_blocks * bs, d), F32)
        xs_sorted = _dispatch(dest_p, x1_p, xs_sorted)
        xs_sorted = _dispatch(dest_s, x1_s, xs_sorted)
        ys = _expert_ffn(block_e, is_first, n_used.reshape(1), next_e, slot, xs_sorted, w_gate[l], w_up[l], w_down[l],
                         b_gate[l].reshape(N_EXPERTS, 1, -1), b_up[l].reshape(N_EXPERTS, 1, -1),
                         b_down[l].reshape(N_EXPERTS, 1, -1))
        xp_new = _combine(dest_p, ys, gate_p, x1_p, g2, b2, alpha)
        xs_new = _combine(dest_s, ys, gate_s, x1_s, g2, b2, alpha)

        heads_p = lambda a: a.reshape(batch, -1, HEAD_DIM, seq).transpose(0, 3, 1, 2)
        heads_s = lambda a: a.reshape(-1, HEAD_DIM, dec_b, n_q).transpose(2, 3, 0, 1)
        for acc, a in zip(outs_p, (heads_p(pp["fkt"]), heads_p(pp["fvt"]), pp["lft"].transpose(0, 2, 1),
                                   heads_p(pp["dkt"]), heads_p(pp["dvt"]), pp["ikt"].transpose(0, 2, 1))):
            acc.append(a)
        for acc, a in zip(outs_s, (heads_s(ps["fkt"]), heads_s(ps["fvt"]), per_seq(ps["lft"]).transpose(0, 2, 1),
                                   heads_s(ps["dkt"]), heads_s(ps["dvt"]), per_seq(ps["ikt"]).transpose(0, 2, 1))):
            acc.append(a)
        xp, xs_ = xp_new, xs_new

    stack = lambda lst: [jnp.stack(r, axis=0) for r in lst]
    return (xp.reshape(batch, seq, d), xs_.reshape(dec_b, n_q, d), *stack(outs_p), *stack(outs_s))
```

```python
import functools

import jax
import jax.numpy as jnp
from jax import lax
from jax.experimental import pallas as pl
from jax.experimental.pallas import tpu as pltpu

F32, BF16, I32 = jnp.float32, jnp.bfloat16, jnp.int32
NEG_INF = float("-inf")

HEAD_DIM = 64
H_FOX = 8
H_DSA = 8
W_FOX = H_FOX * HEAD_DIM
W_DSA = H_DSA * HEAD_DIM
H_IDX = 16
D_IDX = 64
TOPK_MAX = 256
PAGE_SIZE = 128
ROPE_THETA = 10000.0
ATTN_SCALE = HEAD_DIM ** -0.5
IDX_SCALE = D_IDX ** -0.5
IDX_W_SCALE = H_IDX ** -0.5
N_EXPERTS = 32
TOP_K_EXPERTS = 4
SWIGLU_LIMIT = 7.0
SWIGLU_ALPHA = 1.702
LN_EPS = 1e-5

LANES = 128
KEY_NEG_INF = -2139095041
VMEM_LIMIT = 56 * 1024 * 1024

NT_DIMS = (((1,), (1,)), ((), ()))


def _cparams(sem):
    return pltpu.CompilerParams(dimension_semantics=sem, vmem_limit_bytes=VMEM_LIMIT)


def _split3(x):
    hi = x.astype(BF16)
    r1 = x - hi.astype(F32)
    mid = r1.astype(BF16)
    lo = (r1 - mid.astype(F32)).astype(BF16)
    return hi, mid, lo


def _lane_cumsum(x, upper):
    hi, mid, lo = _split3(x)
    return (jnp.dot(hi, upper, preferred_element_type=F32)
            + jnp.dot(mid, upper, preferred_element_type=F32)
            + jnp.dot(lo, upper, preferred_element_type=F32))


def _upper_tri(n):
    r = lax.broadcasted_iota(I32, (n, n), 0)
    c = lax.broadcasted_iota(I32, (n, n), 1)
    return (r <= c).astype(BF16)


def _sort_key(x):
    bits = pltpu.bitcast(x, I32)
    return bits ^ ((bits >> 31) & 0x7FFFFFFF)


C_FQ, C_FK, C_FV, C_DQ, C_DK, C_DV, C_IQ, C_SM = 0, 512, 1024, 1536, 2048, 2560, 3072, 4096
N_PROJ = 4224
SM_IK, SM_LF, SM_IW = 0, 64, 72


def _in_proj_kernel(x_ref, w_ref, cos_ref, sin_ref, bf_ref,
                    fqt_ref, fkb_ref, fkt_ref, fvt_ref, fvtc_ref, cb_ref, lft_ref,
                    dqt_ref, dkb_ref, dkt_ref, dvt_ref, dvtc_ref, iqt_ref, ikb_ref, ikt_ref, iwt_ref,
                    carry_ref, *, tiles_per_seq):
    @pl.when(pl.program_id(0) % tiles_per_seq == 0)
    def _():
        carry_ref[...] = jnp.zeros(carry_ref.shape, F32)

    xb = x_ref[...].astype(BF16)
    tm = xb.shape[0]
    cos = cos_ref[...]
    sin = sin_ref[...]
    lane = lax.broadcasted_iota(I32, (tm, LANES), 1)
    first_half = (lane & (HEAD_DIM - 1)) < HEAD_DIM // 2

    def mm(c0, n):
        return jnp.dot(xb, w_ref[:, c0:c0 + n], preferred_element_type=F32)

    def rope(y):
        outs = []
        for c in range(0, y.shape[1], LANES):
            yc = y[:, c:c + LANES]
            partner = jnp.where(first_half, pltpu.roll(yc, LANES - HEAD_DIM // 2, 1),
                                pltpu.roll(yc, HEAD_DIM // 2, 1))
            outs.append(yc * cos + partner * sin)
        return outs[0] if len(outs) == 1 else jnp.concatenate(outs, axis=1)

    y = mm(C_FQ, W_FOX)
    fqt_ref[...] = (y * ATTN_SCALE).T.astype(BF16)
    y = mm(C_FK, W_FOX)
    fkb_ref[...] = y.astype(BF16)
    fkt_ref[...] = y.T
    yt = mm(C_FV, W_FOX).T
    fvt_ref[...] = yt
    fvtc_ref[...] = yt.astype(BF16)
    y = rope(mm(C_DQ, W_DSA))
    dqt_ref[...] = (y * ATTN_SCALE).T.astype(BF16)
    y = rope(mm(C_DK, W_DSA))
    dkb_ref[...] = y.astype(BF16)
    dkt_ref[...] = y.T
    yt = mm(C_DV, W_DSA).T
    dvt_ref[...] = yt
    dvtc_ref[...] = yt.astype(BF16)
    y = rope(mm(C_IQ, H_IDX * D_IDX))
    iqt_ref[...] = (y * IDX_SCALE).T.astype(BF16)
    z = mm(C_SM, LANES)
    r = rope(z)
    ikb_ref[...] = r[:, SM_IK:SM_IK + D_IDX].astype(BF16)
    zb = z + bf_ref[...]
    lsg = -(jnp.maximum(-zb, 0.0) + jnp.log1p(jnp.exp(-jnp.abs(zb))))
    small_t = jnp.where(lane < SM_LF, r, jnp.where(lane < SM_IW, lsg, z * IDX_W_SCALE)).T
    ikt_ref[...] = small_t[SM_IK:SM_IK + D_IDX, :]
    lft_ref[...] = small_t[SM_LF:SM_LF + H_FOX, :]
    iwt_ref[...] = small_t[SM_IW:SM_IW + H_IDX, :]

    hi, mid, lo = _split3(jnp.where((lane >= SM_LF) & (lane < SM_IW), lsg, 0.0))
    rr = lax.broadcasted_iota(I32, (tm, tm), 0)
    cc = lax.broadcasted_iota(I32, (tm, tm), 1)
    lower = (cc <= rr).astype(BF16)
    parts = jnp.dot(lower, jnp.concatenate([hi, mid, lo], axis=1), preferred_element_type=F32)
    cs = parts[:, :LANES] + parts[:, LANES:2 * LANES] + parts[:, 2 * LANES:] + carry_ref[...]
    carry_ref[...] = cs[tm - 1:tm, :]
    for h in range(H_FOX):
        cb_ref[h] = jnp.broadcast_to(cs[:, SM_LF + h:SM_LF + h + 1], (tm, LANES))


FOX_KV_CHUNK = 256
DSA_KV_CHUNK = 512
VALUE_BISECTION_ROUNDS = 14


def _in_proj(x2d, wp, bf_row, cos_t, sin_t, n_seq, seq, tm):
    rows, d = x2d.shape
    tps = seq // tm
    ckf, ckd = min(FOX_KV_CHUNK, seq), min(DSA_KV_CHUNK, seq)
    n_tab = cos_t.shape[0] // tm
    t_spec = lambda w: pl.BlockSpec((None, w, tm), lambda i: (i // tps, 0, i % tps))
    n_spec = lambda w: pl.BlockSpec((tm, w), lambda i: (i, 0))
    c_spec = lambda w, ck: pl.BlockSpec((None, None, w, tm),
                                        lambda i: (i // tps, (i % tps) * tm // ck, 0, (i % tps) % (ck // tm)))
    t_shape = lambda w, dt: jax.ShapeDtypeStruct((n_seq, w, seq), dt)
    n_shape = lambda w, dt: jax.ShapeDtypeStruct((rows, w), dt)
    c_shape = lambda w, ck: jax.ShapeDtypeStruct((n_seq, seq // ck, w, ck), BF16)
    wi = H_IDX * D_IDX
    outs = [("fqt", t_spec(W_FOX), t_shape(W_FOX, BF16)), ("fkb", n_spec(W_FOX), n_shape(W_FOX, BF16)),
            ("fkt", t_spec(W_FOX), t_shape(W_FOX, F32)), ("fvt", t_spec(W_FOX), t_shape(W_FOX, F32)),
            ("fvtc", c_spec(W_FOX, ckf), c_shape(W_FOX, ckf)),
            ("cb", pl.BlockSpec((H_FOX, tm, LANES), lambda i: (0, i, 0)),
             jax.ShapeDtypeStruct((H_FOX, rows, LANES), F32)),
            ("lft", t_spec(H_FOX), t_shape(H_FOX, F32)),
            ("dqt", t_spec(W_DSA), t_shape(W_DSA, BF16)), ("dkb", n_spec(W_DSA), n_shape(W_DSA, BF16)),
            ("dkt", t_spec(W_DSA), t_shape(W_DSA, F32)), ("dvt", t_spec(W_DSA), t_shape(W_DSA, F32)),
            ("dvtc", c_spec(W_DSA, ckd), c_shape(W_DSA, ckd)),
            ("iqt", t_spec(wi), t_shape(wi, BF16)), ("ikb", n_spec(D_IDX), n_shape(D_IDX, BF16)),
            ("ikt", t_spec(D_IDX), t_shape(D_IDX, F32)), ("iwt", t_spec(H_IDX), t_shape(H_IDX, F32))]
    res = pl.pallas_call(
        functools.partial(_in_proj_kernel, tiles_per_seq=tps),
        grid=(rows // tm,),
        in_specs=[pl.BlockSpec((tm, d), lambda i: (i, 0)),
                  pl.BlockSpec((d, N_PROJ), lambda i: (0, 0)),
                  pl.BlockSpec((tm, LANES), lambda i: (i % n_tab, 0)),
                  pl.BlockSpec((tm, LANES), lambda i: (i % n_tab, 0)),
                  pl.BlockSpec((1, LANES), lambda i: (0, 0))],
        out_specs=[o[1] for o in outs],
        out_shape=[o[2] for o in outs],
        scratch_shapes=[pltpu.VMEM((1, LANES), F32)],
        compiler_params=_cparams(("arbitrary",)),
        name="in_proj",
    )(x2d, wp, cos_t, sin_t, bf_row)
    return {o[0]: r for o, r in zip(outs, res)}


SUBLANES = 8


def _over_sublanes(x8, reduce_fn):
    return jnp.broadcast_to(reduce_fn(x8, axis=0, keepdims=True), x8.shape)


def _by_sublane_tile(x):
    return x.reshape(x.shape[0] // SUBLANES, SUBLANES, x.shape[1])


def _flash_step(n_heads, score_fn, vt_fn, st):
    m_ref, l_ref, acc_ref, _, s_ref, p_ref, alpha_ref = st
    for h in range(n_heads):
        s_ref[h] = score_fn(h)
    for h in range(n_heads):
        s3 = _by_sublane_tile(s_ref[h])
        m_old = m_ref[h]
        m_new = jnp.maximum(m_old, _over_sublanes(jnp.max(s3, axis=0), jnp.max))
        alpha = jnp.exp(m_old - m_new)
        p3 = jnp.exp(s3 - m_new[None])
        l_ref[h] = alpha * l_ref[h] + _over_sublanes(jnp.sum(p3, axis=0), jnp.sum)
        p_ref[h] = p3.reshape(s_ref.shape[1:]).astype(BF16)
        alpha_ref[h] = alpha
        m_ref[h] = m_new
    for h in range(n_heads):
        pv = jnp.dot(vt_fn(h), p_ref[h], preferred_element_type=F32)
        acc_ref[h] = (_by_sublane_tile(acc_ref[h]) * alpha_ref[h][None]).reshape(acc_ref.shape[1:]) + pv


def _flash_init(m_ref, l_ref, acc_ref):
    m_ref[...] = jnp.full(m_ref.shape, -1e30, F32)
    l_ref[...] = jnp.zeros(l_ref.shape, F32)
    acc_ref[...] = jnp.zeros(acc_ref.shape, F32)


def _flash_finish(o_ref, l_ref, acc_ref, n_heads):
    tq = acc_ref.shape[2]
    for pair in range(n_heads // 2):
        parts = [(_by_sublane_tile(acc_ref[h]) / l_ref[h][None]).reshape(HEAD_DIM, tq)
                 for h in (2 * pair, 2 * pair + 1)]
        o_ref[:, pair * LANES:(pair + 1) * LANES] = jnp.concatenate(parts, axis=0).T.astype(BF16)


def _flash_scratch(n_heads, tk, tq):
    return [pltpu.VMEM((n_heads, SUBLANES, tq), F32), pltpu.VMEM((n_heads, SUBLANES, tq), F32),
            pltpu.VMEM((n_heads, HEAD_DIM, tq), F32), pltpu.VMEM((n_heads, LANES, tq), BF16),
            pltpu.VMEM((n_heads, tk, tq), F32), pltpu.VMEM((n_heads, tk, tq), BF16),
            pltpu.VMEM((n_heads, SUBLANES, tq), F32)]


def _stage_pair_masked_q(qt_ref, qpm_ref, n_heads):
    for h in range(n_heads):
        pair = h // 2
        rows = qt_ref[pair * LANES:(pair + 1) * LANES, :].astype(F32)
        r = lax.broadcasted_iota(I32, rows.shape, 0)
        qpm_ref[h] = jnp.where(r // HEAD_DIM == h % 2, rows, 0.0).astype(BF16)


def _fox_kernel(qt_ref, k_ref, vt_ref, cb_ref, o_ref, *st, tq):
    m_ref, l_ref, acc_ref, qpm_ref = st[:4]
    i = pl.program_id(1)
    _stage_pair_masked_q(qt_ref, qpm_ref, H_FOX)
    _flash_init(m_ref, l_ref, acc_ref)
    key_row = lax.broadcasted_iota(I32, (tq, tq), 0)
    query_col = lax.broadcasted_iota(I32, (tq, tq), 1)
    visible = key_row <= query_col

    def kv_step(j, masked):
        start = pl.multiple_of(j * tq, tq)

        def scores(h):
            pair = h // 2
            s = jnp.dot(k_ref[pl.ds(start, tq), pair * LANES:(pair + 1) * LANES], qpm_ref[h],
                        preferred_element_type=F32)
            s = s - jnp.concatenate([cb_ref[h, pl.ds(start, tq), :]] * (tq // LANES), axis=1)
            return jnp.where(visible, s, NEG_INF) if masked else s

        _flash_step(H_FOX, scores, lambda h: vt_ref[j, h * HEAD_DIM:(h + 1) * HEAD_DIM, :], st)

    def body(j, c):
        kv_step(j, False)
        return c

    lax.fori_loop(0, i, body, 0)
    kv_step(i, True)
    _flash_finish(o_ref, l_ref, acc_ref, H_FOX)


def _fox_prompt(qt, kb, vtc, cb, batch, seq):
    tq = FOX_KV_CHUNK
    nq = seq // tq
    return pl.pallas_call(
        functools.partial(_fox_kernel, tq=tq),
        grid=(batch, nq),
        in_specs=[pl.BlockSpec((None, W_FOX, tq), lambda b, i: (b, 0, i)),
                  pl.BlockSpec((seq, W_FOX), lambda b, i: (b, 0)),
                  pl.BlockSpec((None, nq, W_FOX, tq), lambda b, i: (b, 0, 0, 0)),
                  pl.BlockSpec((H_FOX, seq, LANES), lambda b, i: (0, b, 0))],
        out_specs=pl.BlockSpec((tq, W_FOX), lambda b, i: (b * nq + i, 0)),
        out_shape=jax.ShapeDtypeStruct((batch * seq, W_FOX), BF16),
        scratch_shapes=_flash_scratch(H_FOX, tq, tq),
        compiler_params=_cparams(("arbitrary", "arbitrary")),
        name="fox_prompt",
    )(qt, kb, vtc, cb)


def _search_threshold(count_ge, lo0, hi0, topk):
    def key_to_value(k):
        return pltpu.bitcast(k ^ ((k >> 31) & 0x7FFFFFFF), F32)

    def key_midpoint(lo, hi):
        return (lo >> 1) + (hi >> 1) + (lo & hi & 1)

    def guarded(lo, hi, value):
        mid = _sort_key(value)
        return jnp.where((mid > lo) & (mid < hi), mid, key_midpoint(lo, hi))

    def refine(lo, hi, n_lo, n_hi, mid):
        n_mid = count_ge(mid)
        take = n_mid >= topk
        return (jnp.where(take, mid, lo), jnp.where(take, hi, mid),
                jnp.where(take, n_mid, n_lo), jnp.where(take, n_hi, n_mid))

    def search_round(state):
        it, lo, hi, n_lo, n_hi, _ = state
        by_value = it < VALUE_BISECTION_ROUNDS
        for _ in range(2):
            mid_v = guarded(lo, hi, 0.5 * key_to_value(lo) + 0.5 * key_to_value(hi))
            lo, hi, n_lo, n_hi = refine(lo, hi, n_lo, n_hi, jnp.where(by_value, mid_v, key_midpoint(lo, hi)))
        active = (n_lo > topk) & (hi != lo + 1)
        return it + 1, lo, hi, n_lo, n_hi, jnp.max(active.astype(F32))

    def searching(state):
        return (state[0] < VALUE_BISECTION_ROUNDS + 17) & (state[5] > 0)

    n0 = count_ge(lo0)
    return lax.while_loop(searching, search_round,
                          (jnp.int32(0), lo0, hi0, n0, jnp.zeros_like(n0), jnp.max((n0 > topk).astype(F32))))[1]


def _dsa_kernel(iqt_ref, iwt_ref, ik_ref, qt_ref, k_ref, vt_ref, o_ref, key_ref, bias_ref, *st,
                tq, ck, topk):
    m_ref, l_ref, acc_ref, qpm_ref = st[:4]
    i = pl.program_id(1)
    q0 = i * tq
    n_chunks = (q0 + tq + ck - 1) // ck
    key_row = lax.broadcasted_iota(I32, (ck, tq), 0)
    query_pos = q0 + lax.broadcasted_iota(I32, (ck, tq), 1)
    iw8 = [jnp.broadcast_to(iwt_ref[h:h + 1, :], (SUBLANES, tq)) for h in range(H_IDX)]

    def score_chunk(c, _):
        start = pl.multiple_of(c * ck, ck)
        ikc = ik_ref[pl.ds(start, ck), :]
        sc = jnp.zeros((ck // SUBLANES, SUBLANES, tq), F32)
        for h in range(H_IDX):
            rel = jnp.dot(ikc, iqt_ref[h * D_IDX:(h + 1) * D_IDX, :], preferred_element_type=F32)
            sc = sc + _by_sublane_tile(jnp.maximum(rel, 0.0)) * iw8[h][None]
        sc = jnp.where(key_row + start <= query_pos, sc.reshape(ck, tq), NEG_INF)
        key_ref[c] = _sort_key(sc)
        return 0

    lax.fori_loop(0, n_chunks, score_chunk, 0)

    def count(pred):
        def body(c, acc):
            return acc + jnp.sum(pred(_by_sublane_tile(key_ref[c])).astype(F32), axis=0)
        return _over_sublanes(lax.fori_loop(0, n_chunks, body, jnp.zeros((SUBLANES, tq), F32)), jnp.sum)

    def extreme(pick, reduce_fn, fill):
        def body(c, acc):
            k = _by_sublane_tile(key_ref[c])
            return pick(acc, reduce_fn(jnp.where(k > KEY_NEG_INF, k, fill), axis=0))
        init = jnp.full((SUBLANES, tq), fill, I32)
        return _over_sublanes(lax.fori_loop(0, n_chunks, body, init), reduce_fn)

    lo0 = extreme(jnp.minimum, jnp.min, jnp.int32(2 ** 31 - 1))
    hi0 = extreme(jnp.maximum, jnp.max, jnp.int32(KEY_NEG_INF)) + 1

    thr = _search_threshold(lambda t: count(lambda k: k >= t[None]), lo0, hi0, topk)
    need = topk - count(lambda k: k > thr[None])
    ties = count(lambda k: k == thr[None])
    excess = jnp.max(jnp.where((ties > need) & (thr > KEY_NEG_INF), 1.0, 0.0)) > 0.0

    @pl.when(jnp.logical_not(excess))
    def _():
        def fill(c, _):
            k = _by_sublane_tile(key_ref[c])
            bias_ref[c] = jnp.where((k >= thr[None]) & (k > KEY_NEG_INF), 0.0, NEG_INF).reshape(ck, tq)
            return 0
        lax.fori_loop(0, n_chunks, fill, 0)

    @pl.when(excess)
    def _():
        rr = lax.broadcasted_iota(I32, (ck, ck), 0)
        cc = lax.broadcasted_iota(I32, (ck, ck), 1)
        strict_lower = (cc < rr).astype(BF16)

        def fill(c, before):
            k = _by_sublane_tile(key_ref[c])
            tie = k == thr[None]
            tie_f = tie.astype(F32)
            earlier = jnp.dot(strict_lower, tie_f.reshape(ck, tq).astype(BF16), preferred_element_type=F32)
            earlier = _by_sublane_tile(earlier) + before[None]
            sel = (k > thr[None]) | (tie & (earlier < need[None]))
            bias_ref[c] = jnp.where(sel & (k > KEY_NEG_INF), 0.0, NEG_INF).reshape(ck, tq)
            return before + _over_sublanes(jnp.sum(tie_f, axis=0), jnp.sum)
        lax.fori_loop(0, n_chunks, fill, jnp.zeros((SUBLANES, tq), F32))

    _stage_pair_masked_q(qt_ref, qpm_ref, H_DSA)
    _flash_init(m_ref, l_ref, acc_ref)

    def kv_step(c, carry):
        start = pl.multiple_of(c * ck, ck)
        def scores(h):
            pair = h // 2
            return jnp.dot(k_ref[pl.ds(start, ck), pair * LANES:(pair + 1) * LANES], qpm_ref[h],
                           preferred_element_type=F32) + bias_ref[c]

        _flash_step(H_DSA, scores, lambda h: vt_ref[c, h * HEAD_DIM:(h + 1) * HEAD_DIM, :], st)
        return carry

    lax.fori_loop(0, n_chunks, kv_step, 0)
    _flash_finish(o_ref, l_ref, acc_ref, H_DSA)


def _dsa_prompt(iqt, iwt, ikb, dqt, dkb, dvtc, batch, seq, tq):
    ck = dvtc.shape[3]
    nq = seq // tq
    topk = min(TOPK_MAX, seq // 4)
    return pl.pallas_call(
        functools.partial(_dsa_kernel, tq=tq, ck=ck, topk=topk),
        grid=(batch, nq),
        in_specs=[pl.BlockSpec((None, H_IDX * D_IDX, tq), lambda b, i: (b, 0, i)),
                  pl.BlockSpec((None, H_IDX, tq), lambda b, i: (b, 0, i)),
                  pl.BlockSpec((seq, D_IDX), lambda b, i: (b, 0)),
                  pl.BlockSpec((None, W_DSA, tq), lambda b, i: (b, 0, i)),
                  pl.BlockSpec((seq, W_DSA), lambda b, i: (b, 0)),
                  pl.BlockSpec((None, seq // ck, W_DSA, ck), lambda b, i: (b, 0, 0, 0))],
        out_specs=pl.BlockSpec((tq, W_DSA), lambda b, i: (b * nq + i, 0)),
        out_shape=jax.ShapeDtypeStruct((batch * seq, W_DSA), BF16),
        scratch_shapes=[pltpu.VMEM((seq // ck, ck, tq), I32), pltpu.VMEM((seq // ck, ck, tq), F32)]
        + _flash_scratch(H_DSA, ck, tq),
        compiler_params=_cparams(("arbitrary", "arbitrary")),
        name="dsa_prompt",
    )(iqt, iwt, ikb, dqt, dkb, dvtc)


KV_PAGES_PER_STEP = 16
IDX_PAGES_PER_STEP = 32


def _head_mask(rows, width):
    r = lax.broadcasted_iota(I32, (rows, width), 0)
    c = lax.broadcasted_iota(I32, (rows, width), 1)
    return (c // HEAD_DIM) == (r % H_FOX)


def _block_diag_q(q, n_q):
    hm = _head_mask(H_FOX, q.shape[1])
    qf = q.astype(F32)
    parts = [jnp.where(hm, jnp.broadcast_to(qf[t:t + 1, :], hm.shape), 0.0) for t in range(n_q)]
    return jnp.concatenate(parts, axis=0).astype(BF16)


def _rows_to_groups(x, n_q):
    return jnp.concatenate([jnp.broadcast_to(x[t:t + 1, :], (H_FOX, x.shape[1])) for t in range(n_q)], axis=0)


DECODE_SPLITS = 2


def _softmax_page(g, s, vt_bf16, m_ref, l_ref, acc_ref):
    m = m_ref[g]
    m_new = jnp.maximum(m, jnp.max(s, axis=1, keepdims=True))
    alpha = jnp.exp(m - m_new)
    p = jnp.exp(s - m_new)
    l_ref[g] = alpha * l_ref[g] + jnp.sum(p, axis=1, keepdims=True)
    acc_ref[g] = alpha * acc_ref[g] + lax.dot_general(p.astype(BF16), vt_bf16, NT_DIMS,
                                                      preferred_element_type=F32)
    m_ref[g] = m_new


def _decode_scratch(rows, w):
    g = DECODE_SPLITS
    return [pltpu.VMEM((g, rows, 1), F32), pltpu.VMEM((g, rows, 1), F32), pltpu.VMEM((g, rows, w), F32)]


def _stack_pages(refs):
    return jnp.concatenate([r[...].reshape(-1, PAGE_SIZE) for r in refs], axis=1).astype(BF16)


def _finish_decode(o_ref, m_ref, l_ref, acc_ref, n_q):
    m = m_ref[0]
    for g in range(1, DECODE_SPLITS):
        m = jnp.maximum(m, m_ref[g])
    l = jnp.zeros_like(m)
    acc = jnp.zeros(acc_ref.shape[1:], F32)
    for g in range(DECODE_SPLITS):
        w = jnp.exp(m_ref[g] - m)
        l = l + w * l_ref[g]
        acc = acc + w * acc_ref[g]
    o = acc / l
    o = jnp.where(_head_mask(o.shape[0], o.shape[1]), o, 0.0)
    o_ref[...] = jnp.sum(o.reshape(n_q, H_FOX, o.shape[1]), axis=1).astype(BF16)


def _sample_fox_kernel(pt_ref, q_ref, kn_ref, vn_ref, lfn_ref, *rest, n_q, pps):
    k_refs, v_refs, lf_refs = rest[:pps], rest[pps:2 * pps], rest[2 * pps:3 * pps]
    o_ref, m_ref, l_ref, acc_ref, cc_ref, qbd_ref = rest[3 * pps:]
    step = pl.program_id(1)
    rows = n_q * H_FOX

    @pl.when(step == 0)
    def _():
        m_ref[...] = jnp.full(m_ref.shape, -1e30, F32)
        l_ref[...] = jnp.zeros(l_ref.shape, F32)
        acc_ref[...] = jnp.zeros(acc_ref.shape, F32)
        cc_ref[...] = jnp.zeros(cc_ref.shape, F32)
        qbd_ref[...] = _block_diag_q(q_ref[...], n_q)

    upper = _upper_tri(PAGE_SIZE)

    within = _lane_cumsum(jnp.concatenate([r[...] for r in lf_refs], axis=0), upper)
    off = cc_ref[...]
    parts = []
    for p in range(pps):
        cp = within[p * H_FOX:(p + 1) * H_FOX] + off
        parts.append(cp)
        off = cp[:, PAGE_SIZE - 1:PAGE_SIZE]
    cc_ref[...] = off
    per = pps // DECODE_SPLITS
    for g in range(DECODE_SPLITS):
        pages = slice(g * per, (g + 1) * per)
        cbias = jnp.concatenate(parts[pages], axis=1)
        s = jnp.dot(qbd_ref[...], _stack_pages(k_refs[pages]), preferred_element_type=F32)
        s = s - jnp.concatenate([cbias] * n_q, axis=0)
        _softmax_page(g, s, _stack_pages(v_refs[pages]), m_ref, l_ref, acc_ref)

    @pl.when(step == pl.num_programs(1) - 1)
    def _():
        r = lax.broadcasted_iota(I32, (rows, PAGE_SIZE), 0)
        c = lax.broadcasted_iota(I32, (rows, PAGE_SIZE), 1)
        cn = _lane_cumsum(lfn_ref[...], upper) + cc_ref[...]
        sn = jnp.dot(qbd_ref[...], kn_ref[...].astype(BF16), preferred_element_type=F32)
        sn = sn - jnp.concatenate([cn] * n_q, axis=0)
        sn = jnp.where(c <= r // H_FOX, sn, NEG_INF)
        _softmax_page(0, sn, vn_ref[...].astype(BF16), m_ref, l_ref, acc_ref)
        _finish_decode(o_ref, m_ref, l_ref, acc_ref, n_q)


def _page_spec(block, p, pps):
    nd = len(block)
    return pl.BlockSpec((None,) + block, lambda b, s, pt: (pt[b, s * pps + p],) + (0,) * nd)


def _sample_fox(page_table, q, k_new, v_new, lft_new, pool_k, pool_v, pool_lft):
    dec_b, n_q, w = q.shape
    n_pages = page_table.shape[1]
    pps = min(KV_PAGES_PER_STEP, n_pages)
    rows = n_q * H_FOX
    seq_spec = lambda blk: pl.BlockSpec((None,) + blk, lambda b, s, pt: (b,) + (0,) * len(blk))
    kv_page = (H_FOX, HEAD_DIM, PAGE_SIZE)
    in_specs = ([seq_spec((n_q, w)), seq_spec((w, PAGE_SIZE)), seq_spec((w, PAGE_SIZE)),
                 seq_spec((H_FOX, PAGE_SIZE))]
                + [_page_spec(kv_page, p, pps) for p in range(pps)]
                + [_page_spec(kv_page, p, pps) for p in range(pps)]
                + [_page_spec((H_FOX, PAGE_SIZE), p, pps) for p in range(pps)])
    return pl.pallas_call(
        functools.partial(_sample_fox_kernel, n_q=n_q, pps=pps),
        grid_spec=pltpu.PrefetchScalarGridSpec(
            num_scalar_prefetch=1,
            grid=(dec_b, n_pages // pps),
            in_specs=in_specs,
            out_specs=pl.BlockSpec((None, n_q, w), lambda b, s, pt: (b, 0, 0)),
            scratch_shapes=_decode_scratch(rows, w) + [pltpu.VMEM((H_FOX, 1), F32),
                                                       pltpu.VMEM((rows, w), BF16)]),
        out_shape=jax.ShapeDtypeStruct((dec_b, n_q, w), BF16),
        compiler_params=_cparams(("arbitrary", "arbitrary")),
        name="sample_fox",
    )(page_table, q, k_new, v_new, lft_new, *([pool_k] * pps), *([pool_v] * pps), *([pool_lft] * pps))


def _sample_index_kernel(pt_ref, iq_ref, iw_ref, kn_ref, *rest, pps, n_q, topk):
    k_refs = rest[:pps]
    bias_ref, biasn_ref, key_ref, keyn_ref = rest[pps:]
    step = pl.program_id(1)
    n_steps = key_ref.shape[0]
    iq = iq_ref[...]
    iw = iw_ref[...]

    def scores(kt):
        rel = jnp.dot(iq, kt, preferred_element_type=F32)
        w = iw * jnp.maximum(rel, 0.0)
        by_parity = jnp.sum(w.reshape(H_IDX // 2, SUBLANES, kt.shape[1]), axis=0)
        return by_parity + pltpu.roll(by_parity, SUBLANES // 2, 0)

    key_ref[step] = _sort_key(scores(_stack_pages(k_refs)))

    @pl.when(step == n_steps - 1)
    def _():
        r = lax.broadcasted_iota(I32, (H_FOX, PAGE_SIZE), 0)
        c = lax.broadcasted_iota(I32, (H_FOX, PAGE_SIZE), 1)
        keyn_ref[...] = _sort_key(jnp.where((c <= r) & (r < n_q), scores(kn_ref[...].astype(BF16)), NEG_INF))

        def count(pred):
            past = jnp.sum(jnp.sum(pred(key_ref[...]).astype(F32), axis=0), axis=1, keepdims=True)
            return past + jnp.sum(pred(keyn_ref[...]).astype(F32), axis=1, keepdims=True)

        def extreme(reduce_fn, fill):
            past = reduce_fn(reduce_fn(jnp.where(key_ref[...] > KEY_NEG_INF, key_ref[...], fill), axis=0),
                             axis=1, keepdims=True)
            new = reduce_fn(jnp.where(keyn_ref[...] > KEY_NEG_INF, keyn_ref[...], fill), axis=1, keepdims=True)
            return reduce_fn(jnp.concatenate([past, new], axis=1), axis=1, keepdims=True)

        lo0 = extreme(jnp.min, jnp.int32(2 ** 31 - 1))
        hi0 = extreme(jnp.max, jnp.int32(KEY_NEG_INF)) + 1
        thr = _search_threshold(lambda t: count(lambda k: k >= t), lo0, hi0, topk)
        need = topk - count(lambda k: k > thr)
        ties = count(lambda k: k == thr)
        excess = jnp.max(jnp.where((ties > need) & (thr > KEY_NEG_INF), 1.0, 0.0)) > 0.0

        @pl.when(jnp.logical_not(excess))
        def _():
            k = key_ref[...]
            bias_ref[...] = jnp.where((k >= thr) & (k > KEY_NEG_INF), 0.0, NEG_INF)
            k = keyn_ref[...]
            biasn_ref[...] = jnp.where((k >= thr) & (k > KEY_NEG_INF), 0.0, NEG_INF)

        @pl.when(excess)
        def _():
            upper = _upper_tri(PAGE_SIZE)

            def tie_block(k, before):
                tie = k == thr
                incl = jnp.dot(tie.astype(BF16), upper, preferred_element_type=F32) + before
                sel = (k > thr) | (tie & (incl - tie.astype(F32) < need))
                return jnp.where(sel & (k > KEY_NEG_INF), 0.0, NEG_INF), incl[:, PAGE_SIZE - 1:PAGE_SIZE]

            def fill(c, before):
                for p in range(pps):
                    lanes = slice(p * PAGE_SIZE, (p + 1) * PAGE_SIZE)
                    b, before = tie_block(key_ref[c, :, lanes], before)
                    bias_ref[c, :, lanes] = b
                return before

            before = lax.fori_loop(0, n_steps, fill, jnp.zeros((H_FOX, 1), F32))
            biasn_ref[...] = tie_block(keyn_ref[...], before)[0]


def _sample_index(page_table, iqx, iwx, kidx_new, pool_kidx, n_q):
    dec_b = iqx.shape[0]
    n_pages = page_table.shape[1]
    pps = min(IDX_PAGES_PER_STEP, n_pages)
    n_steps = n_pages // pps
    step_w = pps * PAGE_SIZE
    topk = min(TOPK_MAX, (n_pages * PAGE_SIZE + n_q) // 4)
    seq_spec = lambda blk: pl.BlockSpec((None,) + blk, lambda b, s, pt: (b,) + (0,) * len(blk))
    iq_rows = iqx.shape[1]
    in_specs = ([seq_spec((iq_rows, D_IDX)), seq_spec((iq_rows, 1)), seq_spec((D_IDX, PAGE_SIZE))]
                + [_page_spec((D_IDX, PAGE_SIZE), p, pps) for p in range(pps)])
    return pl.pallas_call(
        functools.partial(_sample_index_kernel, pps=pps, n_q=n_q, topk=topk),
        grid_spec=pltpu.PrefetchScalarGridSpec(
            num_scalar_prefetch=1,
            grid=(dec_b, n_steps),
            in_specs=in_specs,
            out_specs=[pl.BlockSpec((None, n_steps, H_FOX, step_w), lambda b, s, pt: (b, 0, 0, 0)),
                       pl.BlockSpec((None, H_FOX, PAGE_SIZE), lambda b, s, pt: (b, 0, 0))],
            scratch_shapes=[pltpu.VMEM((n_steps, H_FOX, step_w), I32), pltpu.VMEM((H_FOX, PAGE_SIZE), I32)]),
        out_shape=[jax.ShapeDtypeStruct((dec_b, n_steps, H_FOX, step_w), F32),
                   jax.ShapeDtypeStruct((dec_b, H_FOX, PAGE_SIZE), F32)],
        compiler_params=_cparams(("arbitrary", "arbitrary")),
        name="sample_index",
    )(page_table, iqx, iwx, kidx_new, *([pool_kidx] * pps))


def _sample_dsa_kernel(pt_ref, q_ref, kn_ref, vn_ref, bias_ref, biasn_ref, *rest, n_q, pps):
    k_refs, v_refs = rest[:pps], rest[pps:2 * pps]
    o_ref, m_ref, l_ref, acc_ref, qbd_ref = rest[2 * pps:]
    step = pl.program_id(1)

    @pl.when(step == 0)
    def _():
        m_ref[...] = jnp.full(m_ref.shape, -1e30, F32)
        l_ref[...] = jnp.zeros(l_ref.shape, F32)
        acc_ref[...] = jnp.zeros(acc_ref.shape, F32)
        qbd_ref[...] = _block_diag_q(q_ref[...], n_q)

    def attend(g, kt, vt, bias):
        s = jnp.dot(qbd_ref[...], kt, preferred_element_type=F32) + _rows_to_groups(bias, n_q)
        _softmax_page(g, s, vt, m_ref, l_ref, acc_ref)

    per = pps // DECODE_SPLITS
    for g in range(DECODE_SPLITS):
        pages = slice(g * per, (g + 1) * per)
        attend(g, _stack_pages(k_refs[pages]), _stack_pages(v_refs[pages]),
               bias_ref[0, :, g * per * PAGE_SIZE:(g + 1) * per * PAGE_SIZE])

    @pl.when(step == pl.num_programs(1) - 1)
    def _():
        attend(0, kn_ref[...].astype(BF16), vn_ref[...].astype(BF16), biasn_ref[...])
        _finish_decode(o_ref, m_ref, l_ref, acc_ref, n_q)


def _sample_dsa(page_table, q, k_new, v_new, bias, bias_new, pool_k, pool_v):
    dec_b, n_q, w = q.shape
    n_pages = page_table.shape[1]
    pps = min(KV_PAGES_PER_STEP, n_pages)
    rows = n_q * H_DSA
    kv_page = (H_DSA, HEAD_DIM, PAGE_SIZE)
    per_bias_chunk = bias.shape[3] // (pps * PAGE_SIZE)
    seq_spec = lambda blk: pl.BlockSpec((None,) + blk, lambda b, s, pt: (b,) + (0,) * len(blk))
    in_specs = ([seq_spec((n_q, w)), seq_spec((w, PAGE_SIZE)), seq_spec((w, PAGE_SIZE)),
                 pl.BlockSpec((None, 1, H_FOX, pps * PAGE_SIZE),
                              lambda b, s, pt: (b, s // per_bias_chunk, 0, s % per_bias_chunk)),
                 seq_spec((H_FOX, PAGE_SIZE))]
                + [_page_spec(kv_page, p, pps) for p in range(pps)]
                + [_page_spec(kv_page, p, pps) for p in range(pps)])
    return pl.pallas_call(
        functools.partial(_sample_dsa_kernel, n_q=n_q, pps=pps),
        grid_spec=pltpu.PrefetchScalarGridSpec(
            num_scalar_prefetch=1,
            grid=(dec_b, n_pages // pps),
            in_specs=in_specs,
            out_specs=pl.BlockSpec((None, n_q, w), lambda b, s, pt: (b, 0, 0)),
            scratch_shapes=_decode_scratch(rows, w) + [pltpu.VMEM((rows, w), BF16)]),
        out_shape=jax.ShapeDtypeStruct((dec_b, n_q, w), BF16),
        compiler_params=_cparams(("arbitrary", "arbitrary")),
        name="sample_dsa",
    )(page_table, q, k_new, v_new, bias, bias_new, *([pool_k] * pps), *([pool_v] * pps))


ROUTE_LANES = 8


def _layer_norm(r, g, b):
    mu = jnp.mean(r, axis=1, keepdims=True)
    d = r - mu
    var = jnp.mean(d * d, axis=1, keepdims=True)
    return d * lax.rsqrt(var + LN_EPS) * g + b


def _tail_kernel(of_ref, od_ref, x_ref, wo_ref, g_ref, b_ref, wr_ref, br_ref, cin_ref,
                 x1_ref, e_ref, gate_ref, rank_ref, cnt_ref, carry_ref, *, alpha):
    i = pl.program_id(0)
    tm = x_ref.shape[0]

    @pl.when(i == 0)
    def _():
        carry_ref[...] = cin_ref[...]

    r = (alpha * x_ref[...]
         + jnp.dot(of_ref[...], wo_ref[0:W_FOX, :], preferred_element_type=F32)
         + jnp.dot(od_ref[...], wo_ref[W_FOX:W_FOX + W_DSA, :], preferred_element_type=F32))
    x1 = _layer_norm(r, g_ref[...], b_ref[...])
    x1_ref[...] = x1

    logits = jnp.dot(x1.astype(BF16), wr_ref[...], preferred_element_type=F32) + br_ref[...]
    lane = lax.broadcasted_iota(I32, (tm, LANES), 1)
    lane_f = lane.astype(F32)
    vals, idxs, hots = [], [], []
    lg = logits
    for _ in range(TOP_K_EXPERTS):
        m = jnp.max(lg, axis=1, keepdims=True)
        idx = jnp.min(jnp.where(lg == m, lane_f, float(LANES)), axis=1, keepdims=True)
        hot = lane_f == idx
        vals.append(m)
        idxs.append(idx)
        hots.append(hot)
        lg = jnp.where(hot, NEG_INF, lg)
    exps = [jnp.exp(v - vals[0]) for v in vals]
    den = exps[0]
    for e in exps[1:]:
        den = den + e
    sel = jnp.zeros((tm, LANES), F32)
    for hot in hots:
        sel = sel + hot.astype(F32)
    rr = lax.broadcasted_iota(I32, (tm, tm), 0)
    cc = lax.broadcasted_iota(I32, (tm, tm), 1)
    strict_lower = (cc < rr).astype(BF16)
    before = jnp.dot(strict_lower, sel.astype(BF16), preferred_element_type=F32) + carry_ref[...]
    e_out = jnp.zeros((tm, LANES), F32)
    g_out = jnp.zeros((tm, LANES), F32)
    r_out = jnp.zeros((tm, LANES), F32)
    for j in range(TOP_K_EXPERTS):
        rank = jnp.sum(jnp.where(hots[j], before, 0.0), axis=1, keepdims=True)
        e_out = jnp.where(lane == j, idxs[j], e_out)
        g_out = jnp.where(lane == j, exps[j] / den, g_out)
        r_out = jnp.where(lane == j, rank, r_out)
    e_ref[...] = e_out[:, :ROUTE_LANES].astype(I32)
    gate_ref[...] = g_out[:, :ROUTE_LANES]
    rank_ref[...] = r_out[:, :ROUTE_LANES].astype(I32)
    carry_ref[...] = carry_ref[...] + jnp.sum(sel, axis=0, keepdims=True)
    cnt_ref[...] = carry_ref[...]


def _tail(of, od, x2d, wo, g1, b1, wr, br, cin, alpha, tm):
    rows, d = x2d.shape
    row_spec = lambda w: pl.BlockSpec((tm, w), lambda i: (i, 0))
    full = lambda a: pl.BlockSpec(a.shape, lambda i: (0,) * a.ndim)
    return pl.pallas_call(
        functools.partial(_tail_kernel, alpha=alpha),
        grid=(rows // tm,),
        in_specs=[row_spec(W_FOX), row_spec(W_DSA), row_spec(d), full(wo), full(g1), full(b1),
                  full(wr), full(br), full(cin)],
        out_specs=[row_spec(d), row_spec(ROUTE_LANES), row_spec(ROUTE_LANES), row_spec(ROUTE_LANES),
                   pl.BlockSpec((1, LANES), lambda i: (0, 0))],
        out_shape=[jax.ShapeDtypeStruct((rows, d), F32), jax.ShapeDtypeStruct((rows, ROUTE_LANES), I32),
                   jax.ShapeDtypeStruct((rows, ROUTE_LANES), F32), jax.ShapeDtypeStruct((rows, ROUTE_LANES), I32),
                   jax.ShapeDtypeStruct((1, LANES), F32)],
        scratch_shapes=[pltpu.VMEM((1, LANES), F32)],
        compiler_params=_cparams(("arbitrary",)),
        name="tail_router",
    )(of, od, x2d, wo, g1, b1, wr, br, cin)


EXPERT_BLOCK = 256
DISPATCH_TOKENS = 128
COMBINE_UNROLL = 4


def _dispatch_kernel(dest_ref, x_ref, xs_in, xs_out, sem, *, td):
    del xs_in

    def row_copy(src_row, dst_row):
        return pltpu.make_async_copy(x_ref.at[pl.ds(src_row, 1), :], xs_out.at[pl.ds(dst_row, 1), :], sem)

    def issue(tt, c):
        for u in range(COMBINE_UNROLL):
            t = tt * COMBINE_UNROLL + u
            for j in range(TOP_K_EXPERTS):
                row_copy(t, dest_ref[t * TOP_K_EXPERTS + j]).start(priority=j % 2)
        return c

    lax.fori_loop(0, td // COMBINE_UNROLL, issue, 0)

    def drain(tt, c):
        for _ in range(COMBINE_UNROLL * TOP_K_EXPERTS):
            row_copy(0, 0).wait()
        return c

    lax.fori_loop(0, td // COMBINE_UNROLL, drain, 0)


def _dispatch(dest_flat, x1, xs):
    rows, d = x1.shape
    td = DISPATCH_TOKENS
    return pl.pallas_call(
        functools.partial(_dispatch_kernel, td=td),
        grid=(rows // td,),
        in_specs=[pl.BlockSpec((td * TOP_K_EXPERTS,), lambda i: (i,), memory_space=pltpu.SMEM),
                  pl.BlockSpec((td, d), lambda i: (i, 0)), pl.BlockSpec(memory_space=pl.ANY)],
        out_specs=pl.BlockSpec(memory_space=pl.ANY),
        out_shape=jax.ShapeDtypeStruct(xs.shape, xs.dtype),
        scratch_shapes=[pltpu.SemaphoreType.DMA(())],
        input_output_aliases={2: 0},
        compiler_params=_cparams(("arbitrary",)),
        name="moe_dispatch",
    )(dest_flat, x1, xs)


def _ffn_kernel(be_ref, first_ref, nused_ref, nexte_ref, slot_ref, xs_ref, wg_hbm, wu_hbm, wd_hbm,
                bg_ref, bu_ref, bd_ref, ys_ref, stage_ref, wgb_ref, wub_ref, wdb_ref, sem):
    b = pl.program_id(0)

    def weight_copies(e, slot):
        return [pltpu.make_async_copy(w.at[e], stage_ref.at[slot, k], sem.at[slot, k])
                for k, w in enumerate((wg_hbm, wu_hbm, wd_hbm))]

    @pl.when(b == 0)
    def _():
        for c in weight_copies(be_ref[0], 0):
            c.start()

    @pl.when(first_ref[b] == 1)
    def _():
        slot = slot_ref[b]
        for c in weight_copies(be_ref[b], slot):
            c.wait()

        @pl.when(nexte_ref[b] >= 0)
        def _():
            for c in weight_copies(nexte_ref[b], 1 - slot):
                c.start()

        wgb_ref[...] = stage_ref[slot, 0].astype(BF16)
        wub_ref[...] = stage_ref[slot, 1].astype(BF16)
        wdb_ref[...] = stage_ref[slot, 2].astype(BF16)

    @pl.when(b < nused_ref[0])
    def _():
        xb = xs_ref[...].astype(BF16)
        g = jnp.dot(xb, wgb_ref[...], preferred_element_type=F32) + bg_ref[...]
        u = jnp.dot(xb, wub_ref[...], preferred_element_type=F32) + bu_ref[...]
        g = jnp.minimum(g, SWIGLU_LIMIT)
        u = jnp.clip(u, -SWIGLU_LIMIT, SWIGLU_LIMIT)
        h = g * (1.0 / (1.0 + jnp.exp(-SWIGLU_ALPHA * g))) * (u + 1.0)
        ys_ref[...] = jnp.dot(h.astype(BF16), wdb_ref[...], preferred_element_type=F32) + bd_ref[...]

    @pl.when(b >= nused_ref[0])
    def _():
        ys_ref[...] = jnp.zeros(ys_ref.shape, F32)


def _expert_ffn(block_e, is_first, n_used, next_e, slot, xs, wg, wu, wd, bg, bu, bd):
    p, d = xs.shape
    bs = EXPERT_BLOCK
    de = wg.shape[2]
    assert d == de
    b_spec = lambda w: pl.BlockSpec((None, 1, w), lambda b, be, fi, nu, ne, sl: (be[b], 0, 0))
    hbm = pl.BlockSpec(memory_space=pl.ANY)
    return pl.pallas_call(
        _ffn_kernel,
        grid_spec=pltpu.PrefetchScalarGridSpec(
            num_scalar_prefetch=5,
            grid=(p // bs,),
            in_specs=[pl.BlockSpec((bs, d), lambda b, be, fi, nu, ne, sl: (b, 0)),
                      hbm, hbm, hbm, b_spec(de), b_spec(de), b_spec(d)],
            out_specs=pl.BlockSpec((bs, d), lambda b, be, fi, nu, ne, sl: (b, 0)),
            scratch_shapes=[pltpu.VMEM((2, 3, d, de), F32),
                            pltpu.VMEM((d, de), BF16), pltpu.VMEM((d, de), BF16), pltpu.VMEM((de, d), BF16),
                            pltpu.SemaphoreType.DMA((2, 3))]),
        out_shape=jax.ShapeDtypeStruct((p, d), F32),
        compiler_params=_cparams(("arbitrary",)),
        name="moe_ffn",
    )(block_e, is_first, n_used, next_e, slot, xs, wg, wu, wd, bg, bu, bd)


def _combine_kernel(dest_ref, dest_next_ref, ys_hbm, gate_ref, x1_ref, g_ref, b_ref, o_ref, buf_ref, sem,
                    *, tc, alpha):
    i = pl.program_id(0)
    slot = i % 2

    def row_copy(src_row, sl, j, t):
        return pltpu.make_async_copy(ys_hbm.at[pl.ds(src_row, 1), :], buf_ref.at[sl, j, pl.ds(t, 1), :],
                                     sem.at[sl])

    def issue(d_ref, sl):
        def body(tt, c):
            for u in range(COMBINE_UNROLL):
                t = tt * COMBINE_UNROLL + u
                for j in range(TOP_K_EXPERTS):
                    row_copy(d_ref[t * TOP_K_EXPERTS + j], sl, j, t).start(priority=j % 2)
            return c
        lax.fori_loop(0, tc // COMBINE_UNROLL, body, 0)

    @pl.when(i == 0)
    def _():
        issue(dest_ref, slot)

    @pl.when(i + 1 < pl.num_programs(0))
    def _():
        issue(dest_next_ref, 1 - slot)

    def drain(tt, c):
        for _ in range(COMBINE_UNROLL * TOP_K_EXPERTS):
            row_copy(0, slot, 0, 0).wait()
        return c

    lax.fori_loop(0, tc // COMBINE_UNROLL, drain, 0)

    gates = gate_ref[...]
    moe = gates[:, 0:1] * buf_ref[slot, 0]
    for j in range(1, TOP_K_EXPERTS):
        moe = moe + gates[:, j:j + 1] * buf_ref[slot, j]
    o_ref[...] = _layer_norm(alpha * x1_ref[...] + moe, g_ref[...], b_ref[...])


def _combine(dest_flat, ys, gates, x1, g2, b2, alpha):
    rows, d = x1.shape
    tc = DISPATCH_TOKENS
    last = rows // tc - 1
    return pl.pallas_call(
        functools.partial(_combine_kernel, tc=tc, alpha=alpha),
        grid=(rows // tc,),
        in_specs=[pl.BlockSpec((tc * TOP_K_EXPERTS,), lambda i: (i,), memory_space=pltpu.SMEM),
                  pl.BlockSpec((tc * TOP_K_EXPERTS,), lambda i: (jnp.minimum(i + 1, last),),
                               memory_space=pltpu.SMEM),
                  pl.BlockSpec(memory_space=pl.ANY),
                  pl.BlockSpec((tc, ROUTE_LANES), lambda i: (i, 0)),
                  pl.BlockSpec((tc, d), lambda i: (i, 0)),
                  pl.BlockSpec((1, d), lambda i: (0, 0)),
                  pl.BlockSpec((1, d), lambda i: (0, 0))],
        out_specs=pl.BlockSpec((tc, d), lambda i: (i, 0)),
        out_shape=jax.ShapeDtypeStruct((rows, d), F32),
        scratch_shapes=[pltpu.VMEM((2, TOP_K_EXPERTS, tc, d), F32), pltpu.SemaphoreType.DMA((2,))],
        compiler_params=_cparams(("arbitrary",)),
        name="moe_combine",
    )(dest_flat, dest_flat, ys, gates, x1, g2, b2)


def _prep_w_in(w, b_f):
    d = w.shape[0]
    sizes = (W_FOX, W_FOX, W_FOX, H_FOX, W_DSA, W_DSA, W_DSA, H_IDX * D_IDX, D_IDX, H_IDX)
    parts, acc = [], 0
    for s in sizes:
        parts.append(w[:, acc:acc + s])
        acc += s
    fq, fk, fv, fl, dq, dk, dv, iq, ik, iw = parts
    small = jnp.concatenate([ik, fl, iw, jnp.zeros((d, LANES - D_IDX - H_FOX - H_IDX), w.dtype)], axis=1)
    wp = jnp.concatenate([fq, fk, fv, dq, dk, dv, iq, small], axis=1).astype(BF16)
    bf_row = jnp.zeros((1, LANES), F32).at[0, SM_LF:SM_LF + H_FOX].set(b_f.astype(F32))
    return wp, bf_row


def _rope_tables(pos):
    half = HEAD_DIM // 2
    inv = ROPE_THETA ** (-jnp.arange(half, dtype=F32) / half)
    ang = pos.astype(F32)[:, None] * inv[None, :]
    cos, sin = jnp.cos(ang), jnp.sin(ang)
    cos_t = jnp.concatenate([cos, cos, cos, cos], axis=1)
    sin_t = jnp.concatenate([-sin, sin, -sin, sin], axis=1)
    return cos_t, sin_t


def kernel(x_prompt, x_sample, cache_fox_k, cache_fox_v, cache_fox_logf, cache_dsa_k, cache_dsa_v,
           cache_dsa_kidx, page_table, w_in, b_f, w_o, ln1_g, ln1_b, w_router, b_router, w_gate, b_gate,
           w_up, b_up, w_down, b_down, ln2_g, ln2_b):
    batch, seq, d = x_prompt.shape
    dec_b, n_q, _ = x_sample.shape
    depth = w_in.shape[0]
    n_pool = cache_fox_k.shape[1]
    n_pages = page_table.shape[1]
    past = n_pages * PAGE_SIZE
    alpha = (2.0 * depth) ** 0.25
    rows_p, rows_s = batch * seq, dec_b * n_q

    cos_p, sin_p = _rope_tables(jnp.arange(seq))
    cos_s, sin_s = _rope_tables(past + (jnp.arange(rows_s) % n_q))

    xp = x_prompt.reshape(rows_p, d)
    xs_ = x_sample.reshape(rows_s, d)
    outs_p = [[] for _ in range(6)]
    outs_s = [[] for _ in range(6)]
    for l in range(depth):
        wp, bf_row = _prep_w_in(w_in[l], b_f[l])
        wo = w_o[l].astype(BF16)
        wr = jnp.pad(w_router[l], ((0, 0), (0, LANES - N_EXPERTS))).astype(BF16)
        br = jnp.full((1, LANES), NEG_INF, F32).at[0, :N_EXPERTS].set(b_router[l].astype(F32))
        g1, b1 = ln1_g[l].reshape(1, d), ln1_b[l].reshape(1, d)
        g2, b2 = ln2_g[l].reshape(1, d), ln2_b[l].reshape(1, d)

        pp = _in_proj(xp, wp, bf_row, cos_p, sin_p, batch, seq, tm=256)
        o_fox = _fox_prompt(pp["fqt"], pp["fkb"], pp["fvtc"], pp["cb"], batch, seq)
        o_dsa = _dsa_prompt(pp["iqt"], pp["iwt"], pp["ikb"], pp["dqt"], pp["dkb"], pp["dvtc"], batch, seq,
                            tq=256)

        ps = _in_proj(xs_, wp, bf_row, cos_s, sin_s, 1, rows_s, tm=rows_s)
        ps = {k: v[0] for k, v in ps.items() if v.ndim == 3}
        pool_fk = cache_fox_k[l].transpose(0, 2, 3, 1)
        pool_fv = cache_fox_v[l].transpose(0, 2, 3, 1)
        pool_lft = jnp.swapaxes(cache_fox_logf[l], 1, 2)
        pool_dk = cache_dsa_k[l].transpose(0, 2, 3, 1)
        pool_dv = cache_dsa_v[l].transpose(0, 2, 3, 1)
        pool_ki = jnp.swapaxes(cache_dsa_kidx[l], 1, 2)
        per_seq = lambda a: a.reshape(a.shape[0], dec_b, n_q).transpose(1, 0, 2)
        new_t = lambda a: jnp.pad(per_seq(a), ((0, 0), (0, 0), (0, PAGE_SIZE - n_q)))
        rows_q = lambda a: a.T.reshape(dec_b, n_q, a.shape[0])
        so_fox = _sample_fox(page_table, rows_q(ps["fqt"]), new_t(ps["fkt"]), new_t(ps["fvt"]),
                             new_t(ps["lft"]), pool_fk, pool_fv, pool_lft)
        half = SUBLANES // 2
        assert n_q <= half
        iqx = ps["iqt"].reshape(H_IDX, D_IDX, dec_b, n_q).transpose(2, 0, 3, 1)
        iqx = jnp.pad(iqx, ((0, 0), (0, 0), (0, half - n_q), (0, 0))).reshape(dec_b, H_IDX * half, D_IDX)
        iwx = jnp.pad(per_seq(ps["iwt"]), ((0, 0), (0, 0), (0, half - n_q)))
        iwx = iwx.reshape(dec_b, H_IDX * half, 1)
        sel_bias, sel_bias_new = _sample_index(page_table, iqx, iwx, new_t(ps["ikt"]), pool_ki, n_q)
        so_dsa = _sample_dsa(page_table, rows_q(ps["dqt"]), new_t(ps["dkt"]), new_t(ps["dvt"]), sel_bias,
                             sel_bias_new, pool_dk, pool_dv)

        cin = jnp.zeros((1, LANES), F32)
        x1_p, e_p, gate_p, rank_p, cnt_p = _tail(o_fox, o_dsa, xp, wo, g1, b1, wr, br, cin, alpha, tm=256)
        x1_s, e_s, gate_s, rank_s, cnt = _tail(so_fox.reshape(rows_s, W_FOX), so_dsa.reshape(rows_s, W_DSA),
                                               xs_, wo, g1, b1, wr, br, cnt_p, alpha, tm=rows_s)

        bs = EXPERT_BLOCK
        counts = cnt[0, :N_EXPERTS].astype(I32)
        padded = (counts + bs - 1) // bs * bs
        pad_end = jnp.cumsum(padded)
        pad_start = pad_end - padded
        n_assign = (rows_p + rows_s) * TOP_K_EXPERTS
        n_blocks = -(-(n_assign + N_EXPERTS * (bs - 1)) // bs)
        blk = jnp.arange(n_blocks, dtype=I32)
        n_used = (pad_end[-1] // bs).astype(I32)
        block_e = jnp.sum((pad_end[None, :] <= (blk * bs)[:, None]).astype(I32), axis=1)
        block_e = jnp.minimum(block_e, N_EXPERTS - 1)
        block_e = jnp.where(blk < n_used, block_e, block_e[jnp.maximum(n_used - 1, 0)])
        is_first = jnp.concatenate([jnp.ones((1,), I32), (block_e[1:] != block_e[:-1]).astype(I32)])
        change_at = jnp.where(is_first == 1, blk, n_blocks)
        next_change = jnp.concatenate([lax.cummin(change_at[::-1])[::-1][1:], jnp.full((1,), n_blocks, I32)])
        next_e = jnp.where(next_change < n_blocks, block_e[jnp.minimum(next_change, n_blocks - 1)], -1).astype(I32)
        slot = ((jnp.cumsum(is_first) - 1) % 2).astype(I32)
        def slots(e, rank):
            flat = lambda a: a[:, :TOP_K_EXPERTS].reshape(-1, LANES)
            return (pad_start[flat(e)] + flat(rank)).reshape(-1)

        dest_p, dest_s = slots(e_p, rank_p), slots(e_s, rank_s)

        xs_sorted = jnp.zeros((n_blocks * bs, d), F32)
        xs_sorted = _dispatch(dest_p, x1_p, xs_sorted)
        xs_sorted = _dispatch(dest_s, x1_s, xs_sorted)
        ys = _expert_ffn(block_e, is_first, n_used.reshape(1), next_e, slot, xs_sorted, w_gate[l], w_up[l], w_down[l],
                         b_gate[l].reshape(N_EXPERTS, 1, -1), b_up[l].reshape(N_EXPERTS, 1, -1),
                         b_down[l].reshape(N_EXPERTS, 1, -1))
        xp_new = _combine(dest_p, ys, gate_p, x1_p, g2, b2, alpha)
        xs_new = _combine(dest_s, ys, gate_s, x1_s, g2, b2, alpha)

        heads_p = lambda a: a.reshape(batch, -1, HEAD_DIM, seq).transpose(0, 3, 1, 2)
        heads_s = lambda a: a.reshape(-1, HEAD_DIM, dec_b, n_q).transpose(2, 3, 0, 1)
        for acc, a in zip(outs_p, (heads_p(pp["fkt"]), heads_p(pp["fvt"]), pp["lft"].transpose(0, 2, 1),
                                   heads_p(pp["dkt"]), heads_p(pp["dvt"]), pp["ikt"].transpose(0, 2, 1))):
            acc.append(a)
        for acc, a in zip(outs_s, (heads_s(ps["fkt"]), heads_s(ps["fvt"]), per_seq(ps["lft"]).transpose(0, 2, 1),
                                   heads_s(ps["dkt"]), heads_s(ps["dvt"]), per_seq(ps["ikt"]).transpose(0, 2, 1))):
            acc.append(a)
        xp, xs_ = xp_new, xs_new

    stack = lambda lst: [jnp.stack(r, axis=0) for r in lst]
    return (xp.reshape(batch, seq, d), xs_.reshape(dec_b, n_q, d), *stack(outs_p), *stack(outs_s))
```
